```python
import math
import jax
import jax.numpy as jnp
from jax import lax
import numpy as np

D_MODEL = 4096
BATCH = 1
SEQ = 8192
DEPTH = 1

CHUNK = 64
CTX_CHUNKS = 8
BAND = (CTX_CHUNKS + 1) * CHUNK

LRU_WIDTH = D_MODEL
LRU_BLOCKS = 16
LRU_BLOCK_W = LRU_WIDTH // LRU_BLOCKS
CONV_W = 4
LRU_C = 8.0

ATT_HEADS = 16
ATT_HEAD_DIM = 128
ATT_WIDTH = ATT_HEADS * ATT_HEAD_DIM
REL_CLIP = 128

PLE_DIM = 256

EPS = 1e-6
NEG_INF = -1e30

IN_WIDTHS = [LRU_WIDTH, LRU_WIDTH, ATT_WIDTH, ATT_WIDTH, ATT_WIDTH, ATT_WIDTH, D_MODEL, D_MODEL]
IN_COLS = int(sum(IN_WIDTHS))
IN_SPLITS = [int(s) for s in np.cumsum(IN_WIDTHS)[:-1]]

kernel_name = 'hybrid_rglru_chunkattn_block'


def rms_norm(x, g):
    x32 = x.astype(jnp.float32)
    y = x32 * lax.rsqrt(jnp.mean(x32 * x32, axis=-1, keepdims=True) + EPS)
    return (y * g.astype(jnp.float32)).astype(x.dtype)


def causal_depthwise_conv(x, w, b):
    S = x.shape[1]
    xp = jnp.pad(x, ((0, 0), (CONV_W - 1, 0), (0, 0)))
    out = b + xp[:, 0:S] * w[0]
    for k in range(1, CONV_W):
        out = out + xp[:, k:k + S] * w[k]
    return out


def rg_lru(x, w_a, b_a, w_i, b_i, lam):
    B, S, W = x.shape
    xb = x.reshape(B, S, LRU_BLOCKS, LRU_BLOCK_W)
    r = jax.nn.sigmoid(jnp.einsum('bsnc,ncd->bsnd', xb, w_a) + b_a.reshape(LRU_BLOCKS, LRU_BLOCK_W)).reshape(B, S, W)
    i = jax.nn.sigmoid(jnp.einsum('bsnc,ncd->bsnd', xb, w_i) + b_i.reshape(LRU_BLOCKS, LRU_BLOCK_W)).reshape(B, S, W)
    log_a = (-LRU_C * r.astype(jnp.float32)) * jax.nn.softplus(-lam.astype(jnp.float32))
    a = jnp.exp(log_a)
    u = jnp.sqrt(-jnp.expm1(2.0 * log_a)) * (i * x).astype(jnp.float32)

    def step(h, au):
        a_t, u_t = au
        h = a_t * h + u_t
        return h, h

    h0 = jnp.zeros((B, W), jnp.float32)
    _, hs = lax.scan(step, h0, (a.transpose(1, 0, 2), u.transpose(1, 0, 2)))
    return hs.transpose(1, 0, 2).astype(x.dtype)


def chunked_rel_attention(q, k, v, rel_bias):
    B, S, _ = q.shape
    NC = S // CHUNK
    scale = ATT_HEAD_DIM ** -0.5
    qc = (q * scale).reshape(B, NC, CHUNK, ATT_HEADS, ATT_HEAD_DIM)
    kc = k.reshape(B, NC, CHUNK, ATT_HEADS, ATT_HEAD_DIM)
    vc = v.reshape(B, NC, CHUNK, ATT_HEADS, ATT_HEAD_DIM)
    pad = ((0, 0), (CTX_CHUNKS, 0), (0, 0), (0, 0), (0, 0))
    kp = jnp.pad(kc, pad)
    vp = jnp.pad(vc, pad)
    qi = jnp.arange(CHUNK)[:, None]
    kj = jnp.arange(BAND)[None, :]
    dist = jnp.clip(CTX_CHUNKS * CHUNK + qi - kj, -REL_CLIP, REL_CLIP) + REL_CLIP
    bias = rel_bias[:, dist].astype(jnp.float32)
    band_chunk = jnp.arange(BAND) // CHUNK

    def one_chunk(args):
        c, q_blk = args
        kb = lax.dynamic_slice_in_dim(kp, c, CTX_CHUNKS + 1, axis=1).reshape(B, BAND, ATT_HEADS, ATT_HEAD_DIM)
        vb = lax.dynamic_slice_in_dim(vp, c, CTX_CHUNKS + 1, axis=1).reshape(B, BAND, ATT_HEADS, ATT_HEAD_DIM)
        s = jnp.einsum('bqhd,bkhd->bhqk', q_blk, kb).astype(jnp.float32) + bias
        valid = (band_chunk + c) >= CTX_CHUNKS
        s = jnp.where(valid, s, NEG_INF)
        pr = jax.nn.softmax(s, axis=-1).astype(vb.dtype)
        return jnp.einsum('bhqk,bkhd->bqhd', pr, vb)

    out = lax.map(one_chunk, (jnp.arange(NC), qc.transpose(1, 0, 2, 3, 4)))
    return out.transpose(1, 0, 2, 3, 4).reshape(B, S, ATT_WIDTH)


def hybrid_layer(x, p_i, w_in, conv_w, conv_b, w_rg_a, b_rg_a, w_rg_i, b_rg_i, lru_lambda,
                 rel_bias, w_proj_a, w_proj_b, w_out, g_pre, g_post,
                 w_ple, w_ple_gate, g_ple_pre, g_ple_post):
    xn = rms_norm(x, g_pre)
    proj = xn @ w_in
    x_a, z_a, q, k, v, z_b, gate_a, gate_b = jnp.split(proj, IN_SPLITS, axis=-1)
    y_a = rg_lru(causal_depthwise_conv(x_a, conv_w, conv_b), w_rg_a, b_rg_a, w_rg_i, b_rg_i, lru_lambda) * jax.nn.silu(z_a)
    y_b = chunked_rel_attention(q, k, v, rel_bias) * jax.nn.silu(z_b)
    merged = jax.nn.sigmoid(gate_a) * (y_a @ w_proj_a) + jax.nn.sigmoid(gate_b) * (y_b @ w_proj_b)
    h = x + rms_norm(merged @ w_out, g_post)
    ple = rms_norm(p_i @ w_ple, g_ple_post) * jax.nn.sigmoid(rms_norm(h, g_ple_pre) @ w_ple_gate)
    return h + ple


def setup_inputs(seed: int = 0) -> dict:
    key = jax.random.key(seed)
    ks = jax.random.split(key, 22)
    f32 = jnp.float32

    def nrm(k, shape, scale):
        return jax.random.normal(k, shape, f32) * scale

    x = jax.random.normal(ks[0], (BATCH, SEQ, D_MODEL), f32)
    p = jax.random.normal(ks[1], (DEPTH, BATCH, SEQ, PLE_DIM), f32)
    w_in = nrm(ks[2], (DEPTH, D_MODEL, IN_COLS), D_MODEL ** -0.5)
    conv_w = nrm(ks[3], (DEPTH, CONV_W, LRU_WIDTH), CONV_W ** -0.5)
    conv_b = nrm(ks[4], (DEPTH, LRU_WIDTH), 0.02)
    w_rg_a = nrm(ks[5], (DEPTH, LRU_BLOCKS, LRU_BLOCK_W, LRU_BLOCK_W), LRU_BLOCK_W ** -0.5)
    b_rg_a = nrm(ks[6], (DEPTH, LRU_WIDTH), 0.02)
    w_rg_i = nrm(ks[7], (DEPTH, LRU_BLOCKS, LRU_BLOCK_W, LRU_BLOCK_W), LRU_BLOCK_W ** -0.5)
    b_rg_i = nrm(ks[8], (DEPTH, LRU_WIDTH), 0.02)
    a0 = jax.random.uniform(ks[9], (DEPTH, LRU_WIDTH), f32, 0.9, 0.999)
    s = a0 ** (1.0 / LRU_C)
    lru_lambda = jnp.log(s) - jnp.log1p(-s)
    rel_bias = nrm(ks[10], (DEPTH, ATT_HEADS, 2 * REL_CLIP + 1), 0.1)
    w_proj_a = nrm(ks[11], (DEPTH, LRU_WIDTH, D_MODEL), LRU_WIDTH ** -0.5)
    w_proj_b = nrm(ks[12], (DEPTH, ATT_WIDTH, D_MODEL), ATT_WIDTH ** -0.5)
    w_out = nrm(ks[13], (DEPTH, D_MODEL, D_MODEL), D_MODEL ** -0.5)
    g_pre = 1.0 + nrm(ks[14], (DEPTH, D_MODEL), 0.01)
    g_post = 1.0 + nrm(ks[15], (DEPTH, D_MODEL), 0.01)
    w_ple = nrm(ks[16], (DEPTH, PLE_DIM, D_MODEL), PLE_DIM ** -0.5)
    w_ple_gate = nrm(ks[17], (DEPTH, D_MODEL, D_MODEL), D_MODEL ** -0.5)
    g_ple_pre = 1.0 + nrm(ks[18], (DEPTH, D_MODEL), 0.01)
    g_ple_post = 1.0 + nrm(ks[19], (DEPTH, D_MODEL), 0.01)
    return {'x': x, 'p': p, 'w_in': w_in, 'conv_w': conv_w, 'conv_b': conv_b,
            'w_rg_a': w_rg_a, 'b_rg_a': b_rg_a, 'w_rg_i': w_rg_i, 'b_rg_i': b_rg_i,
            'lru_lambda': lru_lambda, 'rel_bias': rel_bias, 'w_proj_a': w_proj_a,
            'w_proj_b': w_proj_b, 'w_out': w_out, 'g_pre': g_pre, 'g_post': g_post,
            'w_ple': w_ple, 'w_ple_gate': w_ple_gate, 'g_ple_pre': g_ple_pre,
            'g_ple_post': g_ple_post}


def reference(x, p, w_in, conv_w, conv_b, w_rg_a, b_rg_a, w_rg_i, b_rg_i, lru_lambda,
              rel_bias, w_proj_a, w_proj_b, w_out, g_pre, g_post,
              w_ple, w_ple_gate, g_ple_pre, g_ple_post):
    h = x
    for l in range(DEPTH):
        h = hybrid_layer(h, p[l], w_in[l], conv_w[l], conv_b[l], w_rg_a[l], b_rg_a[l],
                         w_rg_i[l], b_rg_i[l], lru_lambda[l], rel_bias[l], w_proj_a[l],
                         w_proj_b[l], w_out[l], g_pre[l], g_post[l], w_ple[l],
                         w_ple_gate[l], g_ple_pre[l], g_ple_post[l])
    return h
```

```python
import functools

import jax
import jax.numpy as jnp
from jax import lax
from jax.experimental import pallas as pl
from jax.experimental.pallas import tpu as pltpu

F32 = jnp.float32
BF16 = jnp.bfloat16

EPS = 1e-6
NEG_INF = -1e30
LRU_C = 8.0

CHUNK = 64
CTX_CHUNKS = 8
REL_CLIP = 128
ATT_HEAD_DIM = 128
LRU_BLOCK_W = 256
CONV_W = 4

SUBLANES = 8
ATT_BLOCK_Q = 256
ATT_KEY_BLOCKS = 1 + (CTX_CHUNKS * CHUNK) // ATT_BLOCK_Q
VMEM_LIMIT_BYTES = 56 * 1024 * 1024


def _params(semantics):
    return pltpu.CompilerParams(dimension_semantics=semantics,
                                vmem_limit_bytes=VMEM_LIMIT_BYTES)


def _rms_norm_f32(x, g):
    ms = jnp.mean(x * x, axis=-1, keepdims=True)
    return (x * lax.rsqrt(ms + EPS)) * g


def _rmsnorm_kernel(x_ref, g_ref, o_ref):
    o_ref[...] = _rms_norm_f32(x_ref[...], g_ref[...]).astype(o_ref.dtype)


def _rmsnorm(x, g, bm=256):
    s, d = x.shape
    return pl.pallas_call(
        _rmsnorm_kernel,
        grid=(s // bm,),
        in_specs=[pl.BlockSpec((bm, d), lambda i: (i, 0)),
                  pl.BlockSpec((1, d), lambda i: (0, 0))],
        out_specs=pl.BlockSpec((bm, d), lambda i: (i, 0)),
        out_shape=jax.ShapeDtypeStruct((s, d), BF16),
        compiler_params=_params(("parallel",)),
        name="rmsnorm_pre",
    )(x, g)


def _matmul_kernel(a_ref, b_ref, o_ref):
    o_ref[...] = jnp.dot(a_ref[...], b_ref[...],
                         preferred_element_type=F32).astype(o_ref.dtype)


def _matmul(a, b, name, bm=1024, bn=1024):
    m, k = a.shape
    _, n = b.shape
    return pl.pallas_call(
        _matmul_kernel,
        grid=(m // bm, n // bn),
        in_specs=[pl.BlockSpec((bm, k), lambda i, j: (i, 0)),
                  pl.BlockSpec((k, bn), lambda i, j: (0, j))],
        out_specs=pl.BlockSpec((bm, bn), lambda i, j: (i, j)),
        out_shape=jax.ShapeDtypeStruct((m, n), BF16),
        compiler_params=_params(("parallel", "parallel")),
        name=name,
    )(a, b)


def _lru_kernel(xa_ref, za_ref, cw_ref, cb_ref, wa_ref, wi_ref, ba_ref, bi_ref,
                lam_ref, o_ref, xe_ref, hc_ref, *, bm, n_blocks):
    @pl.when(pl.program_id(0) == 0)
    def _():
        xe_ref[0:SUBLANES, :] = jnp.zeros((SUBLANES, xe_ref.shape[1]), F32)
        hc_ref[...] = jnp.zeros(hc_ref.shape, F32)

    row = lax.broadcasted_iota(jnp.int32, (bm, LRU_BLOCK_W), 0)

    def block(n, carry):
        sl = pl.ds(pl.multiple_of(n * LRU_BLOCK_W, LRU_BLOCK_W), LRU_BLOCK_W)
        xa = xa_ref[:, sl].astype(F32)
        xe_ref[SUBLANES:, sl] = xa
        xc = cb_ref[:, sl] + cw_ref[CONV_W - 1:CONV_W, sl] * xa
        for k in range(CONV_W - 1):
            shift = CONV_W - 1 - k
            xc = xc + cw_ref[k:k + 1, sl] * xe_ref[pl.ds(SUBLANES - shift, bm), sl]
        xe_ref[0:SUBLANES, sl] = xa[bm - SUBLANES:, :]

        xcb = xc.astype(BF16)
        r = jax.nn.sigmoid(jnp.dot(xcb, wa_ref[n], preferred_element_type=F32) + ba_ref[:, sl])
        i = jax.nn.sigmoid(jnp.dot(xcb, wi_ref[n], preferred_element_type=F32) + bi_ref[:, sl])
        lam = lam_ref[:, sl]
        softplus_neg_lam = jnp.maximum(-lam, 0.0) + jnp.log1p(jnp.exp(-jnp.abs(lam)))
        log_a = (-LRU_C * r) * softplus_neg_lam
        a = jnp.exp(log_a)
        u = jnp.sqrt(-jnp.tanh(log_a) * (a * a + 1.0)) * (i * xc)

        d = 1
        while d < bm:
            keep = row >= d
            a_prev = jnp.where(keep, pltpu.roll(a, d, 0), 1.0)
            u_prev = jnp.where(keep, pltpu.roll(u, d, 0), 0.0)
            u = u + a * u_prev
            a = a * a_prev
            d *= 2
        h = a * hc_ref[0:1, sl] + u
        hc_ref[0:1, sl] = h[bm - 1:bm, :]

        z = za_ref[:, sl].astype(F32)
        o_ref[:, sl] = (h * (z * jax.nn.sigmoid(z))).astype(o_ref.dtype)
        return carry

    lax.fori_loop(0, n_blocks, block, 0)


def _lru_branch(proj, conv_w, conv_b, w_a, w_i, b_a, b_i, lam, width, bm=256):
    s = proj.shape[0]
    n_blocks = width // LRU_BLOCK_W
    vec = lambda rows: pl.BlockSpec((rows, width), lambda i: (0, 0))
    wspec = pl.BlockSpec((n_blocks, LRU_BLOCK_W, LRU_BLOCK_W), lambda i: (0, 0, 0))
    return pl.pallas_call(
        functools.partial(_lru_kernel, bm=bm, n_blocks=n_blocks),
        grid=(s // bm,),
        in_specs=[pl.BlockSpec((bm, width), lambda i: (i, 0)),
                  pl.BlockSpec((bm, width), lambda i: (i, 1)),
                  vec(CONV_W), vec(1), wspec, wspec, vec(1), vec(1), vec(1)],
        out_specs=pl.BlockSpec((bm, width), lambda i: (i, 0)),
        out_shape=jax.ShapeDtypeStruct((s, width), BF16),
        scratch_shapes=[pltpu.VMEM((bm + SUBLANES, width), F32),
                        pltpu.VMEM((SUBLANES, width), F32)],
        compiler_params=_params(("arbitrary",)),
        name="rglru_branch",
    )(proj, proj, conv_w, conv_b, w_a, w_i, b_a, b_i, lam)


def _attn_kernel(q_ref, k0_ref, k1_ref, k2_ref, v0_ref, v1_ref, v2_ref, zb_ref, bias_ref,
                 o_ref, *, n_heads, scale):
    bq = q_ref.shape[0]
    nk = ATT_KEY_BLOCKS * bq
    kpos = lax.broadcasted_iota(jnp.int32, (bq, nk), 1) + (pl.program_id(0) - (ATT_KEY_BLOCKS - 1)) * bq
    valid = kpos >= 0

    def head(h, carry):
        hs = pl.ds(pl.multiple_of(h * ATT_HEAD_DIM, ATT_HEAD_DIM), ATT_HEAD_DIM)
        kh = jnp.concatenate([k0_ref[:, hs], k1_ref[:, hs], k2_ref[:, hs]], axis=0)
        vh = jnp.concatenate([v0_ref[:, hs], v1_ref[:, hs], v2_ref[:, hs]], axis=0)
        s = lax.dot_general(q_ref[:, hs], kh, (((1,), (1,)), ((), ())),
                            preferred_element_type=F32)
        s = s * scale + bias_ref[h]
        s = jnp.where(valid, s, NEG_INF)
        m = jnp.max(s, axis=-1, keepdims=True)
        p = jnp.exp(s - m)
        l = jnp.sum(p, axis=-1, keepdims=True)
        o = jnp.dot(p.astype(BF16), vh, preferred_element_type=F32) / l
        z = zb_ref[:, hs].astype(F32)
        o_ref[:, hs] = (o * (z * jax.nn.sigmoid(z))).astype(o_ref.dtype)
        return carry

    lax.fori_loop(0, n_heads, head, 0)


def _band_bias(rel_bias):
    nk = ATT_KEY_BLOCKS * ATT_BLOCK_Q
    qi = jnp.arange(ATT_BLOCK_Q)[:, None]
    kj = jnp.arange(nk)[None, :]
    dist = jnp.clip((nk - ATT_BLOCK_Q) + qi - kj, -REL_CLIP, REL_CLIP) + REL_CLIP
    q_chunk = qi // CHUNK
    k_chunk = kj // CHUNK
    in_band = (k_chunk >= q_chunk) & (k_chunk <= q_chunk + CTX_CHUNKS)
    return jnp.where(in_band[None], rel_bias[:, dist].astype(F32), NEG_INF)


def _attn_branch(proj, rel_bias, att_width, col0):
    s = proj.shape[0]
    n_heads = att_width // ATT_HEAD_DIM
    bq = ATT_BLOCK_Q
    bias = _band_bias(rel_bias)

    def kv_spec(col, back):
        return pl.BlockSpec((bq, att_width), lambda i: (jnp.maximum(i - back, 0), col))

    return pl.pallas_call(
        functools.partial(_attn_kernel, n_heads=n_heads, scale=ATT_HEAD_DIM ** -0.5),
        grid=(s // bq,),
        in_specs=[pl.BlockSpec((bq, att_width), lambda i: (i, col0)),
                  kv_spec(col0 + 1, 2), kv_spec(col0 + 1, 1), kv_spec(col0 + 1, 0),
                  kv_spec(col0 + 2, 2), kv_spec(col0 + 2, 1), kv_spec(col0 + 2, 0),
                  pl.BlockSpec((bq, att_width), lambda i: (i, col0 + 3)),
                  pl.BlockSpec(bias.shape, lambda i: (0, 0, 0))],
        out_specs=pl.BlockSpec((bq, att_width), lambda i: (i, 0)),
        out_shape=jax.ShapeDtypeStruct((s, att_width), BF16),
        compiler_params=_params(("parallel",)),
        name="chunk_attention",
    )(proj, proj, proj, proj, proj, proj, proj, proj, bias)


def _merge_kernel(ya_ref, yb_ref, wa_ref, wb_ref, ga_ref, gb_ref, o_ref):
    pa = jnp.dot(ya_ref[...], wa_ref[...], preferred_element_type=F32)
    pb = jnp.dot(yb_ref[...], wb_ref[...], preferred_element_type=F32)
    ga = jax.nn.sigmoid(ga_ref[...].astype(F32))
    gb = jax.nn.sigmoid(gb_ref[...].astype(F32))
    o_ref[...] = (ga * pa + gb * pb).astype(o_ref.dtype)


def _merge(y_a, y_b, w_pa, w_pb, proj, gate_col0, bm=512, bn=1024):
    m, ka = y_a.shape
    kb = y_b.shape[1]
    n = w_pa.shape[1]
    nb = n // bn
    return pl.pallas_call(
        _merge_kernel,
        grid=(nb, m // bm),
        in_specs=[pl.BlockSpec((bm, ka), lambda j, i: (i, 0)),
                  pl.BlockSpec((bm, kb), lambda j, i: (i, 0)),
                  pl.BlockSpec((ka, bn), lambda j, i: (0, j)),
                  pl.BlockSpec((kb, bn), lambda j, i: (0, j)),
                  pl.BlockSpec((bm, bn), lambda j, i: (i, gate_col0 + j)),
                  pl.BlockSpec((bm, bn), lambda j, i: (i, gate_col0 + nb + j))],
        out_specs=pl.BlockSpec((bm, bn), lambda j, i: (i, j)),
        out_shape=jax.ShapeDtypeStruct((m, n), BF16),
        compiler_params=_params(("parallel", "parallel")),
        name="branch_merge",
    )(y_a, y_b, w_pa, w_pb, proj, proj)


def _rowwise_kernel(x_ref, t_ref, p_ref, wple_ref, gpost_ref, gpre_ref, gple_ref,
                    h_ref, hn_ref, pn_ref):
    h = x_ref[...] + _rms_norm_f32(t_ref[...].astype(F32), gpost_ref[...])
    h_ref[...] = h
    hn_ref[...] = _rms_norm_f32(h, gpre_ref[...]).astype(hn_ref.dtype)
    pe = jnp.dot(p_ref[...].astype(BF16), wple_ref[...], preferred_element_type=F32)
    pn_ref[...] = _rms_norm_f32(pe, gple_ref[...]).astype(pn_ref.dtype)


def _rowwise(x, t, p, w_ple, g_post, g_ple_pre, g_ple_post, bm=256):
    s, d = x.shape
    pd = p.shape[1]
    row = lambda w: pl.BlockSpec((bm, w), lambda i: (i, 0))
    vec = pl.BlockSpec((1, d), lambda i: (0, 0))
    return pl.pallas_call(
        _rowwise_kernel,
        grid=(s // bm,),
        in_specs=[row(d), row(d), row(pd), pl.BlockSpec((pd, d), lambda i: (0, 0)), vec, vec, vec],
        out_specs=[row(d), row(d), row(d)],
        out_shape=[jax.ShapeDtypeStruct((s, d), F32),
                   jax.ShapeDtypeStruct((s, d), BF16),
                   jax.ShapeDtypeStruct((s, d), BF16)],
        compiler_params=_params(("parallel",)),
        name="residual_norms",
    )(x, t, p, w_ple, g_post, g_ple_pre, g_ple_post)


def _ple_kernel(hn_ref, w_ref, h_ref, pn_ref, o_ref):
    g = jnp.dot(hn_ref[...], w_ref[...], preferred_element_type=F32)
    o_ref[...] = h_ref[...] + pn_ref[...].astype(F32) * jax.nn.sigmoid(g)


def _ple_gate(hn, w, h, pn, bm=512, bn=1024):
    m, k = hn.shape
    n = w.shape[1]
    return pl.pallas_call(
        _ple_kernel,
        grid=(n // bn, m // bm),
        in_specs=[pl.BlockSpec((bm, k), lambda j, i: (i, 0)),
                  pl.BlockSpec((k, bn), lambda j, i: (0, j)),
                  pl.BlockSpec((bm, bn), lambda j, i: (i, j)),
                  pl.BlockSpec((bm, bn), lambda j, i: (i, j))],
        out_specs=pl.BlockSpec((bm, bn), lambda j, i: (i, j)),
        out_shape=jax.ShapeDtypeStruct((m, n), F32),
        compiler_params=_params(("parallel", "parallel")),
        name="ple_gate",
    )(hn, w, h, pn)


def _layer(x, p_i, w_in, conv_w, conv_b, w_rg_a, b_rg_a, w_rg_i, b_rg_i, lru_lambda,
           rel_bias, w_proj_a, w_proj_b, w_out, g_pre, g_post,
           w_ple, w_ple_gate, g_ple_pre, g_ple_post):
    d = x.shape[1]
    lru_width = w_proj_a.shape[0]
    att_width = w_proj_b.shape[0]
    assert lru_width == d and 2 * att_width == d
    row = lambda v: v.reshape(1, -1)

    xn = _rmsnorm(x, row(g_pre))
    proj = _matmul(xn, w_in.astype(BF16), "in_proj")
    y_a = _lru_branch(proj, conv_w, row(conv_b), w_rg_a.astype(BF16), w_rg_i.astype(BF16),
                      row(b_rg_a), row(b_rg_i), row(lru_lambda), lru_width)
    y_b = _attn_branch(proj, rel_bias, att_width, col0=2 * lru_width // att_width)
    gate_start = 2 * lru_width + 4 * att_width
    merged = _merge(y_a, y_b, w_proj_a.astype(BF16), w_proj_b.astype(BF16), proj,
                    gate_col0=gate_start // 1024)
    t = _matmul(merged, w_out.astype(BF16), "out_proj")
    h, hn, pn = _rowwise(x, t, p_i, w_ple.astype(BF16), row(g_post), row(g_ple_pre), row(g_ple_post))
    return _ple_gate(hn, w_ple_gate.astype(BF16), h, pn)


def kernel(x, p, w_in, conv_w, conv_b, w_rg_a, b_rg_a, w_rg_i, b_rg_i, lru_lambda, rel_bias,
           w_proj_a, w_proj_b, w_out, g_pre, g_post, w_ple, w_ple_gate, g_ple_pre, g_ple_post):
    batch = x.shape[0]
    outs = []
    for b in range(batch):
        h = x[b]
        for l in range(w_in.shape[0]):
            h = _layer(h, p[l, b], w_in[l], conv_w[l], conv_b[l], w_rg_a[l], b_rg_a[l],
                       w_rg_i[l], b_rg_i[l], lru_lambda[l], rel_bias[l], w_proj_a[l],
                       w_proj_b[l], w_out[l], g_pre[l], g_post[l], w_ple[l],
                       w_ple_gate[l], g_ple_pre[l], g_ple_post[l])
        outs.append(h)
    return jnp.stack(outs, axis=0)
```

```python
import functools
import math

import jax
import jax.numpy as jnp
import numpy as np
from jax import lax
from jax.experimental import pallas as pl
from jax.experimental.pallas import tpu as pltpu

F32 = jnp.float32
BF16 = jnp.bfloat16

EPS = 1e-6
NEG_INF = -1e30
LRU_C = 8.0
LOG2_E = math.log2(math.e)

CHUNK = 64
CTX_CHUNKS = 8
REL_CLIP = 128
ATT_HEAD_DIM = 128
LRU_BLOCK_W = 256
CONV_W = 4

SUBLANES = 8
ATT_BLOCK_Q = 256
ATT_KEY_BLOCKS = 1 + (CTX_CHUNKS * CHUNK) // ATT_BLOCK_Q
VMEM_LIMIT_BYTES = 56 * 1024 * 1024


def _params(semantics):
    return pltpu.CompilerParams(dimension_semantics=semantics,
                                vmem_limit_bytes=VMEM_LIMIT_BYTES)


def _rms_norm_f32(x, g):
    ms = jnp.mean(x * x, axis=-1, keepdims=True)
    return (x * lax.rsqrt(ms + EPS)) * g


def _rmsnorm_kernel(x_ref, g_ref, o_ref):
    o_ref[...] = _rms_norm_f32(x_ref[...], g_ref[...]).astype(o_ref.dtype)


def _rmsnorm(x, g, bm=256):
    s, d = x.shape
    return pl.pallas_call(
        _rmsnorm_kernel,
        grid=(s // bm,),
        in_specs=[pl.BlockSpec((bm, d), lambda i: (i, 0)),
                  pl.BlockSpec((1, d), lambda i: (0, 0))],
        out_specs=pl.BlockSpec((bm, d), lambda i: (i, 0)),
        out_shape=jax.ShapeDtypeStruct((s, d), BF16),
        compiler_params=_params(("parallel",)),
        name="rmsnorm_pre",
    )(x, g)


def _cast_weight_once(w_ref, wb_ref):
    @pl.when(pl.program_id(1) == 0)
    def _():
        wb_ref[...] = w_ref[...].astype(BF16)


def _matmul_kernel(a_ref, b_ref, o_ref, bb_ref):
    _cast_weight_once(b_ref, bb_ref)
    acc = jnp.dot(a_ref[...], bb_ref[...], preferred_element_type=F32)
    o_ref[...] = acc.astype(o_ref.dtype)


def _scaled_matmul_kernel(a_ref, b_ref, s_ref, o_ref, bb_ref):
    _cast_weight_once(b_ref, bb_ref)
    acc = jnp.dot(a_ref[...], bb_ref[...], preferred_element_type=F32)
    o_ref[...] = (acc * s_ref[...]).astype(o_ref.dtype)


def _matmul(a, b, name, col_scale=None, bm=1024, bn=512):
    m, k = a.shape
    _, n = b.shape
    in_specs = [pl.BlockSpec((bm, k), lambda j, i: (i, 0)),
                pl.BlockSpec((k, bn), lambda j, i: (0, j))]
    args = (a, b)
    body = _matmul_kernel
    if col_scale is not None:
        in_specs.append(pl.BlockSpec((1, bn), lambda j, i: (0, j)))
        args += (col_scale,)
        body = _scaled_matmul_kernel
    return pl.pallas_call(
        body,
        grid=(n // bn, m // bm),
        in_specs=in_specs,
        out_specs=pl.BlockSpec((bm, bn), lambda j, i: (i, j)),
        out_shape=jax.ShapeDtypeStruct((m, n), BF16),
        scratch_shapes=[pltpu.VMEM((k, bn), BF16)],
        compiler_params=_params(("parallel", "arbitrary")),
        name=name,
    )(*args)


def _lru_kernel(xa_ref, za_ref, cw_ref, cb_ref, wa_ref, wi_ref, ba_ref, bi_ref,
                lam_ref, o_ref, xe_ref, hc_ref, *, bm, n_blocks):
    @pl.when(pl.program_id(0) == 0)
    def _():
        xe_ref[0:SUBLANES, :] = jnp.zeros((SUBLANES, xe_ref.shape[1]), F32)
        hc_ref[...] = jnp.zeros(hc_ref.shape, F32)

    groups = bm // SUBLANES
    sub = lax.broadcasted_iota(jnp.int32, (groups, SUBLANES, LRU_BLOCK_W), 1)

    def block(n, carry):
        sl = pl.ds(pl.multiple_of(n * LRU_BLOCK_W, LRU_BLOCK_W), LRU_BLOCK_W)
        xa = xa_ref[:, sl].astype(F32)
        xe_ref[SUBLANES:, sl] = xa
        xc = cb_ref[:, sl] + cw_ref[CONV_W - 1:CONV_W, sl] * xa
        for k in range(CONV_W - 1):
            shift = CONV_W - 1 - k
            xc = xc + cw_ref[k:k + 1, sl] * xe_ref[pl.ds(SUBLANES - shift, bm), sl]
        xe_ref[0:SUBLANES, sl] = xa[bm - SUBLANES:, :]

        xcb = xc.astype(BF16)
        r = jax.nn.sigmoid(jnp.dot(xcb, wa_ref[n], preferred_element_type=F32) + ba_ref[:, sl])
        i = jax.nn.sigmoid(jnp.dot(xcb, wi_ref[n], preferred_element_type=F32) + bi_ref[:, sl])
        lam = lam_ref[:, sl]
        softplus_neg_lam = jnp.maximum(-lam, 0.0) + jnp.log1p(jnp.exp(-jnp.abs(lam)))
        log_a = (-LRU_C * r) * softplus_neg_lam
        a = jnp.exp(log_a)
        u = jnp.sqrt(-jnp.tanh(log_a) * (a * a + 1.0)) * (i * xc)

        a = a.reshape(groups, SUBLANES, LRU_BLOCK_W)
        u = u.reshape(groups, SUBLANES, LRU_BLOCK_W)
        d = 1
        while d < SUBLANES:
            keep = sub >= d
            a_prev = jnp.where(keep, pltpu.roll(a, d, 1), 1.0)
            u_prev = jnp.where(keep, pltpu.roll(u, d, 1), 0.0)
            u = u + a * u_prev
            a = a * a_prev
            d *= 2
        h_prev = hc_ref[0:1, sl]
        hs = []
        for g in range(groups):
            hg = a[g] * h_prev + u[g]
            hs.append(hg)
            h_prev = hg[SUBLANES - 1:SUBLANES, :]
        hc_ref[0:1, sl] = h_prev
        h = jnp.concatenate(hs, axis=0)

        z = za_ref[:, sl].astype(F32)
        o_ref[:, sl] = (h * (z * jax.nn.sigmoid(z))).astype(o_ref.dtype)
        return carry

    lax.fori_loop(0, n_blocks, block, 0)


def _lru_branch(proj, conv_w, conv_b, w_a, w_i, b_a, b_i, lam, width, bm=256):
    s = proj.shape[0]
    n_blocks = width // LRU_BLOCK_W
    vec = lambda rows: pl.BlockSpec((rows, width), lambda i: (0, 0))
    wspec = pl.BlockSpec((n_blocks, LRU_BLOCK_W, LRU_BLOCK_W), lambda i: (0, 0, 0))
    return pl.pallas_call(
        functools.partial(_lru_kernel, bm=bm, n_blocks=n_blocks),
        grid=(s // bm,),
        in_specs=[pl.BlockSpec((bm, width), lambda i: (i, 0)),
                  pl.BlockSpec((bm, width), lambda i: (i, 1)),
                  vec(CONV_W), vec(1), wspec, wspec, vec(1), vec(1), vec(1)],
        out_specs=pl.BlockSpec((bm, width), lambda i: (i, 0)),
        out_shape=jax.ShapeDtypeStruct((s, width), BF16),
        scratch_shapes=[pltpu.VMEM((bm + SUBLANES, width), F32),
                        pltpu.VMEM((SUBLANES, width), F32)],
        compiler_params=_params(("arbitrary",)),
        name="rglru_branch",
    )(proj, proj, conv_w, conv_b, w_a, w_i, b_a, b_i, lam)


def _attn_kernel(q_ref, k0_ref, k1_ref, k2_ref, v0_ref, v1_ref, v2_ref, zb_ref, bias_ref,
                 o_ref, *, n_heads):
    def head(h, carry):
        hs = pl.ds(pl.multiple_of(h * ATT_HEAD_DIM, ATT_HEAD_DIM), ATT_HEAD_DIM)
        kh = jnp.concatenate([k0_ref[:, hs], k1_ref[:, hs], k2_ref[:, hs]], axis=0)
        vh = jnp.concatenate([v0_ref[:, hs], v1_ref[:, hs], v2_ref[:, hs]], axis=0)
        s = lax.dot_general(q_ref[:, hs], kh, (((1,), (1,)), ((), ())),
                            preferred_element_type=F32)
        s = s + bias_ref[h]
        m = jnp.max(s, axis=-1, keepdims=True)
        p = jnp.exp2(s - m)
        l = jnp.sum(p, axis=-1, keepdims=True)
        o = jnp.dot(p.astype(BF16), vh, preferred_element_type=F32) / l
        z = zb_ref[:, hs].astype(F32)
        o_ref[:, hs] = (o * (z * jax.nn.sigmoid(z))).astype(o_ref.dtype)
        return carry

    lax.fori_loop(0, n_heads, head, 0, unroll=4)


def _band_bias(rel_bias):
    n_heads = rel_bias.shape[0]
    bq = ATT_BLOCK_Q
    nk = ATT_KEY_BLOCKS * bq
    off = nk - bq
    length = nk + bq
    n_far = off + bq - REL_CLIP
    n_near = length - n_far - 2 * REL_CLIP
    w = jnp.concatenate([
        jnp.broadcast_to(rel_bias[:, 2 * REL_CLIP:], (n_heads, n_far)),
        jnp.flip(rel_bias[:, :2 * REL_CLIP], axis=1),
        jnp.broadcast_to(rel_bias[:, :1], (n_heads, n_near)),
    ], axis=1).astype(F32)
    skew = jnp.broadcast_to(w[:, None, :], (n_heads, bq, length)).reshape(n_heads, bq * length)
    skew = skew[:, :bq * (length - 1)].reshape(n_heads, bq, length - 1)
    bias = skew[:, :, bq - 1:bq - 1 + nk]

    qi = np.arange(bq)[:, None]
    kj = np.arange(nk)[None, :]
    in_band = (kj // CHUNK >= qi // CHUNK) & (kj // CHUNK <= qi // CHUNK + CTX_CHUNKS)
    kinds = np.arange(ATT_KEY_BLOCKS)[:, None, None]
    visible = in_band[None] & (kj[None] >= (ATT_KEY_BLOCKS - 1 - kinds) * bq)
    return jnp.where(visible[:, None], bias[None] * LOG2_E, NEG_INF)


def _attn_branch(proj, rel_bias, att_width, col0):
    s = proj.shape[0]
    n_heads = att_width // ATT_HEAD_DIM
    bq = ATT_BLOCK_Q
    bias = _band_bias(rel_bias)

    def kv_spec(col, back):
        return pl.BlockSpec((bq, att_width), lambda i: (jnp.maximum(i - back, 0), col))

    return pl.pallas_call(
        functools.partial(_attn_kernel, n_heads=n_heads),
        grid=(s // bq,),
        in_specs=[pl.BlockSpec((bq, att_width), lambda i: (i, col0)),
                  kv_spec(col0 + 1, 2), kv_spec(col0 + 1, 1), kv_spec(col0 + 1, 0),
                  kv_spec(col0 + 2, 2), kv_spec(col0 + 2, 1), kv_spec(col0 + 2, 0),
                  pl.BlockSpec((bq, att_width), lambda i: (i, col0 + 3)),
                  pl.BlockSpec((None,) + bias.shape[1:],
                               lambda i: (jnp.minimum(i, ATT_KEY_BLOCKS - 1), 0, 0, 0))],
        out_specs=pl.BlockSpec((bq, att_width), lambda i: (i, 0)),
        out_shape=jax.ShapeDtypeStruct((s, att_width), BF16),
        compiler_params=_params(("parallel",)),
        name="chunk_attention",
    )(proj, proj, proj, proj, proj, proj, proj, proj, bias)


def _merge_kernel(ya_ref, yb_ref, wa_ref, wb_ref, ga_ref, gb_ref, o_ref, wab_ref, wbb_ref):
    _cast_weight_once(wa_ref, wab_ref)
    _cast_weight_once(wb_ref, wbb_ref)
    pa = jnp.dot(ya_ref[...], wab_ref[...], preferred_element_type=F32)
    pb = jnp.dot(yb_ref[...], wbb_ref[...], preferred_element_type=F32)
    ga = jax.nn.sigmoid(ga_ref[...].astype(F32))
    gb = jax.nn.sigmoid(gb_ref[...].astype(F32))
    o_ref[...] = (ga * pa + gb * pb).astype(o_ref.dtype)


def _merge(y_a, y_b, w_pa, w_pb, proj, gate_start, bm=512, bn=512):
    m, ka = y_a.shape
    kb = y_b.shape[1]
    n = w_pa.shape[1]
    nb = n // bn
    gate_col0 = gate_start // bn
    return pl.pallas_call(
        _merge_kernel,
        grid=(nb, m // bm),
        in_specs=[pl.BlockSpec((bm, ka), lambda j, i: (i, 0)),
                  pl.BlockSpec((bm, kb), lambda j, i: (i, 0)),
                  pl.BlockSpec((ka, bn), lambda j, i: (0, j)),
                  pl.BlockSpec((kb, bn), lambda j, i: (0, j)),
                  pl.BlockSpec((bm, bn), lambda j, i: (i, gate_col0 + j)),
                  pl.BlockSpec((bm, bn), lambda j, i: (i, gate_col0 + nb + j))],
        out_specs=pl.BlockSpec((bm, bn), lambda j, i: (i, j)),
        out_shape=jax.ShapeDtypeStruct((m, n), BF16),
        scratch_shapes=[pltpu.VMEM((ka, bn), BF16), pltpu.VMEM((kb, bn), BF16)],
        compiler_params=_params(("parallel", "arbitrary")),
        name="branch_merge",
    )(y_a, y_b, w_pa, w_pb, proj, proj)


def _rowwise_kernel(x_ref, t_ref, p_ref, wple_ref, gpost_ref, gpre_ref, gple_ref,
                    h_ref, hn_ref, pn_ref):
    h = x_ref[...] + _rms_norm_f32(t_ref[...].astype(F32), gpost_ref[...])
    h_ref[...] = h
    hn_ref[...] = _rms_norm_f32(h, gpre_ref[...]).astype(hn_ref.dtype)
    pe = jnp.dot(p_ref[...].astype(BF16), wple_ref[...], preferred_element_type=F32)
    pn_ref[...] = _rms_norm_f32(pe, gple_ref[...]).astype(pn_ref.dtype)


def _rowwise(x, t, p, w_ple, g_post, g_ple_pre, g_ple_post, bm=256):
    s, d = x.shape
    pd = p.shape[1]
    row = lambda w: pl.BlockSpec((bm, w), lambda i: (i, 0))
    vec = pl.BlockSpec((1, d), lambda i: (0, 0))
    return pl.pallas_call(
        _rowwise_kernel,
        grid=(s // bm,),
        in_specs=[row(d), row(d), row(pd), pl.BlockSpec((pd, d), lambda i: (0, 0)), vec, vec, vec],
        out_specs=[row(d), row(d), row(d)],
        out_shape=[jax.ShapeDtypeStruct((s, d), F32),
                   jax.ShapeDtypeStruct((s, d), BF16),
                   jax.ShapeDtypeStruct((s, d), BF16)],
        compiler_params=_params(("parallel",)),
        name="residual_norms",
    )(x, t, p, w_ple, g_post, g_ple_pre, g_ple_post)


def _ple_kernel(hn_ref, w_ref, h_ref, pn_ref, o_ref, wb_ref):
    _cast_weight_once(w_ref, wb_ref)
    g = jnp.dot(hn_ref[...], wb_ref[...], preferred_element_type=F32)
    o_ref[...] = h_ref[...] + pn_ref[...].astype(F32) * jax.nn.sigmoid(g)


def _ple_gate(hn, w, h, pn, bm=1024, bn=512):
    m, k = hn.shape
    n = w.shape[1]
    return pl.pallas_call(
        _ple_kernel,
        grid=(n // bn, m // bm),
        in_specs=[pl.BlockSpec((bm, k), lambda j, i: (i, 0)),
                  pl.BlockSpec((k, bn), lambda j, i: (0, j)),
                  pl.BlockSpec((bm, bn), lambda j, i: (i, j)),
                  pl.BlockSpec((bm, bn), lambda j, i: (i, j))],
        out_specs=pl.BlockSpec((bm, bn), lambda j, i: (i, j)),
        out_shape=jax.ShapeDtypeStruct((m, n), F32),
        scratch_shapes=[pltpu.VMEM((k, bn), BF16)],
        compiler_params=_params(("parallel", "arbitrary")),
        name="ple_gate",
    )(hn, w, h, pn)


def _layer(x, p_i, w_in, conv_w, conv_b, w_rg_a, b_rg_a, w_rg_i, b_rg_i, lru_lambda,
           rel_bias, w_proj_a, w_proj_b, w_out, g_pre, g_post,
           w_ple, w_ple_gate, g_ple_pre, g_ple_post):
    d = x.shape[1]
    lru_width = w_proj_a.shape[0]
    att_width = w_proj_b.shape[0]
    assert lru_width == d and 2 * att_width == d
    row = lambda v: v.reshape(1, -1)

    xn = _rmsnorm(x, row(g_pre))
    q_start = 2 * lru_width
    col_scale = jnp.ones((1, w_in.shape[1]), F32).at[:, q_start:q_start + att_width].set(
        ATT_HEAD_DIM ** -0.5 * LOG2_E)
    proj = _matmul(xn, w_in, "in_proj", col_scale=col_scale)
    y_a = _lru_branch(proj, conv_w, row(conv_b), w_rg_a.astype(BF16), w_rg_i.astype(BF16),
                      row(b_rg_a), row(b_rg_i), row(lru_lambda), lru_width)
    y_b = _attn_branch(proj, rel_bias, att_width, col0=2 * lru_width // att_width)
    merged = _merge(y_a, y_b, w_proj_a, w_proj_b, proj, gate_start=2 * lru_width + 4 * att_width)
    t = _matmul(merged, w_out, "out_proj")
    h, hn, pn = _rowwise(x, t, p_i, w_ple.astype(BF16), row(g_post), row(g_ple_pre), row(g_ple_post))
    return _ple_gate(hn, w_ple_gate, h, pn)


def kernel(x, p, w_in, conv_w, conv_b, w_rg_a, b_rg_a, w_rg_i, b_rg_i, lru_lambda, rel_bias,
           w_proj_a, w_proj_b, w_out, g_pre, g_post, w_ple, w_ple_gate, g_ple_pre, g_ple_post):
    batch = x.shape[0]
    outs = []
    for b in range(batch):
        h = x[b]
        for l in range(w_in.shape[0]):
            h = _layer(h, p[l, b], w_in[l], conv_w[l], conv_b[l], w_rg_a[l], b_rg_a[l],
                       w_rg_i[l], b_rg_i[l], lru_lambda[l], rel_bias[l], w_proj_a[l],
                       w_proj_b[l], w_out[l], g_pre[l], g_post[l], w_ple[l],
                       w_ple_gate[l], g_ple_pre[l], g_ple_post[l])
        outs.append(h)
    return jnp.stack(outs, axis=0)
```

```python
import functools
import math

import jax
import jax.numpy as jnp
import numpy as np
from jax import lax
from jax.experimental import pallas as pl
from jax.experimental.pallas import tpu as pltpu

F32 = jnp.float32
BF16 = jnp.bfloat16

EPS = 1e-6
NEG_INF = -1e30
LRU_C = 8.0
LOG2_E = math.log2(math.e)

CHUNK = 64
CTX_CHUNKS = 8
REL_CLIP = 128
ATT_HEAD_DIM = 128
LRU_BLOCK_W = 256
CONV_W = 4

SUBLANES = 8
ATT_BLOCK_Q = 256
ATT_KEY_BLOCKS = 1 + (CTX_CHUNKS * CHUNK) // ATT_BLOCK_Q
VMEM_LIMIT_BYTES = 56 * 1024 * 1024


def _params(semantics):
    return pltpu.CompilerParams(dimension_semantics=semantics,
                                vmem_limit_bytes=VMEM_LIMIT_BYTES)


def _rms_norm_f32(x, g):
    ms = jnp.mean(x * x, axis=-1, keepdims=True)
    return (x * lax.rsqrt(ms + EPS)) * g


def _rmsnorm_kernel(x_ref, g_ref, o_ref):
    o_ref[...] = _rms_norm_f32(x_ref[...], g_ref[...]).astype(o_ref.dtype)


def _rmsnorm(x, g, bm=256):
    s, d = x.shape
    return pl.pallas_call(
        _rmsnorm_kernel,
        grid=(s // bm,),
        in_specs=[pl.BlockSpec((bm, d), lambda i: (i, 0)),
                  pl.BlockSpec((1, d), lambda i: (0, 0))],
        out_specs=pl.BlockSpec((bm, d), lambda i: (i, 0)),
        out_shape=jax.ShapeDtypeStruct((s, d), BF16),
        compiler_params=_params(("parallel",)),
        name="rmsnorm_pre",
    )(x, g)


def _cast_chunk(chunk_ref, w_next_ref):
    kc = chunk_ref.shape[0]
    rows = pl.ds(pl.multiple_of(pl.program_id(1) * kc, kc), kc)
    w_next_ref[rows, :] = chunk_ref[...].astype(BF16)


def _stream_weights(chunk_refs, bufs0, bufs1, o_ref, compute):
    def run(cur, nxt):
        def cast():
            for chunk_ref, w_next_ref in zip(chunk_refs, nxt):
                _cast_chunk(chunk_ref, w_next_ref)

        @pl.when(pl.program_id(0) == 0)
        def _():
            cast()

        @pl.when(pl.program_id(0) > 0)
        def _():
            cast()
            o_ref[...] = compute(cur).astype(o_ref.dtype)

    parity = lax.rem(pl.program_id(0), 2)

    @pl.when(parity == 0)
    def _():
        run(bufs1, bufs0)

    @pl.when(parity == 1)
    def _():
        run(bufs0, bufs1)


def _chunk_spec(kc, bn, n_tiles, j0=0):
    return pl.BlockSpec((kc, bn), lambda j, i: (i, j0 + jnp.minimum(j, n_tiles - 1)))


def _prev_tile(j):
    return jnp.maximum(j - 1, 0)


def _row_tile(j, i):
    return jnp.where(j > 0, i, 0)


def _matmul_kernel(a_ref, chunk_ref, *rest, scaled):
    s_ref = rest[0] if scaled else None
    o_ref, wb0_ref, wb1_ref = rest[-3:]

    def compute(w):
        acc = jnp.dot(a_ref[...], w[0][...], preferred_element_type=F32)
        return acc * s_ref[...] if scaled else acc

    _stream_weights((chunk_ref,), (wb0_ref,), (wb1_ref,), o_ref, compute)


def _matmul(a, b, name, col_scale=None, bm=1024, bn=1024):
    m, k = a.shape
    _, n = b.shape
    n_tiles, row_tiles = n // bn, m // bm
    in_specs = [pl.BlockSpec((bm, k), lambda j, i: (_row_tile(j, i), 0)),
                _chunk_spec(k // row_tiles, bn, n_tiles)]
    args = (a, b)
    if col_scale is not None:
        in_specs.append(pl.BlockSpec((1, bn), lambda j, i: (0, _prev_tile(j))))
        args += (col_scale,)
    return pl.pallas_call(
        functools.partial(_matmul_kernel, scaled=col_scale is not None),
        grid=(n_tiles + 1, row_tiles),
        in_specs=in_specs,
        out_specs=pl.BlockSpec((bm, bn), lambda j, i: (_row_tile(j, i), _prev_tile(j))),
        out_shape=jax.ShapeDtypeStruct((m, n), BF16),
        scratch_shapes=[pltpu.VMEM((k, bn), BF16), pltpu.VMEM((k, bn), BF16)],
        compiler_params=_params(("arbitrary", "arbitrary")),
        name=name,
    )(*args)


def _lru_kernel(xa_ref, za_ref, cw_ref, cb_ref, wa_ref, wi_ref, ba_ref, bi_ref,
                lam_ref, o_ref, xe_ref, hc_ref, *, bm, n_blocks):
    @pl.when(pl.program_id(0) == 0)
    def _():
        xe_ref[0:SUBLANES, :] = jnp.zeros((SUBLANES, xe_ref.shape[1]), F32)
        hc_ref[...] = jnp.zeros(hc_ref.shape, F32)

    groups = bm // SUBLANES
    sub = lax.broadcasted_iota(jnp.int32, (groups, SUBLANES, LRU_BLOCK_W), 1)

    def block(n, carry):
        sl = pl.ds(pl.multiple_of(n * LRU_BLOCK_W, LRU_BLOCK_W), LRU_BLOCK_W)
        xa = xa_ref[:, sl].astype(F32)
        xe_ref[SUBLANES:, sl] = xa
        xc = cb_ref[:, sl] + cw_ref[CONV_W - 1:CONV_W, sl] * xa
        for k in range(CONV_W - 1):
            shift = CONV_W - 1 - k
            xc = xc + cw_ref[k:k + 1, sl] * xe_ref[pl.ds(SUBLANES - shift, bm), sl]
        xe_ref[0:SUBLANES, sl] = xa[bm - SUBLANES:, :]

        xcb = xc.astype(BF16)
        r = jax.nn.sigmoid(jnp.dot(xcb, wa_ref[n], preferred_element_type=F32) + ba_ref[:, sl])
        i = jax.nn.sigmoid(jnp.dot(xcb, wi_ref[n], preferred_element_type=F32) + bi_ref[:, sl])
        lam = lam_ref[:, sl]
        softplus_neg_lam = jnp.maximum(-lam, 0.0) + jnp.log1p(jnp.exp(-jnp.abs(lam)))
        log_a = (-LRU_C * r) * softplus_neg_lam
        a = jnp.exp(log_a)
        u = jnp.sqrt(-jnp.tanh(log_a) * (a * a + 1.0)) * (i * xc)

        a = a.reshape(groups, SUBLANES, LRU_BLOCK_W)
        u = u.reshape(groups, SUBLANES, LRU_BLOCK_W)
        d = 1
        while d < SUBLANES:
            keep = sub >= d
            a_prev = jnp.where(keep, pltpu.roll(a, d, 1), 1.0)
            u_prev = jnp.where(keep, pltpu.roll(u, d, 1), 0.0)
            u = u + a * u_prev
            a = a * a_prev
            d *= 2
        h_prev = hc_ref[0:1, sl]
        hs = []
        for g in range(groups):
            hg = a[g] * h_prev + u[g]
            hs.append(hg)
            h_prev = hg[SUBLANES - 1:SUBLANES, :]
        hc_ref[0:1, sl] = h_prev
        h = jnp.concatenate(hs, axis=0)

        z = za_ref[:, sl].astype(F32)
        o_ref[:, sl] = (h * (z * jax.nn.sigmoid(z))).astype(o_ref.dtype)
        return carry

    lax.fori_loop(0, n_blocks, block, 0)


def _lru_branch(proj, conv_w, conv_b, w_a, w_i, b_a, b_i, lam, width, bm=256):
    s = proj.shape[0]
    n_blocks = width // LRU_BLOCK_W
    vec = lambda rows: pl.BlockSpec((rows, width), lambda i: (0, 0))
    wspec = pl.BlockSpec((n_blocks, LRU_BLOCK_W, LRU_BLOCK_W), lambda i: (0, 0, 0))
    return pl.pallas_call(
        functools.partial(_lru_kernel, bm=bm, n_blocks=n_blocks),
        grid=(s // bm,),
        in_specs=[pl.BlockSpec((bm, width), lambda i: (i, 0)),
                  pl.BlockSpec((bm, width), lambda i: (i, 1)),
                  vec(CONV_W), vec(1), wspec, wspec, vec(1), vec(1), vec(1)],
        out_specs=pl.BlockSpec((bm, width), lambda i: (i, 0)),
        out_shape=jax.ShapeDtypeStruct((s, width), BF16),
        scratch_shapes=[pltpu.VMEM((bm + SUBLANES, width), F32),
                        pltpu.VMEM((SUBLANES, width), F32)],
        compiler_params=_params(("arbitrary",)),
        name="rglru_branch",
    )(proj, proj, conv_w, conv_b, w_a, w_i, b_a, b_i, lam)


def _attn_kernel(q_ref, k0_ref, k1_ref, k2_ref, v0_ref, v1_ref, v2_ref, zb_ref, bias_ref,
                 o_ref, *, n_heads):
    def head(h, carry):
        hs = pl.ds(pl.multiple_of(h * ATT_HEAD_DIM, ATT_HEAD_DIM), ATT_HEAD_DIM)
        kh = jnp.concatenate([k0_ref[:, hs], k1_ref[:, hs], k2_ref[:, hs]], axis=0)
        vh = jnp.concatenate([v0_ref[:, hs], v1_ref[:, hs], v2_ref[:, hs]], axis=0)
        s = lax.dot_general(q_ref[:, hs], kh, (((1,), (1,)), ((), ())),
                            preferred_element_type=F32)
        s = s + bias_ref[h]
        m = jnp.max(s, axis=-1, keepdims=True)
        p = jnp.exp2(s - m)
        l = jnp.sum(p, axis=-1, keepdims=True)
        o = jnp.dot(p.astype(BF16), vh, preferred_element_type=F32) / l
        z = zb_ref[:, hs].astype(F32)
        o_ref[:, hs] = (o * (z * jax.nn.sigmoid(z))).astype(o_ref.dtype)
        return carry

    lax.fori_loop(0, n_heads, head, 0, unroll=4)


def _band_bias(rel_bias):
    n_heads = rel_bias.shape[0]
    bq = ATT_BLOCK_Q
    nk = ATT_KEY_BLOCKS * bq
    off = nk - bq
    length = nk + bq
    n_far = off + bq - REL_CLIP
    n_near = length - n_far - 2 * REL_CLIP
    w = jnp.concatenate([
        jnp.broadcast_to(rel_bias[:, 2 * REL_CLIP:], (n_heads, n_far)),
        jnp.flip(rel_bias[:, :2 * REL_CLIP], axis=1),
        jnp.broadcast_to(rel_bias[:, :1], (n_heads, n_near)),
    ], axis=1).astype(F32)
    skew = jnp.broadcast_to(w[:, None, :], (n_heads, bq, length)).reshape(n_heads, bq * length)
    skew = skew[:, :bq * (length - 1)].reshape(n_heads, bq, length - 1)
    bias = skew[:, :, bq - 1:bq - 1 + nk]

    qi = np.arange(bq)[:, None]
    kj = np.arange(nk)[None, :]
    in_band = (kj // CHUNK >= qi // CHUNK) & (kj // CHUNK <= qi // CHUNK + CTX_CHUNKS)
    kinds = np.arange(ATT_KEY_BLOCKS)[:, None, None]
    visible = in_band[None] & (kj[None] >= (ATT_KEY_BLOCKS - 1 - kinds) * bq)
    return jnp.where(visible[:, None], bias[None] * LOG2_E, NEG_INF)


def _attn_branch(proj, rel_bias, att_width, col0):
    s = proj.shape[0]
    n_heads = att_width // ATT_HEAD_DIM
    bq = ATT_BLOCK_Q
    bias = _band_bias(rel_bias)

    def kv_spec(col, back):
        return pl.BlockSpec((bq, att_width), lambda i: (jnp.maximum(i - back, 0), col))

    return pl.pallas_call(
        functools.partial(_attn_kernel, n_heads=n_heads),
        grid=(s // bq,),
        in_specs=[pl.BlockSpec((bq, att_width), lambda i: (i, col0)),
                  kv_spec(col0 + 1, 2), kv_spec(col0 + 1, 1), kv_spec(col0 + 1, 0),
                  kv_spec(col0 + 2, 2), kv_spec(col0 + 2, 1), kv_spec(col0 + 2, 0),
                  pl.BlockSpec((bq, att_width), lambda i: (i, col0 + 3)),
                  pl.BlockSpec((None,) + bias.shape[1:],
                               lambda i: (jnp.minimum(i, ATT_KEY_BLOCKS - 1), 0, 0, 0))],
        out_specs=pl.BlockSpec((bq, att_width), lambda i: (i, 0)),
        out_shape=jax.ShapeDtypeStruct((s, att_width), BF16),
        compiler_params=_params(("parallel",)),
        name="chunk_attention",
    )(proj, proj, proj, proj, proj, proj, proj, proj, bias)


def _merge_kernel(ya_ref, yb_ref, ca_ref, cb_ref, ga_ref, gb_ref, o_ref,
                  wa0_ref, wb0_ref, wa1_ref, wb1_ref):
    def compute(w):
        pa = jnp.dot(ya_ref[...], w[0][...], preferred_element_type=F32)
        pb = jnp.dot(yb_ref[...], w[1][...], preferred_element_type=F32)
        ga = jax.nn.sigmoid(ga_ref[...].astype(F32))
        gb = jax.nn.sigmoid(gb_ref[...].astype(F32))
        return ga * pa + gb * pb

    _stream_weights((ca_ref, cb_ref), (wa0_ref, wb0_ref), (wa1_ref, wb1_ref), o_ref, compute)


def _merge(y_a, y_b, w_pa, w_pb, proj, gate_start, bm=512, bn=1024):
    m, ka = y_a.shape
    kb = y_b.shape[1]
    n = w_pa.shape[1]
    n_tiles, row_tiles = n // bn, m // bm
    gate_col0 = gate_start // bn
    return pl.pallas_call(
        _merge_kernel,
        grid=(n_tiles + 1, row_tiles),
        in_specs=[pl.BlockSpec((bm, ka), lambda j, i: (_row_tile(j, i), 0)),
                  pl.BlockSpec((bm, kb), lambda j, i: (_row_tile(j, i), 0)),
                  _chunk_spec(ka // row_tiles, bn, n_tiles),
                  _chunk_spec(kb // row_tiles, bn, n_tiles),
                  pl.BlockSpec((bm, bn), lambda j, i: (_row_tile(j, i), gate_col0 + _prev_tile(j))),
                  pl.BlockSpec((bm, bn),
                               lambda j, i: (_row_tile(j, i), gate_col0 + n_tiles + _prev_tile(j)))],
        out_specs=pl.BlockSpec((bm, bn), lambda j, i: (_row_tile(j, i), _prev_tile(j))),
        out_shape=jax.ShapeDtypeStruct((m, n), BF16),
        scratch_shapes=[pltpu.VMEM((ka, bn), BF16), pltpu.VMEM((kb, bn), BF16),
                        pltpu.VMEM((ka, bn), BF16), pltpu.VMEM((kb, bn), BF16)],
        compiler_params=_params(("arbitrary", "arbitrary")),
        name="branch_merge",
    )(y_a, y_b, w_pa, w_pb, proj, proj)


def _rowwise_kernel(x_ref, t_ref, p_ref, wple_ref, gpost_ref, gpre_ref, gple_ref,
                    h_ref, hn_ref, pn_ref):
    h = x_ref[...] + _rms_norm_f32(t_ref[...].astype(F32), gpost_ref[...])
    h_ref[...] = h
    hn_ref[...] = _rms_norm_f32(h, gpre_ref[...]).astype(hn_ref.dtype)
    pe = jnp.dot(p_ref[...].astype(BF16), wple_ref[...], preferred_element_type=F32)
    pn_ref[...] = _rms_norm_f32(pe, gple_ref[...]).astype(pn_ref.dtype)


def _rowwise(x, t, p, w_ple, g_post, g_ple_pre, g_ple_post, bm=256):
    s, d = x.shape
    pd = p.shape[1]
    row = lambda w: pl.BlockSpec((bm, w), lambda i: (i, 0))
    vec = pl.BlockSpec((1, d), lambda i: (0, 0))
    return pl.pallas_call(
        _rowwise_kernel,
        grid=(s // bm,),
        in_specs=[row(d), row(d), row(pd), pl.BlockSpec((pd, d), lambda i: (0, 0)), vec, vec, vec],
        out_specs=[row(d), row(d), row(d)],
        out_shape=[jax.ShapeDtypeStruct((s, d), F32),
                   jax.ShapeDtypeStruct((s, d), BF16),
                   jax.ShapeDtypeStruct((s, d), BF16)],
        compiler_params=_params(("parallel",)),
        name="residual_norms",
    )(x, t, p, w_ple, g_post, g_ple_pre, g_ple_post)


def _ple_kernel(hn_ref, chunk_ref, h_ref, pn_ref, o_ref, wb0_ref, wb1_ref):
    def compute(w):
        g = jnp.dot(hn_ref[...], w[0][...], preferred_element_type=F32)
        return h_ref[...] + pn_ref[...].astype(F32) * jax.nn.sigmoid(g)

    _stream_weights((chunk_ref,), (wb0_ref,), (wb1_ref,), o_ref, compute)


def _ple_gate(hn, w, h, pn, bm=1024, bn=512):
    m, k = hn.shape
    n = w.shape[1]
    n_tiles, row_tiles = n // bn, m // bm
    tile = pl.BlockSpec((bm, bn), lambda j, i: (_row_tile(j, i), _prev_tile(j)))
    return pl.pallas_call(
        _ple_kernel,
        grid=(n_tiles + 1, row_tiles),
        in_specs=[pl.BlockSpec((bm, k), lambda j, i: (_row_tile(j, i), 0)),
                  _chunk_spec(k // row_tiles, bn, n_tiles),
                  tile, tile],
        out_specs=tile,
        out_shape=jax.ShapeDtypeStruct((m, n), F32),
        scratch_shapes=[pltpu.VMEM((k, bn), BF16), pltpu.VMEM((k, bn), BF16)],
        compiler_params=_params(("arbitrary", "arbitrary")),
        name="ple_gate",
    )(hn, w, h, pn)


def _layer(x, p_i, w_in, conv_w, conv_b, w_rg_a, b_rg_a, w_rg_i, b_rg_i, lru_lambda,
           rel_bias, w_proj_a, w_proj_b, w_out, g_pre, g_post,
           w_ple, w_ple_gate, g_ple_pre, g_ple_post):
    d = x.shape[1]
    lru_width = w_proj_a.shape[0]
    att_width = w_proj_b.shape[0]
    assert lru_width == d and 2 * att_width == d
    row = lambda v: v.reshape(1, -1)

    xn = _rmsnorm(x, row(g_pre))
    q_start = 2 * lru_width
    col_scale = jnp.ones((1, w_in.shape[1]), F32).at[:, q_start:q_start + att_width].set(
        ATT_HEAD_DIM ** -0.5 * LOG2_E)
    proj = _matmul(xn, w_in, "in_proj", col_scale=col_scale)
    y_a = _lru_branch(proj, conv_w, row(conv_b), w_rg_a.astype(BF16), w_rg_i.astype(BF16),
                      row(b_rg_a), row(b_rg_i), row(lru_lambda), lru_width)
    y_b = _attn_branch(proj, rel_bias, att_width, col0=2 * lru_width // att_width)
    merged = _merge(y_a, y_b, w_proj_a, w_proj_b, proj, gate_start=2 * lru_width + 4 * att_width)
    t = _matmul(merged, w_out, "out_proj")
    h, hn, pn = _rowwise(x, t, p_i, w_ple.astype(BF16), row(g_post), row(g_ple_pre), row(g_ple_post))
    return _ple_gate(hn, w_ple_gate, h, pn)


def kernel(x, p, w_in, conv_w, conv_b, w_rg_a, b_rg_a, w_rg_i, b_rg_i, lru_lambda, rel_bias,
           w_proj_a, w_proj_b, w_out, g_pre, g_post, w_ple, w_ple_gate, g_ple_pre, g_ple_post):
    batch = x.shape[0]
    outs = []
    for b in range(batch):
        h = x[b]
        for l in range(w_in.shape[0]):
            h = _layer(h, p[l, b], w_in[l], conv_w[l], conv_b[l], w_rg_a[l], b_rg_a[l],
                       w_rg_i[l], b_rg_i[l], lru_lambda[l], rel_bias[l], w_proj_a[l],
                       w_proj_b[l], w_out[l], g_pre[l], g_post[l], w_ple[l],
                       w_ple_gate[l], g_ple_pre[l], g_ple_post[l])
        outs.append(h)
    return jnp.stack(outs, axis=0)
```

```python
import functools
import math

import jax
import jax.numpy as jnp
import numpy as np
from jax import lax
from jax.experimental import pallas as pl
from jax.experimental.pallas import tpu as pltpu

F32 = jnp.float32
BF16 = jnp.bfloat16

EPS = 1e-6
NEG_INF = -1e30
LRU_C = 8.0
LOG2_E = math.log2(math.e)

CHUNK = 64
CTX_CHUNKS = 8
REL_CLIP = 128
ATT_HEAD_DIM = 128
LRU_BLOCK_W = 256
CONV_W = 4

SUBLANES = 8
ATT_BLOCK_Q = 256
ATT_KEY_BLOCKS = 1 + (CTX_CHUNKS * CHUNK) // ATT_BLOCK_Q
VMEM_LIMIT_BYTES = 56 * 1024 * 1024


def _params(semantics):
    return pltpu.CompilerParams(dimension_semantics=semantics,
                                vmem_limit_bytes=VMEM_LIMIT_BYTES)


def _sigmoid(x):
    return 0.5 * jnp.tanh(0.5 * x) + 0.5


def _rms_norm_f32(x, g):
    ms = jnp.mean(x * x, axis=-1, keepdims=True)
    return (x * lax.rsqrt(ms + EPS)) * g


def _rmsnorm_kernel(x_ref, g_ref, o_ref):
    o_ref[...] = _rms_norm_f32(x_ref[...], g_ref[...]).astype(o_ref.dtype)


def _rmsnorm(x, g, bm=256):
    s, d = x.shape
    return pl.pallas_call(
        _rmsnorm_kernel,
        grid=(s // bm,),
        in_specs=[pl.BlockSpec((bm, d), lambda i: (i, 0)),
                  pl.BlockSpec((1, d), lambda i: (0, 0))],
        out_specs=pl.BlockSpec((bm, d), lambda i: (i, 0)),
        out_shape=jax.ShapeDtypeStruct((s, d), BF16),
        compiler_params=_params(("parallel",)),
        name="rmsnorm_pre",
    )(x, g)


def _cast_chunk(chunk_ref, w_next_ref):
    kc = chunk_ref.shape[0]
    rows = pl.ds(pl.multiple_of(pl.program_id(1) * kc, kc), kc)
    w_next_ref[rows, :] = chunk_ref[...].astype(BF16)


def _stream_weights(chunk_refs, bufs0, bufs1, o_ref, compute):
    def run(cur, nxt):
        def cast():
            for chunk_ref, w_next_ref in zip(chunk_refs, nxt):
                _cast_chunk(chunk_ref, w_next_ref)

        @pl.when(pl.program_id(0) == 0)
        def _():
            cast()

        @pl.when(pl.program_id(0) > 0)
        def _():
            cast()
            o_ref[...] = compute(cur).astype(o_ref.dtype)

    parity = lax.rem(pl.program_id(0), 2)

    @pl.when(parity == 0)
    def _():
        run(bufs1, bufs0)

    @pl.when(parity == 1)
    def _():
        run(bufs0, bufs1)


def _chunk_spec(kc, bn, n_tiles, j0=0):
    return pl.BlockSpec((kc, bn), lambda j, i: (i, j0 + jnp.minimum(j, n_tiles - 1)))


def _prev_tile(j):
    return jnp.maximum(j - 1, 0)


def _row_tile(j, i):
    return jnp.where(j > 0, i, 0)


def _matmul_kernel(a_ref, chunk_ref, *rest, scaled):
    s_ref = rest[0] if scaled else None
    o_ref, wb0_ref, wb1_ref = rest[-3:]

    def compute(w):
        acc = jnp.dot(a_ref[...], w[0][...], preferred_element_type=F32)
        return acc * s_ref[...] if scaled else acc

    _stream_weights((chunk_ref,), (wb0_ref,), (wb1_ref,), o_ref, compute)


def _matmul(a, b, name, col_scale=None, bm=1024, bn=1024):
    m, k = a.shape
    _, n = b.shape
    n_tiles, row_tiles = n // bn, m // bm
    in_specs = [pl.BlockSpec((bm, k), lambda j, i: (_row_tile(j, i), 0)),
                _chunk_spec(k // row_tiles, bn, n_tiles)]
    args = (a, b)
    if col_scale is not None:
        in_specs.append(pl.BlockSpec((1, bn), lambda j, i: (0, _prev_tile(j))))
        args += (col_scale,)
    return pl.pallas_call(
        functools.partial(_matmul_kernel, scaled=col_scale is not None),
        grid=(n_tiles + 1, row_tiles),
        in_specs=in_specs,
        out_specs=pl.BlockSpec((bm, bn), lambda j, i: (_row_tile(j, i), _prev_tile(j))),
        out_shape=jax.ShapeDtypeStruct((m, n), BF16),
        scratch_shapes=[pltpu.VMEM((k, bn), BF16), pltpu.VMEM((k, bn), BF16)],
        compiler_params=_params(("arbitrary", "arbitrary")),
        name=name,
    )(*args)


def _lru_kernel(xa_ref, za_ref, cw_ref, cb_ref, wa_ref, wi_ref, ba_ref, bi_ref,
                lam_ref, o_ref, xe_ref, hc_ref, *, bm, n_blocks):
    @pl.when(pl.program_id(0) == 0)
    def _():
        xe_ref[0:SUBLANES, :] = jnp.zeros((SUBLANES, xe_ref.shape[1]), F32)
        hc_ref[...] = jnp.zeros(hc_ref.shape, F32)

    groups = bm // SUBLANES
    sub = lax.broadcasted_iota(jnp.int32, (groups, SUBLANES, LRU_BLOCK_W), 1)

    def block(n, carry):
        sl = pl.ds(pl.multiple_of(n * LRU_BLOCK_W, LRU_BLOCK_W), LRU_BLOCK_W)
        xa = xa_ref[:, sl].astype(F32)
        xe_ref[SUBLANES:, sl] = xa
        xc = cb_ref[:, sl] + cw_ref[CONV_W - 1:CONV_W, sl] * xa
        for k in range(CONV_W - 1):
            shift = CONV_W - 1 - k
            xc = xc + cw_ref[k:k + 1, sl] * xe_ref[pl.ds(SUBLANES - shift, bm), sl]
        xe_ref[0:SUBLANES, sl] = xa[bm - SUBLANES:, :]

        xcb = xc.astype(BF16)
        r = _sigmoid(jnp.dot(xcb, wa_ref[n], preferred_element_type=F32) + ba_ref[:, sl])
        i = _sigmoid(jnp.dot(xcb, wi_ref[n], preferred_element_type=F32) + bi_ref[:, sl])
        lam = lam_ref[:, sl]
        softplus_neg_lam = jnp.maximum(-lam, 0.0) + jnp.log1p(jnp.exp(-jnp.abs(lam)))
        log_a = (-LRU_C * r) * softplus_neg_lam
        a = jnp.exp(log_a)
        y = -jnp.tanh(log_a) * (a * a + 1.0)
        u = jnp.where(y > 0.0, y * lax.rsqrt(y), 0.0) * (i * xc)

        a = a.reshape(groups, SUBLANES, LRU_BLOCK_W)
        u = u.reshape(groups, SUBLANES, LRU_BLOCK_W)
        d = 1
        while d < SUBLANES:
            keep = sub >= d
            a_prev = jnp.where(keep, pltpu.roll(a, d, 1), 1.0)
            u_prev = jnp.where(keep, pltpu.roll(u, d, 1), 0.0)
            u = u + a * u_prev
            a = a * a_prev
            d *= 2
        h_prev = hc_ref[0:1, sl]
        hs = []
        for g in range(groups):
            hg = a[g] * h_prev + u[g]
            hs.append(hg)
            h_prev = hg[SUBLANES - 1:SUBLANES, :]
        hc_ref[0:1, sl] = h_prev
        h = jnp.concatenate(hs, axis=0)

        z = za_ref[:, sl].astype(F32)
        o_ref[:, sl] = (h * (z * _sigmoid(z))).astype(o_ref.dtype)
        return carry

    lax.fori_loop(0, n_blocks, block, 0)


def _lru_branch(proj, conv_w, conv_b, w_a, w_i, b_a, b_i, lam, width, bm=256):
    s = proj.shape[0]
    n_blocks = width // LRU_BLOCK_W
    vec = lambda rows: pl.BlockSpec((rows, width), lambda i: (0, 0))
    wspec = pl.BlockSpec((n_blocks, LRU_BLOCK_W, LRU_BLOCK_W), lambda i: (0, 0, 0))
    return pl.pallas_call(
        functools.partial(_lru_kernel, bm=bm, n_blocks=n_blocks),
        grid=(s // bm,),
        in_specs=[pl.BlockSpec((bm, width), lambda i: (i, 0)),
                  pl.BlockSpec((bm, width), lambda i: (i, 1)),
                  vec(CONV_W), vec(1), wspec, wspec, vec(1), vec(1), vec(1)],
        out_specs=pl.BlockSpec((bm, width), lambda i: (i, 0)),
        out_shape=jax.ShapeDtypeStruct((s, width), BF16),
        scratch_shapes=[pltpu.VMEM((bm + SUBLANES, width), F32),
                        pltpu.VMEM((SUBLANES, width), F32)],
        compiler_params=_params(("arbitrary",)),
        name="rglru_branch",
    )(proj, proj, conv_w, conv_b, w_a, w_i, b_a, b_i, lam)


def _attn_kernel(q_ref, k0_ref, k1_ref, k2_ref, v0_ref, v1_ref, v2_ref, zb_ref, w_ref,
                 o_ref, bias_ref, *, n_heads):
    bq = q_ref.shape[0]
    nk = ATT_KEY_BLOCKS * bq
    step = pl.program_id(0)

    @pl.when(step < ATT_KEY_BLOCKS)
    def _():
        qi = lax.broadcasted_iota(jnp.int32, (bq, nk), 0)
        kj = lax.broadcasted_iota(jnp.int32, (bq, nk), 1)
        q_chunk = lax.shift_right_logical(qi, CHUNK.bit_length() - 1)
        k_chunk = lax.shift_right_logical(kj, CHUNK.bit_length() - 1)
        visible = ((k_chunk >= q_chunk) & (k_chunk <= q_chunk + CTX_CHUNKS)
                   & (kj + step * bq >= (ATT_KEY_BLOCKS - 1) * bq))

        def build(h, carry):
            rows = jnp.broadcast_to(w_ref[h], (bq, w_ref.shape[2]))
            toeplitz = pltpu.roll(rows, 0, 1, stride=1, stride_axis=0)
            bias_ref[h] = jnp.where(visible, toeplitz[:, :nk], NEG_INF)
            return carry

        lax.fori_loop(0, n_heads, build, 0)

    def head(h, carry):
        hs = pl.ds(pl.multiple_of(h * ATT_HEAD_DIM, ATT_HEAD_DIM), ATT_HEAD_DIM)
        kh = jnp.concatenate([k0_ref[:, hs], k1_ref[:, hs], k2_ref[:, hs]], axis=0)
        vh = jnp.concatenate([v0_ref[:, hs], v1_ref[:, hs], v2_ref[:, hs]], axis=0)
        s = lax.dot_general(q_ref[:, hs], kh, (((1,), (1,)), ((), ())),
                            preferred_element_type=F32)
        s = s + bias_ref[h]
        m = jnp.max(s, axis=-1, keepdims=True)
        p = jnp.exp2(s - m)
        l = jnp.sum(p, axis=-1, keepdims=True)
        o = jnp.dot(p.astype(BF16), vh, preferred_element_type=F32) / l
        z = zb_ref[:, hs].astype(F32)
        o_ref[:, hs] = (o * (z * _sigmoid(z))).astype(o_ref.dtype)
        return carry

    lax.fori_loop(0, n_heads, head, 0, unroll=4)


def _bias_by_offset(rel_bias):
    n_heads = rel_bias.shape[0]
    bq = ATT_BLOCK_Q
    nk = ATT_KEY_BLOCKS * bq
    off = nk - bq
    length = nk + bq
    n_far = off - REL_CLIP + 1
    n_near = nk - n_far - (2 * REL_CLIP - 1)
    far = rel_bias[:, 2 * REL_CLIP:]
    near = rel_bias[:, :1]
    w = jnp.concatenate([
        jnp.broadcast_to(far, (n_heads, n_far)),
        jnp.flip(rel_bias[:, 1:2 * REL_CLIP], axis=1),
        jnp.broadcast_to(near, (n_heads, n_near + 1)),
        jnp.broadcast_to(far, (n_heads, bq - 1)),
    ], axis=1).astype(F32)
    assert w.shape[1] == length
    return (w * LOG2_E)[:, None, :]


def _attn_branch(proj, rel_bias, att_width, col0):
    s = proj.shape[0]
    n_heads = att_width // ATT_HEAD_DIM
    bq = ATT_BLOCK_Q
    w = _bias_by_offset(rel_bias)

    def kv_spec(col, back):
        return pl.BlockSpec((bq, att_width), lambda i: (jnp.maximum(i - back, 0), col))

    return pl.pallas_call(
        functools.partial(_attn_kernel, n_heads=n_heads),
        grid=(s // bq,),
        in_specs=[pl.BlockSpec((bq, att_width), lambda i: (i, col0)),
                  kv_spec(col0 + 1, 2), kv_spec(col0 + 1, 1), kv_spec(col0 + 1, 0),
                  kv_spec(col0 + 2, 2), kv_spec(col0 + 2, 1), kv_spec(col0 + 2, 0),
                  pl.BlockSpec((bq, att_width), lambda i: (i, col0 + 3)),
                  pl.BlockSpec(w.shape, lambda i: (0, 0, 0))],
        out_specs=pl.BlockSpec((bq, att_width), lambda i: (i, 0)),
        out_shape=jax.ShapeDtypeStruct((s, att_width), BF16),
        scratch_shapes=[pltpu.VMEM((n_heads, bq, ATT_KEY_BLOCKS * bq), F32)],
        compiler_params=_params(("arbitrary",)),
        name="chunk_attention",
    )(proj, proj, proj, proj, proj, proj, proj, proj, w)


def _merge_kernel(ya_ref, yb_ref, ca_ref, cb_ref, ga_ref, gb_ref, o_ref,
                  wa0_ref, wb0_ref, wa1_ref, wb1_ref):
    def compute(w):
        pa = jnp.dot(ya_ref[...], w[0][...], preferred_element_type=F32)
        pb = jnp.dot(yb_ref[...], w[1][...], preferred_element_type=F32)
        ga = jax.nn.sigmoid(ga_ref[...].astype(F32))
        gb = jax.nn.sigmoid(gb_ref[...].astype(F32))
        return ga * pa + gb * pb

    _stream_weights((ca_ref, cb_ref), (wa0_ref, wb0_ref), (wa1_ref, wb1_ref), o_ref, compute)


def _merge(y_a, y_b, w_pa, w_pb, proj, gate_start, bm=512, bn=1024):
    m, ka = y_a.shape
    kb = y_b.shape[1]
    n = w_pa.shape[1]
    n_tiles, row_tiles = n // bn, m // bm
    gate_col0 = gate_start // bn
    return pl.pallas_call(
        _merge_kernel,
        grid=(n_tiles + 1, row_tiles),
        in_specs=[pl.BlockSpec((bm, ka), lambda j, i: (_row_tile(j, i), 0)),
                  pl.BlockSpec((bm, kb), lambda j, i: (_row_tile(j, i), 0)),
                  _chunk_spec(ka // row_tiles, bn, n_tiles),
                  _chunk_spec(kb // row_tiles, bn, n_tiles),
                  pl.BlockSpec((bm, bn), lambda j, i: (_row_tile(j, i), gate_col0 + _prev_tile(j))),
                  pl.BlockSpec((bm, bn),
                               lambda j, i: (_row_tile(j, i), gate_col0 + n_tiles + _prev_tile(j)))],
        out_specs=pl.BlockSpec((bm, bn), lambda j, i: (_row_tile(j, i), _prev_tile(j))),
        out_shape=jax.ShapeDtypeStruct((m, n), BF16),
        scratch_shapes=[pltpu.VMEM((ka, bn), BF16), pltpu.VMEM((kb, bn), BF16),
                        pltpu.VMEM((ka, bn), BF16), pltpu.VMEM((kb, bn), BF16)],
        compiler_params=_params(("arbitrary", "arbitrary")),
        name="branch_merge",
    )(y_a, y_b, w_pa, w_pb, proj, proj)


def _rowwise_kernel(x_ref, t_ref, p_ref, wple_ref, gpost_ref, gpre_ref, gple_ref,
                    h_ref, hn_ref, pn_ref):
    h = x_ref[...] + _rms_norm_f32(t_ref[...].astype(F32), gpost_ref[...])
    h_ref[...] = h
    hn_ref[...] = _rms_norm_f32(h, gpre_ref[...]).astype(hn_ref.dtype)
    pe = jnp.dot(p_ref[...].astype(BF16), wple_ref[...], preferred_element_type=F32)
    pn_ref[...] = _rms_norm_f32(pe, gple_ref[...]).astype(pn_ref.dtype)


def _rowwise(x, t, p, w_ple, g_post, g_ple_pre, g_ple_post, bm=256):
    s, d = x.shape
    pd = p.shape[1]
    row = lambda w: pl.BlockSpec((bm, w), lambda i: (i, 0))
    vec = pl.BlockSpec((1, d), lambda i: (0, 0))
    return pl.pallas_call(
        _rowwise_kernel,
        grid=(s // bm,),
        in_specs=[row(d), row(d), row(pd), pl.BlockSpec((pd, d), lambda i: (0, 0)), vec, vec, vec],
        out_specs=[row(d), row(d), row(d)],
        out_shape=[jax.ShapeDtypeStruct((s, d), F32),
                   jax.ShapeDtypeStruct((s, d), BF16),
                   jax.ShapeDtypeStruct((s, d), BF16)],
        compiler_params=_params(("parallel",)),
        name="residual_norms",
    )(x, t, p, w_ple, g_post, g_ple_pre, g_ple_post)


def _ple_kernel(hn_ref, chunk_ref, h_ref, pn_ref, o_ref, wb0_ref, wb1_ref):
    def compute(w):
        g = jnp.dot(hn_ref[...], w[0][...], preferred_element_type=F32)
        return h_ref[...] + pn_ref[...].astype(F32) * jax.nn.sigmoid(g)

    _stream_weights((chunk_ref,), (wb0_ref,), (wb1_ref,), o_ref, compute)


def _ple_gate(hn, w, h, pn, bm=1024, bn=512):
    m, k = hn.shape
    n = w.shape[1]
    n_tiles, row_tiles = n // bn, m // bm
    tile = pl.BlockSpec((bm, bn), lambda j, i: (_row_tile(j, i), _prev_tile(j)))
    return pl.pallas_call(
        _ple_kernel,
        grid=(n_tiles + 1, row_tiles),
        in_specs=[pl.BlockSpec((bm, k), lambda j, i: (_row_tile(j, i), 0)),
                  _chunk_spec(k // row_tiles, bn, n_tiles),
                  tile, tile],
        out_specs=tile,
        out_shape=jax.ShapeDtypeStruct((m, n), F32),
        scratch_shapes=[pltpu.VMEM((k, bn), BF16), pltpu.VMEM((k, bn), BF16)],
        compiler_params=_params(("arbitrary", "arbitrary")),
        name="ple_gate",
    )(hn, w, h, pn)


def _layer(x, p_i, w_in, conv_w, conv_b, w_rg_a, b_rg_a, w_rg_i, b_rg_i, lru_lambda,
           rel_bias, w_proj_a, w_proj_b, w_out, g_pre, g_post,
           w_ple, w_ple_gate, g_ple_pre, g_ple_post):
    d = x.shape[1]
    lru_width = w_proj_a.shape[0]
    att_width = w_proj_b.shape[0]
    assert lru_width == d and 2 * att_width == d
    row = lambda v: v.reshape(1, -1)

    xn = _rmsnorm(x, row(g_pre))
    q_start = 2 * lru_width
    col_scale = jnp.ones((1, w_in.shape[1]), F32).at[:, q_start:q_start + att_width].set(
        ATT_HEAD_DIM ** -0.5 * LOG2_E)
    proj = _matmul(xn, w_in, "in_proj", col_scale=col_scale)
    y_a = _lru_branch(proj, conv_w, row(conv_b), w_rg_a.astype(BF16), w_rg_i.astype(BF16),
                      row(b_rg_a), row(b_rg_i), row(lru_lambda), lru_width)
    y_b = _attn_branch(proj, rel_bias, att_width, col0=2 * lru_width // att_width)
    merged = _merge(y_a, y_b, w_proj_a, w_proj_b, proj, gate_start=2 * lru_width + 4 * att_width)
    t = _matmul(merged, w_out, "out_proj")
    h, hn, pn = _rowwise(x, t, p_i, w_ple.astype(BF16), row(g_post), row(g_ple_pre), row(g_ple_post))
    return _ple_gate(hn, w_ple_gate, h, pn)


def kernel(x, p, w_in, conv_w, conv_b, w_rg_a, b_rg_a, w_rg_i, b_rg_i, lru_lambda, rel_bias,
           w_proj_a, w_proj_b, w_out, g_pre, g_post, w_ple, w_ple_gate, g_ple_pre, g_ple_post):
    batch = x.shape[0]
    outs = []
    for b in range(batch):
        h = x[b]
        for l in range(w_in.shape[0]):
            h = _layer(h, p[l, b], w_in[l], conv_w[l], conv_b[l], w_rg_a[l], b_rg_a[l],
                       w_rg_i[l], b_rg_i[l], lru_lambda[l], rel_bias[l], w_proj_a[l],
                       w_proj_b[l], w_out[l], g_pre[l], g_post[l], w_ple[l],
                       w_ple_gate[l], g_ple_pre[l], g_ple_post[l])
        outs.append(h)
    return jnp.stack(outs, axis=0)
```

```python
import functools
import math

import jax
import jax.numpy as jnp
import numpy as np
from jax import lax
from jax.experimental import pallas as pl
from jax.experimental.pallas import tpu as pltpu

F32 = jnp.float32
BF16 = jnp.bfloat16

EPS = 1e-6
NEG_INF = -1e30
LRU_C = 8.0
LOG2_E = math.log2(math.e)

CHUNK = 64
CTX_CHUNKS = 8
REL_CLIP = 128
ATT_HEAD_DIM = 128
LRU_BLOCK_W = 256
LRU_GROUP_W = 2 * LRU_BLOCK_W
LRU_CHUNK = 64
CONV_W = 4

SUBLANES = 8
ATT_BLOCK_Q = 256
ATT_KEY_BLOCKS = 1 + (CTX_CHUNKS * CHUNK) // ATT_BLOCK_Q
VMEM_LIMIT_BYTES = 56 * 1024 * 1024


def _params(semantics):
    return pltpu.CompilerParams(dimension_semantics=semantics,
                                vmem_limit_bytes=VMEM_LIMIT_BYTES)


def _sigmoid(x):
    return 0.5 * jnp.tanh(0.5 * x) + 0.5


def _rms_norm_f32(x, g):
    ms = jnp.mean(x * x, axis=-1, keepdims=True)
    return (x * lax.rsqrt(ms + EPS)) * g


def _rmsnorm_kernel(x_ref, g_ref, o_ref):
    o_ref[...] = _rms_norm_f32(x_ref[...], g_ref[...]).astype(o_ref.dtype)


def _rmsnorm(x, g, bm=256):
    s, d = x.shape
    return pl.pallas_call(
        _rmsnorm_kernel,
        grid=(s // bm,),
        in_specs=[pl.BlockSpec((bm, d), lambda i: (i, 0)),
                  pl.BlockSpec((1, d), lambda i: (0, 0))],
        out_specs=pl.BlockSpec((bm, d), lambda i: (i, 0)),
        out_shape=jax.ShapeDtypeStruct((s, d), BF16),
        compiler_params=_params(("parallel",)),
        name="rmsnorm_pre",
    )(x, g)


def _cast_chunk(chunk_ref, w_next_ref):
    kc = chunk_ref.shape[0]
    rows = pl.ds(pl.multiple_of(pl.program_id(1) * kc, kc), kc)
    w_next_ref[rows, :] = chunk_ref[...].astype(BF16)


def _stream_weights(chunk_refs, bufs0, bufs1, o_ref, compute):
    def run(cur, nxt):
        def cast():
            for chunk_ref, w_next_ref in zip(chunk_refs, nxt):
                _cast_chunk(chunk_ref, w_next_ref)

        @pl.when(pl.program_id(0) == 0)
        def _():
            cast()

        @pl.when(pl.program_id(0) > 0)
        def _():
            cast()
            o_ref[...] = compute(cur).astype(o_ref.dtype)

    parity = lax.rem(pl.program_id(0), 2)

    @pl.when(parity == 0)
    def _():
        run(bufs1, bufs0)

    @pl.when(parity == 1)
    def _():
        run(bufs0, bufs1)


def _chunk_spec(kc, bn, n_tiles, j0=0):
    return pl.BlockSpec((kc, bn), lambda j, i: (i, j0 + jnp.minimum(j, n_tiles - 1)))


def _prev_tile(j):
    return jnp.maximum(j - 1, 0)


def _row_tile(j, i):
    return jnp.where(j > 0, i, 0)


def _matmul_kernel(a_ref, chunk_ref, *rest, scaled):
    s_ref = rest[0] if scaled else None
    o_ref, wb0_ref, wb1_ref = rest[-3:]

    def compute(w):
        acc = jnp.dot(a_ref[...], w[0][...], preferred_element_type=F32)
        return acc * s_ref[...] if scaled else acc

    _stream_weights((chunk_ref,), (wb0_ref,), (wb1_ref,), o_ref, compute)


def _matmul(a, b, name, col_scale=None, col_start=0, bm=1024, bn=1024):
    m, k = a.shape
    n = b.shape[1] - col_start
    n_tiles, row_tiles = n // bn, m // bm
    in_specs = [pl.BlockSpec((bm, k), lambda j, i: (_row_tile(j, i), 0)),
                _chunk_spec(k // row_tiles, bn, n_tiles, j0=col_start // bn)]
    args = (a, b)
    if col_scale is not None:
        in_specs.append(pl.BlockSpec((1, bn), lambda j, i: (0, _prev_tile(j))))
        args += (col_scale,)
    return pl.pallas_call(
        functools.partial(_matmul_kernel, scaled=col_scale is not None),
        grid=(n_tiles + 1, row_tiles),
        in_specs=in_specs,
        out_specs=pl.BlockSpec((bm, bn), lambda j, i: (_row_tile(j, i), _prev_tile(j))),
        out_shape=jax.ShapeDtypeStruct((m, n), BF16),
        scratch_shapes=[pltpu.VMEM((k, bn), BF16), pltpu.VMEM((k, bn), BF16)],
        compiler_params=_params(("arbitrary", "arbitrary")),
        name=name,
    )(*args)


def _lru_piece(acc_ref, cw_ref, cb_ref, wa_ref, wi_ref, ba_ref, bi_ref, lam_ref, o_ref, j, c, h_prev):
    groups = LRU_CHUNK // SUBLANES
    sub = lax.broadcasted_iota(jnp.int32, (groups, SUBLANES, LRU_BLOCK_W), 1)
    cs = slice(j * LRU_BLOCK_W, (j + 1) * LRU_BLOCK_W)
    r0 = SUBLANES + c * LRU_CHUNK
    xa = acc_ref[0, r0:r0 + LRU_CHUNK, cs]
    xc = cb_ref[:, cs] + cw_ref[CONV_W - 1:CONV_W, cs] * xa
    for k in range(CONV_W - 1):
        shift = CONV_W - 1 - k
        xc = xc + cw_ref[k:k + 1, cs] * acc_ref[0, r0 - shift:r0 - shift + LRU_CHUNK, cs]

    xcb = xc.astype(BF16)
    r = _sigmoid(jnp.dot(xcb, wa_ref[j], preferred_element_type=F32) + ba_ref[:, cs])
    i = _sigmoid(jnp.dot(xcb, wi_ref[j], preferred_element_type=F32) + bi_ref[:, cs])
    lam = lam_ref[:, cs]
    softplus_neg_lam = jnp.maximum(-lam, 0.0) + jnp.log1p(jnp.exp(-jnp.abs(lam)))
    log_a = (-LRU_C * r) * softplus_neg_lam
    a = jnp.exp(log_a)
    y = -jnp.tanh(log_a) * (a * a + 1.0)
    u = jnp.where(y > 0.0, y * lax.rsqrt(y), 0.0) * (i * xc)

    a = a.reshape(groups, SUBLANES, LRU_BLOCK_W)
    u = u.reshape(groups, SUBLANES, LRU_BLOCK_W)
    d = 1
    while d < SUBLANES:
        keep = sub >= d
        a_prev = jnp.where(keep, pltpu.roll(a, d, 1), 1.0)
        u_prev = jnp.where(keep, pltpu.roll(u, d, 1), 0.0)
        u = u + a * u_prev
        a = a * a_prev
        d *= 2
    hs = []
    for g in range(groups):
        hg = a[g] * h_prev + u[g]
        hs.append(hg)
        h_prev = hg[SUBLANES - 1:SUBLANES, :]
    h = jnp.concatenate(hs, axis=0)

    z = acc_ref[1, r0:r0 + LRU_CHUNK, cs]
    o_ref[c * LRU_CHUNK:(c + 1) * LRU_CHUNK, cs] = (h * (z * _sigmoid(z))).astype(o_ref.dtype)
    return h_prev


def _lru_kernel(xn_ref, cx_ref, cz_ref, cw_ref, cb_ref, wa_ref, wi_ref, ba_ref, bi_ref, lam_ref,
                o_ref, wbuf_ref, acc0_ref, acc1_ref, halo_ref, hc_ref, *, bm, rt, n_groups):
    s = pl.program_id(0)
    n_tiles = n_groups * rt
    kc = cx_ref.shape[0]
    n_chunks = bm // LRU_CHUNK
    kp = xn_ref.shape[1] // n_chunks

    def round_chunk():
        slot = lax.rem(s // rt, 2)
        rows = pl.ds(pl.multiple_of(lax.rem(s, rt) * kc, kc), kc)
        wbuf_ref[slot, 0, rows, :] = cx_ref[...].astype(BF16)
        wbuf_ref[slot, 1, rows, :] = cz_ref[...].astype(BF16)

    @pl.when(s == 0)
    def _():
        acc0_ref[...] = jnp.zeros(acc0_ref.shape, F32)
        acc1_ref[...] = jnp.zeros(acc1_ref.shape, F32)
        halo_ref[...] = jnp.zeros(halo_ref.shape, F32)
        hc_ref[...] = jnp.zeros(hc_ref.shape, F32)

    @pl.when(s < rt)
    def _():
        round_chunk()

    w_slot = lax.rem(jnp.clip(s - rt, 0, n_tiles - 1) // rt, 2)
    first = lax.rem(jnp.clip(s - rt - 1, 0, n_tiles - 1), rt) == 0

    def step(acc_w_ref, acc_r_ref):
        acc_r_ref[0, 0:SUBLANES, :] = jnp.where(first, 0.0, halo_ref[...])
        halo_ref[...] = acc_r_ref[0, bm:bm + SUBLANES, :]
        for j in range(LRU_GROUP_W // LRU_BLOCK_W):
            cs = slice(j * LRU_BLOCK_W, (j + 1) * LRU_BLOCK_W)
            h_prev = jnp.where(first, 0.0, hc_ref[0:1, cs])
            for c in range(n_chunks):
                ks = slice(c * kp, (c + 1) * kp)
                part = jnp.dot(xn_ref[:, ks], wbuf_ref[w_slot, j, ks, :], preferred_element_type=F32)
                if c == 0:
                    acc_w_ref[j, SUBLANES:, :] = part
                else:
                    acc_w_ref[j, SUBLANES:, :] += part
                h_prev = _lru_piece(acc_r_ref, cw_ref, cb_ref, wa_ref, wi_ref, ba_ref, bi_ref, lam_ref,
                                    o_ref, j, c, h_prev)
            hc_ref[0:1, cs] = h_prev

        @pl.when(s < n_tiles)
        def _():
            round_chunk()

    parity = lax.rem(s, 2)

    @pl.when((s >= rt) & (parity == 0))
    def _():
        step(acc0_ref, acc1_ref)

    @pl.when((s >= rt) & (parity == 1))
    def _():
        step(acc1_ref, acc0_ref)


def _lru_branch(xn, w_in, conv_w, conv_b, w_a, w_i, b_a, b_i, lam, width, bm=512):
    s, d = xn.shape
    gw = LRU_GROUP_W
    n_groups = width // gw
    rt = s // bm
    n_tiles = n_groups * rt
    kc = d // rt

    def cast_at(step):
        return lax.rem(step, rt), jnp.minimum(step // rt, n_groups - 1)

    def mm_at(step):
        t = jnp.clip(step - rt, 0, n_tiles - 1)
        return lax.rem(t, rt), t // rt

    def ep_at(step):
        t = jnp.clip(step - rt - 1, 0, n_tiles - 1)
        return lax.rem(t, rt), t // rt

    vec = lambda rows: pl.BlockSpec((rows, gw), lambda i: (0, ep_at(i)[1]))
    wspec = pl.BlockSpec((gw // LRU_BLOCK_W, LRU_BLOCK_W, LRU_BLOCK_W), lambda i: (ep_at(i)[1], 0, 0))
    return pl.pallas_call(
        functools.partial(_lru_kernel, bm=bm, rt=rt, n_groups=n_groups),
        grid=(rt + n_tiles + 1,),
        in_specs=[pl.BlockSpec((bm, d), lambda i: (mm_at(i)[0], 0)),
                  pl.BlockSpec((kc, gw), lambda i: cast_at(i)),
                  pl.BlockSpec((kc, gw), lambda i: (cast_at(i)[0], n_groups + cast_at(i)[1])),
                  vec(CONV_W), vec(1), wspec, wspec, vec(1), vec(1), vec(1)],
        out_specs=pl.BlockSpec((bm, gw), lambda i: ep_at(i)),
        out_shape=jax.ShapeDtypeStruct((s, width), BF16),
        scratch_shapes=[pltpu.VMEM((2, 2, d, gw), BF16),
                        pltpu.VMEM((2, bm + SUBLANES, gw), F32),
                        pltpu.VMEM((2, bm + SUBLANES, gw), F32),
                        pltpu.VMEM((SUBLANES, gw), F32),
                        pltpu.VMEM((SUBLANES, gw), F32)],
        compiler_params=_params(("arbitrary",)),
        name="rglru_branch",
    )(xn, w_in, w_in, conv_w, conv_b, w_a, w_i, b_a, b_i, lam)


def _attn_kernel(q_ref, k0_ref, k1_ref, k2_ref, v0_ref, v1_ref, v2_ref, zb_ref, w_ref,
                 o_ref, bias_ref, *, n_heads):
    bq = q_ref.shape[0]
    nk = ATT_KEY_BLOCKS * bq
    step = pl.program_id(0)

    @pl.when(step < ATT_KEY_BLOCKS)
    def _():
        qi = lax.broadcasted_iota(jnp.int32, (bq, nk), 0)
        kj = lax.broadcasted_iota(jnp.int32, (bq, nk), 1)
        q_chunk = lax.shift_right_logical(qi, CHUNK.bit_length() - 1)
        k_chunk = lax.shift_right_logical(kj, CHUNK.bit_length() - 1)
        visible = ((k_chunk >= q_chunk) & (k_chunk <= q_chunk + CTX_CHUNKS)
                   & (kj + step * bq >= (ATT_KEY_BLOCKS - 1) * bq))

        def build(h, carry):
            rows = jnp.broadcast_to(w_ref[h], (bq, w_ref.shape[2]))
            toeplitz = pltpu.roll(rows, 0, 1, stride=1, stride_axis=0)
            bias_ref[h] = jnp.where(visible, toeplitz[:, :nk], NEG_INF)
            return carry

        lax.fori_loop(0, n_heads, build, 0)

    def head(h, carry):
        hs = pl.ds(pl.multiple_of(h * ATT_HEAD_DIM, ATT_HEAD_DIM), ATT_HEAD_DIM)
        kh = jnp.concatenate([k0_ref[:, hs], k1_ref[:, hs], k2_ref[:, hs]], axis=0)
        vh = jnp.concatenate([v0_ref[:, hs], v1_ref[:, hs], v2_ref[:, hs]], axis=0)
        s = lax.dot_general(q_ref[:, hs], kh, (((1,), (1,)), ((), ())),
                            preferred_element_type=F32)
        s = s + bias_ref[h]
        m = jnp.max(s, axis=-1, keepdims=True)
        p = jnp.exp2(s - m)
        l = jnp.sum(p, axis=-1, keepdims=True)
        o = jnp.dot(p.astype(BF16), vh, preferred_element_type=F32) / l
        z = zb_ref[:, hs].astype(F32)
        o_ref[:, hs] = (o * (z * _sigmoid(z))).astype(o_ref.dtype)
        return carry

    lax.fori_loop(0, n_heads, head, 0, unroll=4)


def _bias_by_offset(rel_bias):
    n_heads = rel_bias.shape[0]
    bq = ATT_BLOCK_Q
    nk = ATT_KEY_BLOCKS * bq
    off = nk - bq
    length = nk + bq
    n_far = off - REL_CLIP + 1
    n_near = nk - n_far - (2 * REL_CLIP - 1)
    far = rel_bias[:, 2 * REL_CLIP:]
    near = rel_bias[:, :1]
    w = jnp.concatenate([
        jnp.broadcast_to(far, (n_heads, n_far)),
        jnp.flip(rel_bias[:, 1:2 * REL_CLIP], axis=1),
        jnp.broadcast_to(near, (n_heads, n_near + 1)),
        jnp.broadcast_to(far, (n_heads, bq - 1)),
    ], axis=1).astype(F32)
    assert w.shape[1] == length
    return (w * LOG2_E)[:, None, :]


def _attn_branch(proj, rel_bias, att_width, col0):
    s = proj.shape[0]
    n_heads = att_width // ATT_HEAD_DIM
    bq = ATT_BLOCK_Q
    w = _bias_by_offset(rel_bias)

    def kv_spec(col, back):
        return pl.BlockSpec((bq, att_width), lambda i: (jnp.maximum(i - back, 0), col))

    return pl.pallas_call(
        functools.partial(_attn_kernel, n_heads=n_heads),
        grid=(s // bq,),
        in_specs=[pl.BlockSpec((bq, att_width), lambda i: (i, col0)),
                  kv_spec(col0 + 1, 2), kv_spec(col0 + 1, 1), kv_spec(col0 + 1, 0),
                  kv_spec(col0 + 2, 2), kv_spec(col0 + 2, 1), kv_spec(col0 + 2, 0),
                  pl.BlockSpec((bq, att_width), lambda i: (i, col0 + 3)),
                  pl.BlockSpec(w.shape, lambda i: (0, 0, 0))],
        out_specs=pl.BlockSpec((bq, att_width), lambda i: (i, 0)),
        out_shape=jax.ShapeDtypeStruct((s, att_width), BF16),
        scratch_shapes=[pltpu.VMEM((n_heads, bq, ATT_KEY_BLOCKS * bq), F32)],
        compiler_params=_params(("arbitrary",)),
        name="chunk_attention",
    )(proj, proj, proj, proj, proj, proj, proj, proj, w)


def _merge_kernel(ya_ref, yb_ref, ca_ref, cb_ref, ga_ref, gb_ref, o_ref,
                  wa0_ref, wb0_ref, wa1_ref, wb1_ref):
    def compute(w):
        pa = jnp.dot(ya_ref[...], w[0][...], preferred_element_type=F32)
        pb = jnp.dot(yb_ref[...], w[1][...], preferred_element_type=F32)
        ga = jax.nn.sigmoid(ga_ref[...].astype(F32))
        gb = jax.nn.sigmoid(gb_ref[...].astype(F32))
        return ga * pa + gb * pb

    _stream_weights((ca_ref, cb_ref), (wa0_ref, wb0_ref), (wa1_ref, wb1_ref), o_ref, compute)


def _merge(y_a, y_b, w_pa, w_pb, proj, gate_start, bm=512, bn=1024):
    m, ka = y_a.shape
    kb = y_b.shape[1]
    n = w_pa.shape[1]
    n_tiles, row_tiles = n // bn, m // bm
    gate_col0 = gate_start // bn
    return pl.pallas_call(
        _merge_kernel,
        grid=(n_tiles + 1, row_tiles),
        in_specs=[pl.BlockSpec((bm, ka), lambda j, i: (_row_tile(j, i), 0)),
                  pl.BlockSpec((bm, kb), lambda j, i: (_row_tile(j, i), 0)),
                  _chunk_spec(ka // row_tiles, bn, n_tiles),
                  _chunk_spec(kb // row_tiles, bn, n_tiles),
                  pl.BlockSpec((bm, bn), lambda j, i: (_row_tile(j, i), gate_col0 + _prev_tile(j))),
                  pl.BlockSpec((bm, bn),
                               lambda j, i: (_row_tile(j, i), gate_col0 + n_tiles + _prev_tile(j)))],
        out_specs=pl.BlockSpec((bm, bn), lambda j, i: (_row_tile(j, i), _prev_tile(j))),
        out_shape=jax.ShapeDtypeStruct((m, n), BF16),
        scratch_shapes=[pltpu.VMEM((ka, bn), BF16), pltpu.VMEM((kb, bn), BF16),
                        pltpu.VMEM((ka, bn), BF16), pltpu.VMEM((kb, bn), BF16)],
        compiler_params=_params(("arbitrary", "arbitrary")),
        name="branch_merge",
    )(y_a, y_b, w_pa, w_pb, proj, proj)


def _rowwise_kernel(x_ref, t_ref, p_ref, wple_ref, gpost_ref, gpre_ref, gple_ref,
                    h_ref, hn_ref, pn_ref):
    h = x_ref[...] + _rms_norm_f32(t_ref[...].astype(F32), gpost_ref[...])
    h_ref[...] = h
    hn_ref[...] = _rms_norm_f32(h, gpre_ref[...]).astype(hn_ref.dtype)
    pe = jnp.dot(p_ref[...].astype(BF16), wple_ref[...], preferred_element_type=F32)
    pn_ref[...] = _rms_norm_f32(pe, gple_ref[...]).astype(pn_ref.dtype)


def _rowwise(x, t, p, w_ple, g_post, g_ple_pre, g_ple_post, bm=256):
    s, d = x.shape
    pd = p.shape[1]
    row = lambda w: pl.BlockSpec((bm, w), lambda i: (i, 0))
    vec = pl.BlockSpec((1, d), lambda i: (0, 0))
    return pl.pallas_call(
        _rowwise_kernel,
        grid=(s // bm,),
        in_specs=[row(d), row(d), row(pd), pl.BlockSpec((pd, d), lambda i: (0, 0)), vec, vec, vec],
        out_specs=[row(d), row(d), row(d)],
        out_shape=[jax.ShapeDtypeStruct((s, d), F32),
                   jax.ShapeDtypeStruct((s, d), BF16),
                   jax.ShapeDtypeStruct((s, d), BF16)],
        compiler_params=_params(("parallel",)),
        name="residual_norms",
    )(x, t, p, w_ple, g_post, g_ple_pre, g_ple_post)


def _ple_kernel(hn_ref, chunk_ref, h_ref, pn_ref, o_ref, wb0_ref, wb1_ref):
    def compute(w):
        g = jnp.dot(hn_ref[...], w[0][...], preferred_element_type=F32)
        return h_ref[...] + pn_ref[...].astype(F32) * jax.nn.sigmoid(g)

    _stream_weights((chunk_ref,), (wb0_ref,), (wb1_ref,), o_ref, compute)


def _ple_gate(hn, w, h, pn, bm=1024, bn=512):
    m, k = hn.shape
    n = w.shape[1]
    n_tiles, row_tiles = n // bn, m // bm
    tile = pl.BlockSpec((bm, bn), lambda j, i: (_row_tile(j, i), _prev_tile(j)))
    return pl.pallas_call(
        _ple_kernel,
        grid=(n_tiles + 1, row_tiles),
        in_specs=[pl.BlockSpec((bm, k), lambda j, i: (_row_tile(j, i), 0)),
                  _chunk_spec(k // row_tiles, bn, n_tiles),
                  tile, tile],
        out_specs=tile,
        out_shape=jax.ShapeDtypeStruct((m, n), F32),
        scratch_shapes=[pltpu.VMEM((k, bn), BF16), pltpu.VMEM((k, bn), BF16)],
        compiler_params=_params(("arbitrary", "arbitrary")),
        name="ple_gate",
    )(hn, w, h, pn)


def _layer(x, p_i, w_in, conv_w, conv_b, w_rg_a, b_rg_a, w_rg_i, b_rg_i, lru_lambda,
           rel_bias, w_proj_a, w_proj_b, w_out, g_pre, g_post,
           w_ple, w_ple_gate, g_ple_pre, g_ple_post):
    d = x.shape[1]
    lru_width = w_proj_a.shape[0]
    att_width = w_proj_b.shape[0]
    assert lru_width == d and 2 * att_width == d
    row = lambda v: v.reshape(1, -1)

    xn = _rmsnorm(x, row(g_pre))
    y_a = _lru_branch(xn, w_in, conv_w, row(conv_b), w_rg_a.astype(BF16), w_rg_i.astype(BF16),
                      row(b_rg_a), row(b_rg_i), row(lru_lambda), lru_width)
    rest = w_in.shape[1] - 2 * lru_width
    col_scale = jnp.ones((1, rest), F32).at[:, :att_width].set(ATT_HEAD_DIM ** -0.5 * LOG2_E)
    proj = _matmul(xn, w_in, "in_proj", col_scale=col_scale, col_start=2 * lru_width)
    y_b = _attn_branch(proj, rel_bias, att_width, col0=0)
    merged = _merge(y_a, y_b, w_proj_a, w_proj_b, proj, gate_start=4 * att_width)
    t = _matmul(merged, w_out, "out_proj")
    h, hn, pn = _rowwise(x, t, p_i, w_ple.astype(BF16), row(g_post), row(g_ple_pre), row(g_ple_post))
    return _ple_gate(hn, w_ple_gate, h, pn)


def kernel(x, p, w_in, conv_w, conv_b, w_rg_a, b_rg_a, w_rg_i, b_rg_i, lru_lambda, rel_bias,
           w_proj_a, w_proj_b, w_out, g_pre, g_post, w_ple, w_ple_gate, g_ple_pre, g_ple_post):
    batch = x.shape[0]
    outs = []
    for b in range(batch):
        h = x[b]
        for l in range(w_in.shape[0]):
            h = _layer(h, p[l, b], w_in[l], conv_w[l], conv_b[l], w_rg_a[l], b_rg_a[l],
                       w_rg_i[l], b_rg_i[l], lru_lambda[l], rel_bias[l], w_proj_a[l],
                       w_proj_b[l], w_out[l], g_pre[l], g_post[l], w_ple[l],
                       w_ple_gate[l], g_ple_pre[l], g_ple_post[l])
        outs.append(h)
    return jnp.stack(outs, axis=0)
```

```python
import functools
import math

import jax
import jax.numpy as jnp
from jax import lax
from jax.experimental import pallas as pl
from jax.experimental.pallas import tpu as pltpu

F32 = jnp.float32
BF16 = jnp.bfloat16

EPS = 1e-6
NEG_INF = -1e30
LRU_C = 8.0
LOG2_E = math.log2(math.e)

CHUNK = 64
CTX_CHUNKS = 8
REL_CLIP = 128
ATT_HEAD_DIM = 128
LRU_BLOCK_W = 256
CONV_W = 4

SUBLANES = 8
ATT_BLOCK_Q = 256
ATT_KEY_BLOCKS = 1 + (CTX_CHUNKS * CHUNK) // ATT_BLOCK_Q
VMEM_LIMIT_BYTES = 56 * 1024 * 1024


def _params(semantics):
    return pltpu.CompilerParams(dimension_semantics=semantics,
                                vmem_limit_bytes=VMEM_LIMIT_BYTES)


def _sigmoid(x):
    return 0.5 * jnp.tanh(0.5 * x) + 0.5


def _rms_norm_f32(x, g):
    ms = jnp.mean(x * x, axis=-1, keepdims=True)
    return (x * lax.rsqrt(ms + EPS)) * g


def _rmsnorm_kernel(x_ref, g_ref, o_ref):
    o_ref[...] = _rms_norm_f32(x_ref[...], g_ref[...]).astype(o_ref.dtype)


def _rmsnorm(x, g, bm=256):
    s, d = x.shape
    return pl.pallas_call(
        _rmsnorm_kernel,
        grid=(s // bm,),
        in_specs=[pl.BlockSpec((bm, d), lambda i: (i, 0)),
                  pl.BlockSpec((1, d), lambda i: (0, 0))],
        out_specs=pl.BlockSpec((bm, d), lambda i: (i, 0)),
        out_shape=jax.ShapeDtypeStruct((s, d), BF16),
        compiler_params=_params(("parallel",)),
        name="rmsnorm_pre",
    )(x, g)


def _cast_chunk(chunk_ref, w_next_ref):
    kc = chunk_ref.shape[0]
    rows = pl.ds(pl.multiple_of(pl.program_id(1) * kc, kc), kc)
    w_next_ref[rows, :] = chunk_ref[...].astype(BF16)


def _stream_weights(chunk_refs, bufs0, bufs1, o_ref, compute):
    def run(cur, nxt):
        def cast():
            for chunk_ref, w_next_ref in zip(chunk_refs, nxt):
                _cast_chunk(chunk_ref, w_next_ref)

        @pl.when(pl.program_id(0) == 0)
        def _():
            cast()

        @pl.when(pl.program_id(0) > 0)
        def _():
            cast()
            o_ref[...] = compute(cur).astype(o_ref.dtype)

    parity = lax.rem(pl.program_id(0), 2)

    @pl.when(parity == 0)
    def _():
        run(bufs1, bufs0)

    @pl.when(parity == 1)
    def _():
        run(bufs0, bufs1)


def _chunk_spec(kc, bn, n_tiles, j0=0):
    return pl.BlockSpec((kc, bn), lambda j, i: (i, j0 + jnp.minimum(j, n_tiles - 1)))


def _prev_tile(j):
    return jnp.maximum(j - 1, 0)


def _row_tile(j, i):
    return jnp.where(j > 0, i, 0)


def _matmul_kernel(a_ref, chunk_ref, *rest, scaled):
    s_ref = rest[0] if scaled else None
    o_ref, wb0_ref, wb1_ref = rest[-3:]

    def compute(w):
        acc = jnp.dot(a_ref[...], w[0][...], preferred_element_type=F32)
        return acc * s_ref[...] if scaled else acc

    _stream_weights((chunk_ref,), (wb0_ref,), (wb1_ref,), o_ref, compute)


def _matmul(a, b, name, col_scale=None, bm=1024, bn=1024):
    m, k = a.shape
    _, n = b.shape
    n_tiles, row_tiles = n // bn, m // bm
    in_specs = [pl.BlockSpec((bm, k), lambda j, i: (_row_tile(j, i), 0)),
                _chunk_spec(k // row_tiles, bn, n_tiles)]
    args = (a, b)
    if col_scale is not None:
        in_specs.append(pl.BlockSpec((1, bn), lambda j, i: (0, _prev_tile(j))))
        args += (col_scale,)
    return pl.pallas_call(
        functools.partial(_matmul_kernel, scaled=col_scale is not None),
        grid=(n_tiles + 1, row_tiles),
        in_specs=in_specs,
        out_specs=pl.BlockSpec((bm, bn), lambda j, i: (_row_tile(j, i), _prev_tile(j))),
        out_shape=jax.ShapeDtypeStruct((m, n), BF16),
        scratch_shapes=[pltpu.VMEM((k, bn), BF16), pltpu.VMEM((k, bn), BF16)],
        compiler_params=_params(("arbitrary", "arbitrary")),
        name=name,
    )(*args)


def _lru_kernel(xa_ref, za_ref, cw_ref, cb_ref, wa_ref, wi_ref, ba_ref, bi_ref,
                lam_ref, o_ref, xe_ref, hc_ref, *, bm, n_blocks):
    @pl.when(pl.program_id(0) == 0)
    def _():
        xe_ref[0:SUBLANES, :] = jnp.zeros((SUBLANES, xe_ref.shape[1]), F32)
        hc_ref[...] = jnp.zeros(hc_ref.shape, F32)

    groups = bm // SUBLANES
    sub = lax.broadcasted_iota(jnp.int32, (groups, SUBLANES, LRU_BLOCK_W), 1)

    def block(n, carry):
        sl = pl.ds(pl.multiple_of(n * LRU_BLOCK_W, LRU_BLOCK_W), LRU_BLOCK_W)
        xa = xa_ref[:, sl].astype(F32)
        xe_ref[SUBLANES:, sl] = xa
        xc = cb_ref[:, sl] + cw_ref[CONV_W - 1:CONV_W, sl] * xa
        for k in range(CONV_W - 1):
            shift = CONV_W - 1 - k
            xc = xc + cw_ref[k:k + 1, sl] * xe_ref[pl.ds(SUBLANES - shift, bm), sl]
        xe_ref[0:SUBLANES, sl] = xa[bm - SUBLANES:, :]

        xcb = xc.astype(BF16)
        r = _sigmoid(jnp.dot(xcb, wa_ref[n], preferred_element_type=F32) + ba_ref[:, sl])
        i = _sigmoid(jnp.dot(xcb, wi_ref[n], preferred_element_type=F32) + bi_ref[:, sl])
        lam = lam_ref[:, sl]
        softplus_neg_lam = jnp.maximum(-lam, 0.0) + jnp.log1p(jnp.exp(-jnp.abs(lam)))
        log_a = (-LRU_C * r) * softplus_neg_lam
        a = jnp.exp(log_a)
        y = -jnp.tanh(log_a) * (a * a + 1.0)
        u = jnp.where(y > 0.0, y * lax.rsqrt(y), 0.0) * (i * xc)

        a = a.reshape(groups, SUBLANES, LRU_BLOCK_W)
        u = u.reshape(groups, SUBLANES, LRU_BLOCK_W)
        d = 1
        while d < SUBLANES:
            keep = sub >= d
            a_prev = jnp.where(keep, pltpu.roll(a, d, 1), 1.0)
            u_prev = jnp.where(keep, pltpu.roll(u, d, 1), 0.0)
            u = u + a * u_prev
            a = a * a_prev
            d *= 2
        h_prev = hc_ref[0:1, sl]
        hs = []
        for g in range(groups):
            hg = a[g] * h_prev + u[g]
            hs.append(hg)
            h_prev = hg[SUBLANES - 1:SUBLANES, :]
        hc_ref[0:1, sl] = h_prev
        h = jnp.concatenate(hs, axis=0)

        z = za_ref[:, sl].astype(F32)
        o_ref[:, sl] = (h * (z * _sigmoid(z))).astype(o_ref.dtype)
        return carry

    lax.fori_loop(0, n_blocks, block, 0)


def _lru_branch(proj, conv_w, conv_b, w_a, w_i, b_a, b_i, lam, width, bm=256):
    s = proj.shape[0]
    n_blocks = width // LRU_BLOCK_W
    vec = lambda rows: pl.BlockSpec((rows, width), lambda i: (0, 0))
    wspec = pl.BlockSpec((n_blocks, LRU_BLOCK_W, LRU_BLOCK_W), lambda i: (0, 0, 0))
    return pl.pallas_call(
        functools.partial(_lru_kernel, bm=bm, n_blocks=n_blocks),
        grid=(s // bm,),
        in_specs=[pl.BlockSpec((bm, width), lambda i: (i, 0)),
                  pl.BlockSpec((bm, width), lambda i: (i, 1)),
                  vec(CONV_W), vec(1), wspec, wspec, vec(1), vec(1), vec(1)],
        out_specs=pl.BlockSpec((bm, width), lambda i: (i, 0)),
        out_shape=jax.ShapeDtypeStruct((s, width), BF16),
        scratch_shapes=[pltpu.VMEM((bm + SUBLANES, width), F32),
                        pltpu.VMEM((SUBLANES, width), F32)],
        compiler_params=_params(("arbitrary",)),
        name="rglru_branch",
    )(proj, proj, conv_w, conv_b, w_a, w_i, b_a, b_i, lam)


def _attn_kernel(q_ref, k0_ref, k1_ref, k2_ref, v0_ref, v1_ref, v2_ref, zb_ref, w_ref,
                 o_ref, bias_ref, s_ref, p_ref, l_ref, *, n_heads):
    bq = q_ref.shape[0]
    nk = ATT_KEY_BLOCKS * bq
    step = pl.program_id(0)

    @pl.when(step < ATT_KEY_BLOCKS)
    def _():
        qi = lax.broadcasted_iota(jnp.int32, (bq, nk), 0)
        kj = lax.broadcasted_iota(jnp.int32, (bq, nk), 1)
        q_chunk = lax.shift_right_logical(qi, CHUNK.bit_length() - 1)
        k_chunk = lax.shift_right_logical(kj, CHUNK.bit_length() - 1)
        visible = ((k_chunk >= q_chunk) & (k_chunk <= q_chunk + CTX_CHUNKS)
                   & (kj + step * bq >= (ATT_KEY_BLOCKS - 1) * bq))

        def build(h, carry):
            rows = jnp.broadcast_to(w_ref[h], (bq, w_ref.shape[2]))
            toeplitz = pltpu.roll(rows, 0, 1, stride=1, stride_axis=0)
            bias_ref[h] = jnp.where(visible, toeplitz[:, :nk], NEG_INF)
            return carry

        lax.fori_loop(0, n_heads, build, 0)

    def head_cols(h):
        return pl.ds(pl.multiple_of(h * ATT_HEAD_DIM, ATT_HEAD_DIM), ATT_HEAD_DIM)

    def scores(h, carry):
        hs = head_cols(h)
        kh = jnp.concatenate([k0_ref[:, hs], k1_ref[:, hs], k2_ref[:, hs]], axis=0)
        s = lax.dot_general(q_ref[:, hs], kh, (((1,), (1,)), ((), ())),
                            preferred_element_type=F32)
        s_ref[h] = s + bias_ref[h]
        return carry

    def numerators(h, carry):
        s = s_ref[h]
        m = jnp.max(s, axis=-1, keepdims=True)
        p = jnp.exp2(s - m)
        l_ref[h] = jnp.broadcast_to(jnp.sum(p, axis=-1, keepdims=True), l_ref.shape[1:])
        p_ref[h] = p.astype(BF16)
        return carry

    def values(h, carry):
        hs = head_cols(h)
        vh = jnp.concatenate([v0_ref[:, hs], v1_ref[:, hs], v2_ref[:, hs]], axis=0)
        o = jnp.dot(p_ref[h], vh, preferred_element_type=F32) / l_ref[h]
        z = zb_ref[:, hs].astype(F32)
        o_ref[:, hs] = (o * (z * _sigmoid(z))).astype(o_ref.dtype)
        return carry

    lax.fori_loop(0, n_heads, scores, 0, unroll=n_heads)
    lax.fori_loop(0, n_heads, numerators, 0, unroll=4)
    lax.fori_loop(0, n_heads, values, 0, unroll=n_heads)


def _bias_by_offset(rel_bias):
    n_heads = rel_bias.shape[0]
    bq = ATT_BLOCK_Q
    nk = ATT_KEY_BLOCKS * bq
    off = nk - bq
    length = nk + bq
    n_far = off - REL_CLIP + 1
    n_near = nk - n_far - (2 * REL_CLIP - 1)
    far = rel_bias[:, 2 * REL_CLIP:]
    near = rel_bias[:, :1]
    w = jnp.concatenate([
        jnp.broadcast_to(far, (n_heads, n_far)),
        jnp.flip(rel_bias[:, 1:2 * REL_CLIP], axis=1),
        jnp.broadcast_to(near, (n_heads, n_near + 1)),
        jnp.broadcast_to(far, (n_heads, bq - 1)),
    ], axis=1).astype(F32)
    assert w.shape[1] == length
    return (w * LOG2_E)[:, None, :]


def _attn_branch(proj, rel_bias, att_width, col0):
    s = proj.shape[0]
    n_heads = att_width // ATT_HEAD_DIM
    bq = ATT_BLOCK_Q
    nk = ATT_KEY_BLOCKS * bq
    w = _bias_by_offset(rel_bias)

    def kv_spec(col, back):
        return pl.BlockSpec((bq, att_width), lambda i: (jnp.maximum(i - back, 0), col))

    return pl.pallas_call(
        functools.partial(_attn_kernel, n_heads=n_heads),
        grid=(s // bq,),
        in_specs=[pl.BlockSpec((bq, att_width), lambda i: (i, col0)),
                  kv_spec(col0 + 1, 2), kv_spec(col0 + 1, 1), kv_spec(col0 + 1, 0),
                  kv_spec(col0 + 2, 2), kv_spec(col0 + 2, 1), kv_spec(col0 + 2, 0),
                  pl.BlockSpec((bq, att_width), lambda i: (i, col0 + 3)),
                  pl.BlockSpec(w.shape, lambda i: (0, 0, 0))],
        out_specs=pl.BlockSpec((bq, att_width), lambda i: (i, 0)),
        out_shape=jax.ShapeDtypeStruct((s, att_width), BF16),
        scratch_shapes=[pltpu.VMEM((n_heads, bq, nk), F32),
                        pltpu.VMEM((n_heads, bq, nk), F32),
                        pltpu.VMEM((n_heads, bq, nk), BF16),
                        pltpu.VMEM((n_heads, bq, ATT_HEAD_DIM), F32)],
        compiler_params=_params(("arbitrary",)),
        name="chunk_attention",
    )(proj, proj, proj, proj, proj, proj, proj, proj, w)


def _merge_kernel(ya_ref, yb_ref, ca_ref, cb_ref, ga_ref, gb_ref, o_ref,
                  wa0_ref, wb0_ref, wa1_ref, wb1_ref):
    def compute(w):
        pa = jnp.dot(ya_ref[...], w[0][...], preferred_element_type=F32)
        pb = jnp.dot(yb_ref[...], w[1][...], preferred_element_type=F32)
        ga = jax.nn.sigmoid(ga_ref[...].astype(F32))
        gb = jax.nn.sigmoid(gb_ref[...].astype(F32))
        return ga * pa + gb * pb

    _stream_weights((ca_ref, cb_ref), (wa0_ref, wb0_ref), (wa1_ref, wb1_ref), o_ref, compute)


def _merge(y_a, y_b, w_pa, w_pb, proj, gate_start, bm=512, bn=1024):
    m, ka = y_a.shape
    kb = y_b.shape[1]
    n = w_pa.shape[1]
    n_tiles, row_tiles = n // bn, m // bm
    gate_col0 = gate_start // bn
    return pl.pallas_call(
        _merge_kernel,
        grid=(n_tiles + 1, row_tiles),
        in_specs=[pl.BlockSpec((bm, ka), lambda j, i: (_row_tile(j, i), 0)),
                  pl.BlockSpec((bm, kb), lambda j, i: (_row_tile(j, i), 0)),
                  _chunk_spec(ka // row_tiles, bn, n_tiles),
                  _chunk_spec(kb // row_tiles, bn, n_tiles),
                  pl.BlockSpec((bm, bn), lambda j, i: (_row_tile(j, i), gate_col0 + _prev_tile(j))),
                  pl.BlockSpec((bm, bn),
                               lambda j, i: (_row_tile(j, i), gate_col0 + n_tiles + _prev_tile(j)))],
        out_specs=pl.BlockSpec((bm, bn), lambda j, i: (_row_tile(j, i), _prev_tile(j))),
        out_shape=jax.ShapeDtypeStruct((m, n), BF16),
        scratch_shapes=[pltpu.VMEM((ka, bn), BF16), pltpu.VMEM((kb, bn), BF16),
                        pltpu.VMEM((ka, bn), BF16), pltpu.VMEM((kb, bn), BF16)],
        compiler_params=_params(("arbitrary", "arbitrary")),
        name="branch_merge",
    )(y_a, y_b, w_pa, w_pb, proj, proj)


def _rowwise_kernel(x_ref, t_ref, p_ref, wple_ref, gpost_ref, gpre_ref, gple_ref,
                    h_ref, hn_ref, pn_ref):
    h = x_ref[...] + _rms_norm_f32(t_ref[...].astype(F32), gpost_ref[...])
    h_ref[...] = h
    hn_ref[...] = _rms_norm_f32(h, gpre_ref[...]).astype(hn_ref.dtype)
    pe = jnp.dot(p_ref[...].astype(BF16), wple_ref[...], preferred_element_type=F32)
    pn_ref[...] = _rms_norm_f32(pe, gple_ref[...]).astype(pn_ref.dtype)


def _rowwise(x, t, p, w_ple, g_post, g_ple_pre, g_ple_post, bm=256):
    s, d = x.shape
    pd = p.shape[1]
    row = lambda w: pl.BlockSpec((bm, w), lambda i: (i, 0))
    vec = pl.BlockSpec((1, d), lambda i: (0, 0))
    return pl.pallas_call(
        _rowwise_kernel,
        grid=(s // bm,),
        in_specs=[row(d), row(d), row(pd), pl.BlockSpec((pd, d), lambda i: (0, 0)), vec, vec, vec],
        out_specs=[row(d), row(d), row(d)],
        out_shape=[jax.ShapeDtypeStruct((s, d), F32),
                   jax.ShapeDtypeStruct((s, d), BF16),
                   jax.ShapeDtypeStruct((s, d), BF16)],
        compiler_params=_params(("parallel",)),
        name="residual_norms",
    )(x, t, p, w_ple, g_post, g_ple_pre, g_ple_post)


def _ple_kernel(hn_ref, chunk_ref, h_ref, pn_ref, o_ref, wb0_ref, wb1_ref):
    def compute(w):
        g = jnp.dot(hn_ref[...], w[0][...], preferred_element_type=F32)
        return h_ref[...] + pn_ref[...].astype(F32) * jax.nn.sigmoid(g)

    _stream_weights((chunk_ref,), (wb0_ref,), (wb1_ref,), o_ref, compute)


def _ple_gate(hn, w, h, pn, bm=512, bn=1024):
    m, k = hn.shape
    n = w.shape[1]
    n_tiles, row_tiles = n // bn, m // bm
    tile = pl.BlockSpec((bm, bn), lambda j, i: (_row_tile(j, i), _prev_tile(j)))
    return pl.pallas_call(
        _ple_kernel,
        grid=(n_tiles + 1, row_tiles),
        in_specs=[pl.BlockSpec((bm, k), lambda j, i: (_row_tile(j, i), 0)),
                  _chunk_spec(k // row_tiles, bn, n_tiles),
                  tile, tile],
        out_specs=tile,
        out_shape=jax.ShapeDtypeStruct((m, n), F32),
        scratch_shapes=[pltpu.VMEM((k, bn), BF16), pltpu.VMEM((k, bn), BF16)],
        compiler_params=_params(("arbitrary", "arbitrary")),
        name="ple_gate",
    )(hn, w, h, pn)


def _layer(x, p_i, w_in, conv_w, conv_b, w_rg_a, b_rg_a, w_rg_i, b_rg_i, lru_lambda,
           rel_bias, w_proj_a, w_proj_b, w_out, g_pre, g_post,
           w_ple, w_ple_gate, g_ple_pre, g_ple_post):
    d = x.shape[1]
    lru_width = w_proj_a.shape[0]
    att_width = w_proj_b.shape[0]
    assert lru_width == d and 2 * att_width == d
    row = lambda v: v.reshape(1, -1)

    xn = _rmsnorm(x, row(g_pre))
    q_start = 2 * lru_width
    col_scale = jnp.ones((1, w_in.shape[1]), F32).at[:, q_start:q_start + att_width].set(
        ATT_HEAD_DIM ** -0.5 * LOG2_E)
    proj = _matmul(xn, w_in, "in_proj", col_scale=col_scale)
    y_a = _lru_branch(proj, conv_w, row(conv_b), w_rg_a.astype(BF16), w_rg_i.astype(BF16),
                      row(b_rg_a), row(b_rg_i), row(lru_lambda), lru_width)
    y_b = _attn_branch(proj, rel_bias, att_width, col0=2 * lru_width // att_width)
    merged = _merge(y_a, y_b, w_proj_a, w_proj_b, proj, gate_start=2 * lru_width + 4 * att_width)
    t = _matmul(merged, w_out, "out_proj")
    h, hn, pn = _rowwise(x, t, p_i, w_ple.astype(BF16), row(g_post), row(g_ple_pre), row(g_ple_post))
    return _ple_gate(hn, w_ple_gate, h, pn)


def kernel(x, p, w_in, conv_w, conv_b, w_rg_a, b_rg_a, w_rg_i, b_rg_i, lru_lambda, rel_bias,
           w_proj_a, w_proj_b, w_out, g_pre, g_post, w_ple, w_ple_gate, g_ple_pre, g_ple_post):
    batch = x.shape[0]
    outs = []
    for b in range(batch):
        h = x[b]
        for l in range(w_in.shape[0]):
            h = _layer(h, p[l, b], w_in[l], conv_w[l], conv_b[l], w_rg_a[l], b_rg_a[l],
                       w_rg_i[l], b_rg_i[l], lru_lambda[l], rel_bias[l], w_proj_a[l],
                       w_proj_b[l], w_out[l], g_pre[l], g_post[l], w_ple[l],
                       w_ple_gate[l], g_ple_pre[l], g_ple_post[l])
        outs.append(h)
    return jnp.stack(outs, axis=0)
```

```python
import functools
import math

import jax
import jax.numpy as jnp
from jax import lax
from jax.experimental import pallas as pl
from jax.experimental.pallas import tpu as pltpu

F32 = jnp.float32
BF16 = jnp.bfloat16

EPS = 1e-6
NEG_INF = -1e30
LRU_C = 8.0
LOG2_E = math.log2(math.e)

CHUNK = 64
CTX_CHUNKS = 8
REL_CLIP = 128
ATT_HEAD_DIM = 128
LRU_BLOCK_W = 256
CONV_W = 4

SUBLANES = 8
LANES = 128
ATT_BLOCK_Q = 256
ATT_KEY_BLOCKS = 1 + (CTX_CHUNKS * CHUNK) // ATT_BLOCK_Q
VMEM_LIMIT_BYTES = 56 * 1024 * 1024


def _params(semantics):
    return pltpu.CompilerParams(dimension_semantics=semantics,
                                vmem_limit_bytes=VMEM_LIMIT_BYTES)


def _sigmoid(x):
    return 0.5 * jnp.tanh(0.5 * x) + 0.5


def _rms_norm_f32(x, g):
    ms = jnp.mean(x * x, axis=-1, keepdims=True)
    return (x * lax.rsqrt(ms + EPS)) * g


def _rmsnorm_kernel(x_ref, g_ref, o_ref):
    o_ref[...] = _rms_norm_f32(x_ref[...], g_ref[...]).astype(o_ref.dtype)


def _rmsnorm(x, g, bm=256):
    s, d = x.shape
    return pl.pallas_call(
        _rmsnorm_kernel,
        grid=(s // bm,),
        in_specs=[pl.BlockSpec((bm, d), lambda i: (i, 0)),
                  pl.BlockSpec((1, d), lambda i: (0, 0))],
        out_specs=pl.BlockSpec((bm, d), lambda i: (i, 0)),
        out_shape=jax.ShapeDtypeStruct((s, d), BF16),
        compiler_params=_params(("parallel",)),
        name="rmsnorm_pre",
    )(x, g)


def _cast_chunk(chunk_ref, w_next_ref):
    kc = chunk_ref.shape[0]
    rows = pl.ds(pl.multiple_of(pl.program_id(1) * kc, kc), kc)
    w_next_ref[rows, :] = chunk_ref[...].astype(BF16)


def _stream_weights(chunk_refs, bufs0, bufs1, o_ref, compute):
    def run(cur, nxt):
        def cast():
            for chunk_ref, w_next_ref in zip(chunk_refs, nxt):
                _cast_chunk(chunk_ref, w_next_ref)

        @pl.when(pl.program_id(0) == 0)
        def _():
            cast()

        @pl.when(pl.program_id(0) > 0)
        def _():
            cast()
            o_ref[...] = compute(cur).astype(o_ref.dtype)

    parity = lax.rem(pl.program_id(0), 2)

    @pl.when(parity == 0)
    def _():
        run(bufs1, bufs0)

    @pl.when(parity == 1)
    def _():
        run(bufs0, bufs1)


def _chunk_spec(kc, bn, n_tiles, j0=0):
    return pl.BlockSpec((kc, bn), lambda j, i: (i, j0 + jnp.minimum(j, n_tiles - 1)))


def _prev_tile(j):
    return jnp.maximum(j - 1, 0)


def _row_tile(j, i):
    return jnp.where(j > 0, i, 0)


def _matmul_kernel(a_ref, chunk_ref, *rest, scaled):
    s_ref = rest[0] if scaled else None
    o_ref, wb0_ref, wb1_ref = rest[-3:]

    def compute(w):
        acc = jnp.dot(a_ref[...], w[0][...], preferred_element_type=F32)
        return acc * s_ref[...] if scaled else acc

    _stream_weights((chunk_ref,), (wb0_ref,), (wb1_ref,), o_ref, compute)


def _matmul(a, b, name, col_scale=None, bm=1024, bn=1024):
    m, k = a.shape
    _, n = b.shape
    n_tiles, row_tiles = n // bn, m // bm
    in_specs = [pl.BlockSpec((bm, k), lambda j, i: (_row_tile(j, i), 0)),
                _chunk_spec(k // row_tiles, bn, n_tiles)]
    args = (a, b)
    if col_scale is not None:
        in_specs.append(pl.BlockSpec((1, bn), lambda j, i: (0, _prev_tile(j))))
        args += (col_scale,)
    return pl.pallas_call(
        functools.partial(_matmul_kernel, scaled=col_scale is not None),
        grid=(n_tiles + 1, row_tiles),
        in_specs=in_specs,
        out_specs=pl.BlockSpec((bm, bn), lambda j, i: (_row_tile(j, i), _prev_tile(j))),
        out_shape=jax.ShapeDtypeStruct((m, n), BF16),
        scratch_shapes=[pltpu.VMEM((k, bn), BF16), pltpu.VMEM((k, bn), BF16)],
        compiler_params=_params(("arbitrary", "arbitrary")),
        name=name,
    )(*args)


def _lru_kernel(xa_ref, za_ref, cw_ref, cb_ref, wa_ref, wi_ref, ba_ref, bi_ref,
                lam_ref, o_ref, xe_ref, hc_ref, *, bm, n_blocks):
    @pl.when(pl.program_id(0) == 0)
    def _():
        xe_ref[0:SUBLANES, :] = jnp.zeros((SUBLANES, xe_ref.shape[1]), F32)
        hc_ref[...] = jnp.zeros(hc_ref.shape, F32)

    groups = bm // SUBLANES
    sub = lax.broadcasted_iota(jnp.int32, (groups, SUBLANES, LRU_BLOCK_W), 1)

    def block(n, carry):
        sl = pl.ds(pl.multiple_of(n * LRU_BLOCK_W, LRU_BLOCK_W), LRU_BLOCK_W)
        xa = xa_ref[:, sl].astype(F32)
        xe_ref[SUBLANES:, sl] = xa
        xc = cb_ref[:, sl] + cw_ref[CONV_W - 1:CONV_W, sl] * xa
        for k in range(CONV_W - 1):
            shift = CONV_W - 1 - k
            xc = xc + cw_ref[k:k + 1, sl] * xe_ref[pl.ds(SUBLANES - shift, bm), sl]
        xe_ref[0:SUBLANES, sl] = xa[bm - SUBLANES:, :]

        xcb = xc.astype(BF16)
        r = _sigmoid(jnp.dot(xcb, wa_ref[n], preferred_element_type=F32) + ba_ref[:, sl])
        i = _sigmoid(jnp.dot(xcb, wi_ref[n], preferred_element_type=F32) + bi_ref[:, sl])
        lam = lam_ref[:, sl]
        softplus_neg_lam = jnp.maximum(-lam, 0.0) + jnp.log1p(jnp.exp(-jnp.abs(lam)))
        log_a = (-LRU_C * r) * softplus_neg_lam
        a = jnp.exp(log_a)
        y = -jnp.tanh(log_a) * (a * a + 1.0)
        u = jnp.where(y > 0.0, y * lax.rsqrt(y), 0.0) * (i * xc)

        a = a.reshape(groups, SUBLANES, LRU_BLOCK_W)
        u = u.reshape(groups, SUBLANES, LRU_BLOCK_W)
        d = 1
        while d < SUBLANES:
            keep = sub >= d
            a_prev = jnp.where(keep, pltpu.roll(a, d, 1), 1.0)
            u_prev = jnp.where(keep, pltpu.roll(u, d, 1), 0.0)
            u = u + a * u_prev
            a = a * a_prev
            d *= 2
        h_prev = hc_ref[0:1, sl]
        hs = []
        for g in range(groups):
            hg = a[g] * h_prev + u[g]
            hs.append(hg)
            h_prev = hg[SUBLANES - 1:SUBLANES, :]
        hc_ref[0:1, sl] = h_prev
        h = jnp.concatenate(hs, axis=0)

        z = za_ref[:, sl].astype(F32)
        o_ref[:, sl] = (h * (z * _sigmoid(z))).astype(o_ref.dtype)
        return carry

    lax.fori_loop(0, n_blocks, block, 0)


def _lru_branch(proj, conv_w, conv_b, w_a, w_i, b_a, b_i, lam, width, bm=256):
    s = proj.shape[0]
    n_blocks = width // LRU_BLOCK_W
    vec = lambda rows: pl.BlockSpec((rows, width), lambda i: (0, 0))
    wspec = pl.BlockSpec((n_blocks, LRU_BLOCK_W, LRU_BLOCK_W), lambda i: (0, 0, 0))
    return pl.pallas_call(
        functools.partial(_lru_kernel, bm=bm, n_blocks=n_blocks),
        grid=(s // bm,),
        in_specs=[pl.BlockSpec((bm, width), lambda i: (i, 0)),
                  pl.BlockSpec((bm, width), lambda i: (i, 1)),
                  vec(CONV_W), vec(1), wspec, wspec, vec(1), vec(1), vec(1)],
        out_specs=pl.BlockSpec((bm, width), lambda i: (i, 0)),
        out_shape=jax.ShapeDtypeStruct((s, width), BF16),
        scratch_shapes=[pltpu.VMEM((bm + SUBLANES, width), F32),
                        pltpu.VMEM((SUBLANES, width), F32)],
        compiler_params=_params(("arbitrary",)),
        name="rglru_branch",
    )(proj, proj, conv_w, conv_b, w_a, w_i, b_a, b_i, lam)


def _attn_kernel(q_ref, k0_ref, k1_ref, k2_ref, v0_ref, v1_ref, v2_ref, zb_ref, w_ref,
                 o_ref, bias_ref, s_ref, p_ref, l_ref, *, n_heads):
    bq = q_ref.shape[0]
    nk = ATT_KEY_BLOCKS * bq
    step = pl.program_id(0)

    @pl.when(step < ATT_KEY_BLOCKS)
    def _():
        qi = lax.broadcasted_iota(jnp.int32, (bq, nk), 0)
        kj = lax.broadcasted_iota(jnp.int32, (bq, nk), 1)
        q_chunk = lax.shift_right_logical(qi, CHUNK.bit_length() - 1)
        k_chunk = lax.shift_right_logical(kj, CHUNK.bit_length() - 1)
        visible = ((k_chunk >= q_chunk) & (k_chunk <= q_chunk + CTX_CHUNKS)
                   & (kj + step * bq >= (ATT_KEY_BLOCKS - 1) * bq))

        def build(h, carry):
            rows = jnp.broadcast_to(w_ref[h], (bq, w_ref.shape[2]))
            toeplitz = pltpu.roll(rows, 0, 1, stride=1, stride_axis=0)
            bias_ref[h] = jnp.where(visible, toeplitz[:, :nk], NEG_INF)
            return carry

        lax.fori_loop(0, n_heads, build, 0)

    def head_cols(h):
        return pl.ds(pl.multiple_of(h * ATT_HEAD_DIM, ATT_HEAD_DIM), ATT_HEAD_DIM)

    def scores(h, carry):
        hs = head_cols(h)
        kh = jnp.concatenate([k0_ref[:, hs], k1_ref[:, hs], k2_ref[:, hs]], axis=0)
        s = lax.dot_general(q_ref[:, hs], kh, (((1,), (1,)), ((), ())),
                            preferred_element_type=F32)
        s_ref[h] = s + bias_ref[h]
        return carry

    def numerators(h, carry):
        s = s_ref[h]
        m = jnp.max(s, axis=-1, keepdims=True)
        p = jnp.exp2(s - m)
        l_ref[h] = jnp.broadcast_to(jnp.sum(p, axis=-1, keepdims=True), l_ref.shape[1:])
        p_ref[h] = p.astype(BF16)
        return carry

    def values(h, carry):
        hs = head_cols(h)
        vh = jnp.concatenate([v0_ref[:, hs], v1_ref[:, hs], v2_ref[:, hs]], axis=0)
        o = jnp.dot(p_ref[h], vh, preferred_element_type=F32) / l_ref[h]
        z = zb_ref[:, hs].astype(F32)
        o_ref[:, hs] = (o * (z * _sigmoid(z))).astype(o_ref.dtype)
        return carry

    lax.fori_loop(0, n_heads, scores, 0, unroll=n_heads)
    lax.fori_loop(0, n_heads, numerators, 0, unroll=4)
    lax.fori_loop(0, n_heads, values, 0, unroll=n_heads)


def _bias_by_offset(rel_bias):
    n_heads = rel_bias.shape[0]
    bq = ATT_BLOCK_Q
    nk = ATT_KEY_BLOCKS * bq
    off = nk - bq
    length = nk + bq
    n_far = off - REL_CLIP + 1
    n_near = nk - n_far - (2 * REL_CLIP - 1)
    far = rel_bias[:, 2 * REL_CLIP:]
    near = rel_bias[:, :1]
    w = jnp.concatenate([
        jnp.broadcast_to(far, (n_heads, n_far)),
        jnp.flip(rel_bias[:, 1:2 * REL_CLIP], axis=1),
        jnp.broadcast_to(near, (n_heads, n_near + 1)),
        jnp.broadcast_to(far, (n_heads, bq - 1)),
    ], axis=1).astype(F32)
    assert w.shape[1] == length
    return (w * LOG2_E)[:, None, :]


def _attn_branch(proj, rel_bias, att_width, col0):
    s = proj.shape[0]
    n_heads = att_width // ATT_HEAD_DIM
    bq = ATT_BLOCK_Q
    nk = ATT_KEY_BLOCKS * bq
    w = _bias_by_offset(rel_bias)

    def kv_spec(col, back):
        return pl.BlockSpec((bq, att_width), lambda i: (jnp.maximum(i - back, 0), col))

    return pl.pallas_call(
        functools.partial(_attn_kernel, n_heads=n_heads),
        grid=(s // bq,),
        in_specs=[pl.BlockSpec((bq, att_width), lambda i: (i, col0)),
                  kv_spec(col0 + 1, 2), kv_spec(col0 + 1, 1), kv_spec(col0 + 1, 0),
                  kv_spec(col0 + 2, 2), kv_spec(col0 + 2, 1), kv_spec(col0 + 2, 0),
                  pl.BlockSpec((bq, att_width), lambda i: (i, col0 + 3)),
                  pl.BlockSpec(w.shape, lambda i: (0, 0, 0))],
        out_specs=pl.BlockSpec((bq, att_width), lambda i: (i, 0)),
        out_shape=jax.ShapeDtypeStruct((s, att_width), BF16),
        scratch_shapes=[pltpu.VMEM((n_heads, bq, nk), F32),
                        pltpu.VMEM((n_heads, bq, nk), F32),
                        pltpu.VMEM((n_heads, bq, nk), BF16),
                        pltpu.VMEM((n_heads, bq, ATT_HEAD_DIM), F32)],
        compiler_params=_params(("arbitrary",)),
        name="chunk_attention",
    )(proj, proj, proj, proj, proj, proj, proj, proj, w)


def _merge_kernel(ya_ref, yb_ref, ca_ref, cb_ref, ga_ref, gb_ref, o_ref,
                  wa0_ref, wb0_ref, wa1_ref, wb1_ref):
    def compute(w):
        pa = jnp.dot(ya_ref[...], w[0][...], preferred_element_type=F32)
        pb = jnp.dot(yb_ref[...], w[1][...], preferred_element_type=F32)
        ga = jax.nn.sigmoid(ga_ref[...].astype(F32))
        gb = jax.nn.sigmoid(gb_ref[...].astype(F32))
        return ga * pa + gb * pb

    _stream_weights((ca_ref, cb_ref), (wa0_ref, wb0_ref), (wa1_ref, wb1_ref), o_ref, compute)


def _merge(y_a, y_b, w_pa, w_pb, proj, gate_start, bm=512, bn=1024):
    m, ka = y_a.shape
    kb = y_b.shape[1]
    n = w_pa.shape[1]
    n_tiles, row_tiles = n // bn, m // bm
    gate_col0 = gate_start // bn
    return pl.pallas_call(
        _merge_kernel,
        grid=(n_tiles + 1, row_tiles),
        in_specs=[pl.BlockSpec((bm, ka), lambda j, i: (_row_tile(j, i), 0)),
                  pl.BlockSpec((bm, kb), lambda j, i: (_row_tile(j, i), 0)),
                  _chunk_spec(ka // row_tiles, bn, n_tiles),
                  _chunk_spec(kb // row_tiles, bn, n_tiles),
                  pl.BlockSpec((bm, bn), lambda j, i: (_row_tile(j, i), gate_col0 + _prev_tile(j))),
                  pl.BlockSpec((bm, bn),
                               lambda j, i: (_row_tile(j, i), gate_col0 + n_tiles + _prev_tile(j)))],
        out_specs=pl.BlockSpec((bm, bn), lambda j, i: (_row_tile(j, i), _prev_tile(j))),
        out_shape=jax.ShapeDtypeStruct((m, n), BF16),
        scratch_shapes=[pltpu.VMEM((ka, bn), BF16), pltpu.VMEM((kb, bn), BF16),
                        pltpu.VMEM((ka, bn), BF16), pltpu.VMEM((kb, bn), BF16)],
        compiler_params=_params(("arbitrary", "arbitrary")),
        name="branch_merge",
    )(y_a, y_b, w_pa, w_pb, proj, proj)


def _rstd(v):
    return lax.rsqrt(jnp.mean(v * v, axis=-1, keepdims=True) + EPS)


def _rowwise_kernel(x_ref, t_ref, p_ref, wple_ref, gpost_ref, gpre_ref, hn_ref, rt_ref, rp_ref):
    t = t_ref[...].astype(F32)
    rstd_t = _rstd(t)
    h = x_ref[...] + (t * rstd_t) * gpost_ref[...]
    hn_ref[...] = _rms_norm_f32(h, gpre_ref[...]).astype(hn_ref.dtype)
    pe = jnp.dot(p_ref[...].astype(BF16), wple_ref[...], preferred_element_type=F32)
    rt_ref[...] = jnp.broadcast_to(rstd_t, rt_ref.shape)
    rp_ref[...] = jnp.broadcast_to(_rstd(pe), rp_ref.shape)


def _rowwise(x, t, p, w_ple, g_post, g_ple_pre, bm=256):
    s, d = x.shape
    pd = p.shape[1]
    row = lambda w: pl.BlockSpec((bm, w), lambda i: (i, 0))
    vec = pl.BlockSpec((1, d), lambda i: (0, 0))
    return pl.pallas_call(
        _rowwise_kernel,
        grid=(s // bm,),
        in_specs=[row(d), row(d), row(pd), pl.BlockSpec((pd, d), lambda i: (0, 0)), vec, vec],
        out_specs=[row(d), row(LANES), row(LANES)],
        out_shape=[jax.ShapeDtypeStruct((s, d), BF16),
                   jax.ShapeDtypeStruct((s, LANES), F32),
                   jax.ShapeDtypeStruct((s, LANES), F32)],
        compiler_params=_params(("parallel",)),
        name="residual_norms",
    )(x, t, p, w_ple, g_post, g_ple_pre)


def _ple_kernel(hn_ref, chunk_ref, x_ref, t_ref, p_ref, wple_ref, rt_ref, rp_ref, gpost_ref, gple_ref,
                o_ref, wb0_ref, wb1_ref):
    def compute(w):
        g = jnp.dot(hn_ref[...], w[0][...], preferred_element_type=F32)
        h = x_ref[...] + (t_ref[...].astype(F32) * rt_ref[:, 0:1]) * gpost_ref[...]
        pe = jnp.dot(p_ref[...].astype(BF16), wple_ref[...], preferred_element_type=F32)
        pn = (pe * rp_ref[:, 0:1]) * gple_ref[...]
        return h + pn * jax.nn.sigmoid(g)

    _stream_weights((chunk_ref,), (wb0_ref,), (wb1_ref,), o_ref, compute)


def _ple_gate(hn, w, x, t, p, w_ple, rstd_t, rstd_p, g_post, g_ple_post, bm=512, bn=1024):
    m, k = hn.shape
    n = w.shape[1]
    pd = p.shape[1]
    n_tiles, row_tiles = n // bn, m // bm
    tile = pl.BlockSpec((bm, bn), lambda j, i: (_row_tile(j, i), _prev_tile(j)))
    rows = lambda width: pl.BlockSpec((bm, width), lambda j, i: (_row_tile(j, i), 0))
    cols = lambda height: pl.BlockSpec((height, bn), lambda j, i: (0, _prev_tile(j)))
    return pl.pallas_call(
        _ple_kernel,
        grid=(n_tiles + 1, row_tiles),
        in_specs=[rows(k), _chunk_spec(k // row_tiles, bn, n_tiles),
                  tile, tile, rows(pd), cols(pd), rows(LANES), rows(LANES), cols(1), cols(1)],
        out_specs=tile,
        out_shape=jax.ShapeDtypeStruct((m, n), F32),
        scratch_shapes=[pltpu.VMEM((k, bn), BF16), pltpu.VMEM((k, bn), BF16)],
        compiler_params=_params(("arbitrary", "arbitrary")),
        name="ple_gate",
    )(hn, w, x, t, p, w_ple, rstd_t, rstd_p, g_post, g_ple_post)


def _layer(x, p_i, w_in, conv_w, conv_b, w_rg_a, b_rg_a, w_rg_i, b_rg_i, lru_lambda,
           rel_bias, w_proj_a, w_proj_b, w_out, g_pre, g_post,
           w_ple, w_ple_gate, g_ple_pre, g_ple_post):
    d = x.shape[1]
    lru_width = w_proj_a.shape[0]
    att_width = w_proj_b.shape[0]
    assert lru_width == d and 2 * att_width == d
    row = lambda v: v.reshape(1, -1)

    xn = _rmsnorm(x, row(g_pre))
    q_start = 2 * lru_width
    col_scale = jnp.ones((1, w_in.shape[1]), F32).at[:, q_start:q_start + att_width].set(
        ATT_HEAD_DIM ** -0.5 * LOG2_E)
    proj = _matmul(xn, w_in, "in_proj", col_scale=col_scale)
    y_a = _lru_branch(proj, conv_w, row(conv_b), w_rg_a.astype(BF16), w_rg_i.astype(BF16),
                      row(b_rg_a), row(b_rg_i), row(lru_lambda), lru_width)
    y_b = _attn_branch(proj, rel_bias, att_width, col0=2 * lru_width // att_width)
    merged = _merge(y_a, y_b, w_proj_a, w_proj_b, proj, gate_start=2 * lru_width + 4 * att_width)
    t = _matmul(merged, w_out, "out_proj")
    w_ple_b = w_ple.astype(BF16)
    hn, rstd_t, rstd_p = _rowwise(x, t, p_i, w_ple_b, row(g_post), row(g_ple_pre))
    return _ple_gate(hn, w_ple_gate, x, t, p_i, w_ple_b, rstd_t, rstd_p, row(g_post), row(g_ple_post))


def kernel(x, p, w_in, conv_w, conv_b, w_rg_a, b_rg_a, w_rg_i, b_rg_i, lru_lambda, rel_bias,
           w_proj_a, w_proj_b, w_out, g_pre, g_post, w_ple, w_ple_gate, g_ple_pre, g_ple_post):
    batch = x.shape[0]
    outs = []
    for b in range(batch):
        h = x[b]
        for l in range(w_in.shape[0]):
            h = _layer(h, p[l, b], w_in[l], conv_w[l], conv_b[l], w_rg_a[l], b_rg_a[l],
                       w_rg_i[l], b_rg_i[l], lru_lambda[l], rel_bias[l], w_proj_a[l],
                       w_proj_b[l], w_out[l], g_pre[l], g_post[l], w_ple[l],
                       w_ple_gate[l], g_ple_pre[l], g_ple_post[l])
        outs.append(h)
    return jnp.stack(outs, axis=0)
```

```python
import functools
import math

import jax
import jax.numpy as jnp
from jax import lax
from jax.experimental import pallas as pl
from jax.experimental.pallas import tpu as pltpu

F32 = jnp.float32
BF16 = jnp.bfloat16

EPS = 1e-6
NEG_INF = -1e30
LRU_C = 8.0
LOG2_E = math.log2(math.e)

CHUNK = 64
CTX_CHUNKS = 8
REL_CLIP = 128
ATT_HEAD_DIM = 128
LRU_BLOCK_W = 256
CONV_W = 4

SUBLANES = 8
LANES = 128
ATT_BLOCK_Q = 256
ATT_KEY_BLOCKS = 1 + (CTX_CHUNKS * CHUNK) // ATT_BLOCK_Q
VMEM_LIMIT_BYTES = 56 * 1024 * 1024


def _params(semantics):
    return pltpu.CompilerParams(dimension_semantics=semantics,
                                vmem_limit_bytes=VMEM_LIMIT_BYTES)


def _sigmoid(x):
    return 0.5 * jnp.tanh(0.5 * x) + 0.5


def _rms_norm_f32(x, g):
    ms = jnp.mean(x * x, axis=-1, keepdims=True)
    return (x * lax.rsqrt(ms + EPS)) * g


def _rmsnorm_kernel(x_ref, g_ref, o_ref):
    o_ref[...] = _rms_norm_f32(x_ref[...], g_ref[...]).astype(o_ref.dtype)


def _rmsnorm(x, g, bm=256):
    s, d = x.shape
    return pl.pallas_call(
        _rmsnorm_kernel,
        grid=(s // bm,),
        in_specs=[pl.BlockSpec((bm, d), lambda i: (i, 0)),
                  pl.BlockSpec((1, d), lambda i: (0, 0))],
        out_specs=pl.BlockSpec((bm, d), lambda i: (i, 0)),
        out_shape=jax.ShapeDtypeStruct((s, d), BF16),
        compiler_params=_params(("parallel",)),
        name="rmsnorm_pre",
    )(x, g)


def _cast_chunk(chunk_ref, w_next_ref):
    kc = chunk_ref.shape[0]
    rows = pl.ds(pl.multiple_of(pl.program_id(1) * kc, kc), kc)
    w_next_ref[rows, :] = chunk_ref[...].astype(BF16)


def _stream_weights(chunk_refs, bufs0, bufs1, o_ref, compute):
    def run(cur, nxt):
        def cast():
            for chunk_ref, w_next_ref in zip(chunk_refs, nxt):
                _cast_chunk(chunk_ref, w_next_ref)

        @pl.when(pl.program_id(0) == 0)
        def _():
            cast()

        @pl.when(pl.program_id(0) > 0)
        def _():
            cast()
            o_ref[...] = compute(cur).astype(o_ref.dtype)

    parity = lax.rem(pl.program_id(0), 2)

    @pl.when(parity == 0)
    def _():
        run(bufs1, bufs0)

    @pl.when(parity == 1)
    def _():
        run(bufs0, bufs1)


def _chunk_spec(kc, bn, n_tiles, j0=0):
    return pl.BlockSpec((kc, bn), lambda j, i: (i, j0 + jnp.minimum(j, n_tiles - 1)))


def _prev_tile(j):
    return jnp.maximum(j - 1, 0)


def _row_tile(j, i):
    return jnp.where(j > 0, i, 0)


def _matmul_kernel(a_ref, chunk_ref, *rest, scaled):
    s_ref = rest[0] if scaled else None
    o_ref, wb0_ref, wb1_ref = rest[-3:]

    def compute(w):
        acc = jnp.dot(a_ref[...], w[0][...], preferred_element_type=F32)
        return acc * s_ref[...] if scaled else acc

    _stream_weights((chunk_ref,), (wb0_ref,), (wb1_ref,), o_ref, compute)


def _matmul(a, b, name, col_scale=None, bm=1024, bn=1024):
    m, k = a.shape
    _, n = b.shape
    n_tiles, row_tiles = n // bn, m // bm
    in_specs = [pl.BlockSpec((bm, k), lambda j, i: (_row_tile(j, i), 0)),
                _chunk_spec(k // row_tiles, bn, n_tiles)]
    args = (a, b)
    if col_scale is not None:
        in_specs.append(pl.BlockSpec((1, bn), lambda j, i: (0, _prev_tile(j))))
        args += (col_scale,)
    return pl.pallas_call(
        functools.partial(_matmul_kernel, scaled=col_scale is not None),
        grid=(n_tiles + 1, row_tiles),
        in_specs=in_specs,
        out_specs=pl.BlockSpec((bm, bn), lambda j, i: (_row_tile(j, i), _prev_tile(j))),
        out_shape=jax.ShapeDtypeStruct((m, n), BF16),
        scratch_shapes=[pltpu.VMEM((k, bn), BF16), pltpu.VMEM((k, bn), BF16)],
        compiler_params=_params(("arbitrary", "arbitrary")),
        name=name,
    )(*args)


def _lru_kernel(xa_ref, za_ref, cw_ref, cb_ref, wa_ref, wi_ref, ba_ref, bi_ref,
                lam_ref, o_ref, xe_ref, hc_ref, *, bm, n_blocks):
    @pl.when(pl.program_id(0) == 0)
    def _():
        xe_ref[0:SUBLANES, :] = jnp.zeros((SUBLANES, xe_ref.shape[1]), F32)
        hc_ref[...] = jnp.zeros(hc_ref.shape, F32)

    groups = bm // SUBLANES
    sub = lax.broadcasted_iota(jnp.int32, (groups, SUBLANES, LRU_BLOCK_W), 1)

    def block(n, carry):
        sl = pl.ds(pl.multiple_of(n * LRU_BLOCK_W, LRU_BLOCK_W), LRU_BLOCK_W)
        xa = xa_ref[:, sl].astype(F32)
        xe_ref[SUBLANES:, sl] = xa
        xc = cb_ref[:, sl] + cw_ref[CONV_W - 1:CONV_W, sl] * xa
        for k in range(CONV_W - 1):
            shift = CONV_W - 1 - k
            xc = xc + cw_ref[k:k + 1, sl] * xe_ref[pl.ds(SUBLANES - shift, bm), sl]
        xe_ref[0:SUBLANES, sl] = xa[bm - SUBLANES:, :]

        xcb = xc.astype(BF16)
        r = _sigmoid(jnp.dot(xcb, wa_ref[n], preferred_element_type=F32) + ba_ref[:, sl])
        i = _sigmoid(jnp.dot(xcb, wi_ref[n], preferred_element_type=F32) + bi_ref[:, sl])
        lam = lam_ref[:, sl]
        softplus_neg_lam = jnp.maximum(-lam, 0.0) + jnp.log1p(jnp.exp(-jnp.abs(lam)))
        log_a = (-LRU_C * r) * softplus_neg_lam
        a = jnp.exp(log_a)
        y = -jnp.tanh(log_a) * (a * a + 1.0)
        u = jnp.where(y > 0.0, y * lax.rsqrt(y), 0.0) * (i * xc)

        a = a.reshape(groups, SUBLANES, LRU_BLOCK_W)
        u = u.reshape(groups, SUBLANES, LRU_BLOCK_W)
        d = 1
        while d < SUBLANES:
            keep = sub >= d
            a_prev = jnp.where(keep, pltpu.roll(a, d, 1), 1.0)
            u_prev = jnp.where(keep, pltpu.roll(u, d, 1), 0.0)
            u = u + a * u_prev
            a = a * a_prev
            d *= 2
        h_prev = hc_ref[0:1, sl]
        hs = []
        for g in range(groups):
            hg = a[g] * h_prev + u[g]
            hs.append(hg)
            h_prev = hg[SUBLANES - 1:SUBLANES, :]
        hc_ref[0:1, sl] = h_prev
        h = jnp.concatenate(hs, axis=0)

        z = za_ref[:, sl].astype(F32)
        o_ref[:, sl] = (h * (z * _sigmoid(z))).astype(o_ref.dtype)
        return carry

    lax.fori_loop(0, n_blocks, block, 0)


def _lru_branch(proj, conv_w, conv_b, w_a, w_i, b_a, b_i, lam, width, bm=256):
    s = proj.shape[0]
    n_blocks = width // LRU_BLOCK_W
    vec = lambda rows: pl.BlockSpec((rows, width), lambda i: (0, 0))
    wspec = pl.BlockSpec((n_blocks, LRU_BLOCK_W, LRU_BLOCK_W), lambda i: (0, 0, 0))
    return pl.pallas_call(
        functools.partial(_lru_kernel, bm=bm, n_blocks=n_blocks),
        grid=(s // bm,),
        in_specs=[pl.BlockSpec((bm, width), lambda i: (i, 0)),
                  pl.BlockSpec((bm, width), lambda i: (i, 1)),
                  vec(CONV_W), vec(1), wspec, wspec, vec(1), vec(1), vec(1)],
        out_specs=pl.BlockSpec((bm, width), lambda i: (i, 0)),
        out_shape=jax.ShapeDtypeStruct((s, width), BF16),
        scratch_shapes=[pltpu.VMEM((bm + SUBLANES, width), F32),
                        pltpu.VMEM((SUBLANES, width), F32)],
        compiler_params=_params(("arbitrary",)),
        name="rglru_branch",
    )(proj, proj, conv_w, conv_b, w_a, w_i, b_a, b_i, lam)


def _attn_kernel(q_ref, k0_ref, k1_ref, k2_ref, v0_ref, v1_ref, v2_ref, zb_ref, w_ref,
                 o_ref, bias_ref, s_ref, p_ref, l_ref, *, n_heads):
    bq = q_ref.shape[0]
    nk = ATT_KEY_BLOCKS * bq
    step = pl.program_id(0)

    @pl.when(step < ATT_KEY_BLOCKS)
    def _():
        qi = lax.broadcasted_iota(jnp.int32, (bq, nk), 0)
        kj = lax.broadcasted_iota(jnp.int32, (bq, nk), 1)
        q_chunk = lax.shift_right_logical(qi, CHUNK.bit_length() - 1)
        k_chunk = lax.shift_right_logical(kj, CHUNK.bit_length() - 1)
        visible = ((k_chunk >= q_chunk) & (k_chunk <= q_chunk + CTX_CHUNKS)
                   & (kj + step * bq >= (ATT_KEY_BLOCKS - 1) * bq))

        def build(h, carry):
            rows = jnp.broadcast_to(w_ref[h], (bq, w_ref.shape[2]))
            toeplitz = pltpu.roll(rows, 0, 1, stride=1, stride_axis=0)
            bias_ref[h] = jnp.where(visible, toeplitz[:, :nk], NEG_INF)
            return carry

        lax.fori_loop(0, n_heads, build, 0)

    def head_cols(h):
        return pl.ds(pl.multiple_of(h * ATT_HEAD_DIM, ATT_HEAD_DIM), ATT_HEAD_DIM)

    def scores(h, carry):
        hs = head_cols(h)
        kh = jnp.concatenate([k0_ref[:, hs], k1_ref[:, hs], k2_ref[:, hs]], axis=0)
        s = lax.dot_general(q_ref[:, hs], kh, (((1,), (1,)), ((), ())),
                            preferred_element_type=F32)
        s_ref[h] = s + bias_ref[h]
        return carry

    def numerators(h, carry):
        s = s_ref[h]
        m = jnp.max(s, axis=-1, keepdims=True)
        p = jnp.exp2(s - m)
        l_ref[h] = jnp.broadcast_to(jnp.sum(p, axis=-1, keepdims=True), l_ref.shape[1:])
        p_ref[h] = p.astype(BF16)
        return carry

    def values(h, carry):
        hs = head_cols(h)
        vh = jnp.concatenate([v0_ref[:, hs], v1_ref[:, hs], v2_ref[:, hs]], axis=0)
        o = jnp.dot(p_ref[h], vh, preferred_element_type=F32) / l_ref[h]
        z = zb_ref[:, hs].astype(F32)
        o_ref[:, hs] = (o * (z * _sigmoid(z))).astype(o_ref.dtype)
        return carry

    lax.fori_loop(0, n_heads, scores, 0, unroll=n_heads)
    lax.fori_loop(0, n_heads, numerators, 0, unroll=4)
    lax.fori_loop(0, n_heads, values, 0, unroll=n_heads)


def _bias_by_offset(rel_bias):
    n_heads = rel_bias.shape[0]
    bq = ATT_BLOCK_Q
    nk = ATT_KEY_BLOCKS * bq
    off = nk - bq
    length = nk + bq
    n_far = off - REL_CLIP + 1
    n_near = nk - n_far - (2 * REL_CLIP - 1)
    far = rel_bias[:, 2 * REL_CLIP:]
    near = rel_bias[:, :1]
    w = jnp.concatenate([
        jnp.broadcast_to(far, (n_heads, n_far)),
        jnp.flip(rel_bias[:, 1:2 * REL_CLIP], axis=1),
        jnp.broadcast_to(near, (n_heads, n_near + 1)),
        jnp.broadcast_to(far, (n_heads, bq - 1)),
    ], axis=1).astype(F32)
    assert w.shape[1] == length
    return (w * LOG2_E)[:, None, :]


def _attn_branch(proj, rel_bias, att_width, col0):
    s = proj.shape[0]
    n_heads = att_width // ATT_HEAD_DIM
    bq = ATT_BLOCK_Q
    nk = ATT_KEY_BLOCKS * bq
    w = _bias_by_offset(rel_bias)

    def kv_spec(col, back):
        return pl.BlockSpec((bq, att_width), lambda i: (jnp.maximum(i - back, 0), col))

    return pl.pallas_call(
        functools.partial(_attn_kernel, n_heads=n_heads),
        grid=(s // bq,),
        in_specs=[pl.BlockSpec((bq, att_width), lambda i: (i, col0)),
                  kv_spec(col0 + 1, 2), kv_spec(col0 + 1, 1), kv_spec(col0 + 1, 0),
                  kv_spec(col0 + 2, 2), kv_spec(col0 + 2, 1), kv_spec(col0 + 2, 0),
                  pl.BlockSpec((bq, att_width), lambda i: (i, col0 + 3)),
                  pl.BlockSpec(w.shape, lambda i: (0, 0, 0))],
        out_specs=pl.BlockSpec((bq, att_width), lambda i: (i, 0)),
        out_shape=jax.ShapeDtypeStruct((s, att_width), BF16),
        scratch_shapes=[pltpu.VMEM((n_heads, bq, nk), F32),
                        pltpu.VMEM((n_heads, bq, nk), F32),
                        pltpu.VMEM((n_heads, bq, nk), BF16),
                        pltpu.VMEM((n_heads, bq, ATT_HEAD_DIM), F32)],
        compiler_params=_params(("arbitrary",)),
        name="chunk_attention",
    )(proj, proj, proj, proj, proj, proj, proj, proj, w)


def _merge_kernel(ya_ref, yb_ref, ca_ref, cb_ref, ga_ref, gb_ref, o_ref,
                  wa0_ref, wb0_ref, wa1_ref, wb1_ref):
    def compute(w):
        pa = jnp.dot(ya_ref[...], w[0][...], preferred_element_type=F32)
        pb = jnp.dot(yb_ref[...], w[1][...], preferred_element_type=F32)
        ga = jax.nn.sigmoid(ga_ref[...].astype(F32))
        gb = jax.nn.sigmoid(gb_ref[...].astype(F32))
        return ga * pa + gb * pb

    _stream_weights((ca_ref, cb_ref), (wa0_ref, wb0_ref), (wa1_ref, wb1_ref), o_ref, compute)


def _merge(y_a, y_b, w_pa, w_pb, proj, gate_start, bm=512, bn=1024):
    m, ka = y_a.shape
    kb = y_b.shape[1]
    n = w_pa.shape[1]
    n_tiles, row_tiles = n // bn, m // bm
    gate_col0 = gate_start // bn
    return pl.pallas_call(
        _merge_kernel,
        grid=(n_tiles + 1, row_tiles),
        in_specs=[pl.BlockSpec((bm, ka), lambda j, i: (_row_tile(j, i), 0)),
                  pl.BlockSpec((bm, kb), lambda j, i: (_row_tile(j, i), 0)),
                  _chunk_spec(ka // row_tiles, bn, n_tiles),
                  _chunk_spec(kb // row_tiles, bn, n_tiles),
                  pl.BlockSpec((bm, bn), lambda j, i: (_row_tile(j, i), gate_col0 + _prev_tile(j))),
                  pl.BlockSpec((bm, bn),
                               lambda j, i: (_row_tile(j, i), gate_col0 + n_tiles + _prev_tile(j)))],
        out_specs=pl.BlockSpec((bm, bn), lambda j, i: (_row_tile(j, i), _prev_tile(j))),
        out_shape=jax.ShapeDtypeStruct((m, n), BF16),
        scratch_shapes=[pltpu.VMEM((ka, bn), BF16), pltpu.VMEM((kb, bn), BF16),
                        pltpu.VMEM((ka, bn), BF16), pltpu.VMEM((kb, bn), BF16)],
        compiler_params=_params(("arbitrary", "arbitrary")),
        name="branch_merge",
    )(y_a, y_b, w_pa, w_pb, proj, proj)


def _rstd(v):
    return lax.rsqrt(jnp.mean(v * v, axis=-1, keepdims=True) + EPS)


def _rowwise_kernel(x_ref, t_ref, p_ref, wple_ref, gpost_ref, gpre_ref, gple_ref,
                    hn_ref, pn_ref, rt_ref):
    t = t_ref[...].astype(F32)
    rstd_t = _rstd(t)
    h = x_ref[...] + (t * rstd_t) * gpost_ref[...]
    hn_ref[...] = _rms_norm_f32(h, gpre_ref[...]).astype(hn_ref.dtype)
    pe = jnp.dot(p_ref[...].astype(BF16), wple_ref[...], preferred_element_type=F32)
    pn_ref[...] = _rms_norm_f32(pe, gple_ref[...]).astype(pn_ref.dtype)
    rt_ref[...] = jnp.broadcast_to(rstd_t, rt_ref.shape)


def _rowwise(x, t, p, w_ple, g_post, g_ple_pre, g_ple_post, bm=256):
    s, d = x.shape
    pd = p.shape[1]
    row = lambda w: pl.BlockSpec((bm, w), lambda i: (i, 0))
    vec = pl.BlockSpec((1, d), lambda i: (0, 0))
    return pl.pallas_call(
        _rowwise_kernel,
        grid=(s // bm,),
        in_specs=[row(d), row(d), row(pd), pl.BlockSpec((pd, d), lambda i: (0, 0)), vec, vec, vec],
        out_specs=[row(d), row(d), row(LANES)],
        out_shape=[jax.ShapeDtypeStruct((s, d), BF16),
                   jax.ShapeDtypeStruct((s, d), BF16),
                   jax.ShapeDtypeStruct((s, LANES), F32)],
        compiler_params=_params(("parallel",)),
        name="residual_norms",
    )(x, t, p, w_ple, g_post, g_ple_pre, g_ple_post)


def _ple_kernel(hn_ref, chunk_ref, x_ref, t_ref, pn_ref, rt_ref, gpost_ref, o_ref, wb0_ref, wb1_ref):
    def compute(w):
        g = jnp.dot(hn_ref[...], w[0][...], preferred_element_type=F32)
        h = x_ref[...] + (t_ref[...].astype(F32) * rt_ref[:, 0:1]) * gpost_ref[...]
        return h + pn_ref[...].astype(F32) * jax.nn.sigmoid(g)

    _stream_weights((chunk_ref,), (wb0_ref,), (wb1_ref,), o_ref, compute)


def _ple_gate(hn, w, x, t, pn, rstd_t, g_post, bm=512, bn=1024):
    m, k = hn.shape
    n = w.shape[1]
    n_tiles, row_tiles = n // bn, m // bm
    tile = pl.BlockSpec((bm, bn), lambda j, i: (_row_tile(j, i), _prev_tile(j)))
    rows = lambda width: pl.BlockSpec((bm, width), lambda j, i: (_row_tile(j, i), 0))
    return pl.pallas_call(
        _ple_kernel,
        grid=(n_tiles + 1, row_tiles),
        in_specs=[rows(k), _chunk_spec(k // row_tiles, bn, n_tiles),
                  tile, tile, tile, rows(LANES),
                  pl.BlockSpec((1, bn), lambda j, i: (0, _prev_tile(j)))],
        out_specs=tile,
        out_shape=jax.ShapeDtypeStruct((m, n), F32),
        scratch_shapes=[pltpu.VMEM((k, bn), BF16), pltpu.VMEM((k, bn), BF16)],
        compiler_params=_params(("arbitrary", "arbitrary")),
        name="ple_gate",
    )(hn, w, x, t, pn, rstd_t, g_post)


def _layer(x, p_i, w_in, conv_w, conv_b, w_rg_a, b_rg_a, w_rg_i, b_rg_i, lru_lambda,
           rel_bias, w_proj_a, w_proj_b, w_out, g_pre, g_post,
           w_ple, w_ple_gate, g_ple_pre, g_ple_post):
    d = x.shape[1]
    lru_width = w_proj_a.shape[0]
    att_width = w_proj_b.shape[0]
    assert lru_width == d and 2 * att_width == d
    row = lambda v: v.reshape(1, -1)

    xn = _rmsnorm(x, row(g_pre))
    q_start = 2 * lru_width
    col_scale = jnp.ones((1, w_in.shape[1]), F32).at[:, q_start:q_start + att_width].set(
        ATT_HEAD_DIM ** -0.5 * LOG2_E)
    proj = _matmul(xn, w_in, "in_proj", col_scale=col_scale)
    y_a = _lru_branch(proj, conv_w, row(conv_b), w_rg_a.astype(BF16), w_rg_i.astype(BF16),
                      row(b_rg_a), row(b_rg_i), row(lru_lambda), lru_width)
    y_b = _attn_branch(proj, rel_bias, att_width, col0=2 * lru_width // att_width)
    merged = _merge(y_a, y_b, w_proj_a, w_proj_b, proj, gate_start=2 * lru_width + 4 * att_width)
    t = _matmul(merged, w_out, "out_proj")
    hn, pn, rstd_t = _rowwise(x, t, p_i, w_ple.astype(BF16), row(g_post), row(g_ple_pre), row(g_ple_post))
    return _ple_gate(hn, w_ple_gate, x, t, pn, rstd_t, row(g_post))


def kernel(x, p, w_in, conv_w, conv_b, w_rg_a, b_rg_a, w_rg_i, b_rg_i, lru_lambda, rel_bias,
           w_proj_a, w_proj_b, w_out, g_pre, g_post, w_ple, w_ple_gate, g_ple_pre, g_ple_post):
    batch = x.shape[0]
    outs = []
    for b in range(batch):
        h = x[b]
        for l in range(w_in.shape[0]):
            h = _layer(h, p[l, b], w_in[l], conv_w[l], conv_b[l], w_rg_a[l], b_rg_a[l],
                       w_rg_i[l], b_rg_i[l], lru_lambda[l], rel_bias[l], w_proj_a[l],
                       w_proj_b[l], w_out[l], g_pre[l], g_post[l], w_ple[l],
                       w_ple_gate[l], g_ple_pre[l], g_ple_post[l])
        outs.append(h)
    return jnp.stack(outs, axis=0)
```

```python
import functools
import math

import jax
import jax.numpy as jnp
from jax import lax
from jax.experimental import pallas as pl
from jax.experimental.pallas import tpu as pltpu

F32 = jnp.float32
BF16 = jnp.bfloat16

EPS = 1e-6
NEG_INF = -1e30
LRU_C = 8.0
LOG2_E = math.log2(math.e)

CHUNK = 64
CTX_CHUNKS = 8
REL_CLIP = 128
ATT_HEAD_DIM = 128
LRU_BLOCK_W = 256
CONV_W = 4

SUBLANES = 8
LANES = 128
ATT_BLOCK_Q = 256
ATT_KEY_BLOCKS = 1 + (CTX_CHUNKS * CHUNK) // ATT_BLOCK_Q
VMEM_LIMIT_BYTES = 56 * 1024 * 1024


def _params(semantics):
    return pltpu.CompilerParams(dimension_semantics=semantics,
                                vmem_limit_bytes=VMEM_LIMIT_BYTES)


def _sigmoid(x):
    return 0.5 * jnp.tanh(0.5 * x) + 0.5


def _rms_norm_f32(x, g):
    ms = jnp.mean(x * x, axis=-1, keepdims=True)
    return (x * lax.rsqrt(ms + EPS)) * g


def _rmsnorm_kernel(x_ref, g_ref, o_ref):
    o_ref[...] = _rms_norm_f32(x_ref[...], g_ref[...]).astype(o_ref.dtype)


def _rmsnorm(x, g, bm=256):
    s, d = x.shape
    return pl.pallas_call(
        _rmsnorm_kernel,
        grid=(s // bm,),
        in_specs=[pl.BlockSpec((bm, d), lambda i: (i, 0)),
                  pl.BlockSpec((1, d), lambda i: (0, 0))],
        out_specs=pl.BlockSpec((bm, d), lambda i: (i, 0)),
        out_shape=jax.ShapeDtypeStruct((s, d), BF16),
        compiler_params=_params(("parallel",)),
        name="rmsnorm_pre",
    )(x, g)


def _cast_chunk(chunk_ref, w_next_ref):
    kc = chunk_ref.shape[0]
    rows = pl.ds(pl.multiple_of(pl.program_id(1) * kc, kc), kc)
    w_next_ref[rows, :] = chunk_ref[...].astype(BF16)


def _stream_weights(chunk_refs, bufs0, bufs1, o_ref, compute):
    def run(cur, nxt):
        def cast():
            for chunk_ref, w_next_ref in zip(chunk_refs, nxt):
                _cast_chunk(chunk_ref, w_next_ref)

        @pl.when(pl.program_id(0) == 0)
        def _():
            cast()

        @pl.when(pl.program_id(0) > 0)
        def _():
            cast()
            o_ref[...] = compute(cur).astype(o_ref.dtype)

    parity = lax.rem(pl.program_id(0), 2)

    @pl.when(parity == 0)
    def _():
        run(bufs1, bufs0)

    @pl.when(parity == 1)
    def _():
        run(bufs0, bufs1)


def _chunk_spec(kc, bn, n_tiles, j0=0):
    return pl.BlockSpec((kc, bn), lambda j, i: (i, j0 + jnp.minimum(j, n_tiles - 1)))


def _prev_tile(j):
    return jnp.maximum(j - 1, 0)


def _row_tile(j, i):
    return jnp.where(j > 0, i, 0)


def _matmul_kernel(a_ref, chunk_ref, *rest, scaled):
    s_ref = rest[0] if scaled else None
    o_ref, wb0_ref, wb1_ref = rest[-3:]

    def compute(w):
        acc = jnp.dot(a_ref[...], w[0][...], preferred_element_type=F32)
        return acc * s_ref[...] if scaled else acc

    _stream_weights((chunk_ref,), (wb0_ref,), (wb1_ref,), o_ref, compute)


def _matmul(a, b, name, col_scale=None, bm=1024, bn=1024):
    m, k = a.shape
    _, n = b.shape
    n_tiles, row_tiles = n // bn, m // bm
    in_specs = [pl.BlockSpec((bm, k), lambda j, i: (_row_tile(j, i), 0)),
                _chunk_spec(k // row_tiles, bn, n_tiles)]
    args = (a, b)
    if col_scale is not None:
        in_specs.append(pl.BlockSpec((1, bn), lambda j, i: (0, _prev_tile(j))))
        args += (col_scale,)
    return pl.pallas_call(
        functools.partial(_matmul_kernel, scaled=col_scale is not None),
        grid=(n_tiles + 1, row_tiles),
        in_specs=in_specs,
        out_specs=pl.BlockSpec((bm, bn), lambda j, i: (_row_tile(j, i), _prev_tile(j))),
        out_shape=jax.ShapeDtypeStruct((m, n), BF16),
        scratch_shapes=[pltpu.VMEM((k, bn), BF16), pltpu.VMEM((k, bn), BF16)],
        compiler_params=_params(("arbitrary", "arbitrary")),
        name=name,
    )(*args)


def _lru_kernel(xa_ref, za_ref, cw_ref, cb_ref, wa_ref, wi_ref, ba_ref, bi_ref,
                lam_ref, o_ref, xe_ref, hc_ref, *, bm, n_blocks):
    @pl.when(pl.program_id(0) == 0)
    def _():
        xe_ref[0:SUBLANES, :] = jnp.zeros((SUBLANES, xe_ref.shape[1]), F32)
        hc_ref[...] = jnp.zeros(hc_ref.shape, F32)

    groups = bm // SUBLANES
    sub = lax.broadcasted_iota(jnp.int32, (groups, SUBLANES, LRU_BLOCK_W), 1)

    def block(n, carry):
        sl = pl.ds(pl.multiple_of(n * LRU_BLOCK_W, LRU_BLOCK_W), LRU_BLOCK_W)
        xa = xa_ref[:, sl].astype(F32)
        xe_ref[SUBLANES:, sl] = xa
        xc = cb_ref[:, sl] + cw_ref[CONV_W - 1:CONV_W, sl] * xa
        for k in range(CONV_W - 1):
            shift = CONV_W - 1 - k
            xc = xc + cw_ref[k:k + 1, sl] * xe_ref[pl.ds(SUBLANES - shift, bm), sl]
        xe_ref[0:SUBLANES, sl] = xa[bm - SUBLANES:, :]

        xcb = xc.astype(BF16)
        r = _sigmoid(jnp.dot(xcb, wa_ref[n], preferred_element_type=F32) + ba_ref[:, sl])
        i = _sigmoid(jnp.dot(xcb, wi_ref[n], preferred_element_type=F32) + bi_ref[:, sl])
        lam = lam_ref[:, sl]
        softplus_neg_lam = jnp.maximum(-lam, 0.0) + jnp.log1p(jnp.exp(-jnp.abs(lam)))
        log_a = (-LRU_C * r) * softplus_neg_lam
        a = jnp.exp(log_a)
        y = -jnp.tanh(log_a) * (a * a + 1.0)
        u = jnp.where(y > 0.0, y * lax.rsqrt(y), 0.0) * (i * xc)

        a = a.reshape(groups, SUBLANES, LRU_BLOCK_W)
        u = u.reshape(groups, SUBLANES, LRU_BLOCK_W)
        d = 1
        while d < SUBLANES:
            keep = sub >= d
            a_prev = jnp.where(keep, pltpu.roll(a, d, 1), 1.0)
            u_prev = jnp.where(keep, pltpu.roll(u, d, 1), 0.0)
            u = u + a * u_prev
            a = a * a_prev
            d *= 2
        h_prev = hc_ref[0:1, sl]
        hs = []
        for g in range(groups):
            hg = a[g] * h_prev + u[g]
            hs.append(hg)
            h_prev = hg[SUBLANES - 1:SUBLANES, :]
        hc_ref[0:1, sl] = h_prev
        h = jnp.concatenate(hs, axis=0)

        z = za_ref[:, sl].astype(F32)
        o_ref[:, sl] = (h * (z * _sigmoid(z))).astype(o_ref.dtype)
        return carry

    lax.fori_loop(0, n_blocks, block, 0)


def _lru_branch(proj, conv_w, conv_b, w_a, w_i, b_a, b_i, lam, width, bm=256):
    s = proj.shape[0]
    n_blocks = width // LRU_BLOCK_W
    vec = lambda rows: pl.BlockSpec((rows, width), lambda i: (0, 0))
    wspec = pl.BlockSpec((n_blocks, LRU_BLOCK_W, LRU_BLOCK_W), lambda i: (0, 0, 0))
    return pl.pallas_call(
        functools.partial(_lru_kernel, bm=bm, n_blocks=n_blocks),
        grid=(s // bm,),
        in_specs=[pl.BlockSpec((bm, width), lambda i: (i, 0)),
                  pl.BlockSpec((bm, width), lambda i: (i, 1)),
                  vec(CONV_W), vec(1), wspec, wspec, vec(1), vec(1), vec(1)],
        out_specs=pl.BlockSpec((bm, width), lambda i: (i, 0)),
        out_shape=jax.ShapeDtypeStruct((s, width), BF16),
        scratch_shapes=[pltpu.VMEM((bm + SUBLANES, width), F32),
                        pltpu.VMEM((SUBLANES, width), F32)],
        compiler_params=_params(("arbitrary",)),
        name="rglru_branch",
    )(proj, proj, conv_w, conv_b, w_a, w_i, b_a, b_i, lam)


def _attn_kernel(q_ref, k_ref, v_ref, zb_ref, w_ref,
                 o_ref, bias_ref, s_ref, p_ref, l_ref, kt_ref, vh_ref, *, n_heads):
    bq = q_ref.shape[0]
    nk = ATT_KEY_BLOCKS * bq
    step = pl.program_id(0)

    @pl.when(step < ATT_KEY_BLOCKS)
    def _():
        qi = lax.broadcasted_iota(jnp.int32, (bq, nk), 0)
        kj = lax.broadcasted_iota(jnp.int32, (bq, nk), 1)
        q_chunk = lax.shift_right_logical(qi, CHUNK.bit_length() - 1)
        k_chunk = lax.shift_right_logical(kj, CHUNK.bit_length() - 1)
        visible = ((k_chunk >= q_chunk) & (k_chunk <= q_chunk + CTX_CHUNKS)
                   & (kj + step * bq >= (ATT_KEY_BLOCKS - 1) * bq))

        def build(h, carry):
            rows = jnp.broadcast_to(w_ref[h], (bq, w_ref.shape[2]))
            toeplitz = pltpu.roll(rows, 0, 1, stride=1, stride_axis=0)
            bias_ref[h] = jnp.where(visible, toeplitz[:, :nk], NEG_INF)
            return carry

        lax.fori_loop(0, n_heads, build, 0)

    def head_cols(h):
        return pl.ds(pl.multiple_of(h * ATT_HEAD_DIM, ATT_HEAD_DIM), ATT_HEAD_DIM)

    @pl.when(step == 0)
    def _():
        kt_ref[...] = jnp.zeros(kt_ref.shape, kt_ref.dtype)
        vh_ref[...] = jnp.zeros(vh_ref.shape, vh_ref.dtype)

    old = nk - bq
    vh_ref[0:old, :] = vh_ref[bq:nk, :]
    vh_ref[old:nk, :] = v_ref[...]

    def shift_keys(h, carry):
        kt_ref[h, :, 0:old] = kt_ref[h, :, bq:nk]
        kt_ref[h, :, old:nk] = k_ref[:, head_cols(h)].T
        return carry

    lax.fori_loop(0, n_heads, shift_keys, 0, unroll=4)

    def scores(h, carry):
        s = jnp.dot(q_ref[:, head_cols(h)], kt_ref[h], preferred_element_type=F32)
        s_ref[h] = s + bias_ref[h]
        return carry

    def numerators(h, carry):
        s = s_ref[h]
        m = jnp.max(s, axis=-1, keepdims=True)
        p = jnp.exp2(s - m)
        l_ref[h] = jnp.broadcast_to(jnp.sum(p, axis=-1, keepdims=True), l_ref.shape[1:])
        p_ref[h] = p.astype(BF16)
        return carry

    def values(h, carry):
        hs = head_cols(h)
        o = jnp.dot(p_ref[h], vh_ref[:, hs], preferred_element_type=F32) / l_ref[h]
        z = zb_ref[:, hs].astype(F32)
        o_ref[:, hs] = (o * (z * _sigmoid(z))).astype(o_ref.dtype)
        return carry

    lax.fori_loop(0, n_heads, scores, 0, unroll=n_heads)
    lax.fori_loop(0, n_heads, numerators, 0, unroll=4)
    lax.fori_loop(0, n_heads, values, 0, unroll=n_heads)


def _bias_by_offset(rel_bias):
    n_heads = rel_bias.shape[0]
    bq = ATT_BLOCK_Q
    nk = ATT_KEY_BLOCKS * bq
    off = nk - bq
    length = nk + bq
    n_far = off - REL_CLIP + 1
    n_near = nk - n_far - (2 * REL_CLIP - 1)
    far = rel_bias[:, 2 * REL_CLIP:]
    near = rel_bias[:, :1]
    w = jnp.concatenate([
        jnp.broadcast_to(far, (n_heads, n_far)),
        jnp.flip(rel_bias[:, 1:2 * REL_CLIP], axis=1),
        jnp.broadcast_to(near, (n_heads, n_near + 1)),
        jnp.broadcast_to(far, (n_heads, bq - 1)),
    ], axis=1).astype(F32)
    assert w.shape[1] == length
    return (w * LOG2_E)[:, None, :]


def _attn_branch(proj, rel_bias, att_width, col0):
    s = proj.shape[0]
    n_heads = att_width // ATT_HEAD_DIM
    bq = ATT_BLOCK_Q
    nk = ATT_KEY_BLOCKS * bq
    w = _bias_by_offset(rel_bias)

    col = lambda c: pl.BlockSpec((bq, att_width), lambda i: (i, c))
    return pl.pallas_call(
        functools.partial(_attn_kernel, n_heads=n_heads),
        grid=(s // bq,),
        in_specs=[col(col0), col(col0 + 1), col(col0 + 2), col(col0 + 3),
                  pl.BlockSpec(w.shape, lambda i: (0, 0, 0))],
        out_specs=pl.BlockSpec((bq, att_width), lambda i: (i, 0)),
        out_shape=jax.ShapeDtypeStruct((s, att_width), BF16),
        scratch_shapes=[pltpu.VMEM((n_heads, bq, nk), F32),
                        pltpu.VMEM((n_heads, bq, nk), F32),
                        pltpu.VMEM((n_heads, bq, nk), BF16),
                        pltpu.VMEM((n_heads, bq, ATT_HEAD_DIM), F32),
                        pltpu.VMEM((n_heads, ATT_HEAD_DIM, nk), BF16),
                        pltpu.VMEM((nk, att_width), BF16)],
        compiler_params=_params(("arbitrary",)),
        name="chunk_attention",
    )(proj, proj, proj, proj, w)


def _merge_kernel(ya_ref, yb_ref, ca_ref, cb_ref, ga_ref, gb_ref, o_ref,
                  wa0_ref, wb0_ref, wa1_ref, wb1_ref):
    def compute(w):
        pa = jnp.dot(ya_ref[...], w[0][...], preferred_element_type=F32)
        pb = jnp.dot(yb_ref[...], w[1][...], preferred_element_type=F32)
        ga = jax.nn.sigmoid(ga_ref[...].astype(F32))
        gb = jax.nn.sigmoid(gb_ref[...].astype(F32))
        return ga * pa + gb * pb

    _stream_weights((ca_ref, cb_ref), (wa0_ref, wb0_ref), (wa1_ref, wb1_ref), o_ref, compute)


def _merge(y_a, y_b, w_pa, w_pb, proj, gate_start, bm=512, bn=1024):
    m, ka = y_a.shape
    kb = y_b.shape[1]
    n = w_pa.shape[1]
    n_tiles, row_tiles = n // bn, m // bm
    gate_col0 = gate_start // bn
    return pl.pallas_call(
        _merge_kernel,
        grid=(n_tiles + 1, row_tiles),
        in_specs=[pl.BlockSpec((bm, ka), lambda j, i: (_row_tile(j, i), 0)),
                  pl.BlockSpec((bm, kb), lambda j, i: (_row_tile(j, i), 0)),
                  _chunk_spec(ka // row_tiles, bn, n_tiles),
                  _chunk_spec(kb // row_tiles, bn, n_tiles),
                  pl.BlockSpec((bm, bn), lambda j, i: (_row_tile(j, i), gate_col0 + _prev_tile(j))),
                  pl.BlockSpec((bm, bn),
                               lambda j, i: (_row_tile(j, i), gate_col0 + n_tiles + _prev_tile(j)))],
        out_specs=pl.BlockSpec((bm, bn), lambda j, i: (_row_tile(j, i), _prev_tile(j))),
        out_shape=jax.ShapeDtypeStruct((m, n), BF16),
        scratch_shapes=[pltpu.VMEM((ka, bn), BF16), pltpu.VMEM((kb, bn), BF16),
                        pltpu.VMEM((ka, bn), BF16), pltpu.VMEM((kb, bn), BF16)],
        compiler_params=_params(("arbitrary", "arbitrary")),
        name="branch_merge",
    )(y_a, y_b, w_pa, w_pb, proj, proj)


def _rstd(v):
    return lax.rsqrt(jnp.mean(v * v, axis=-1, keepdims=True) + EPS)


def _rowwise_kernel(x_ref, t_ref, p_ref, wple_ref, gpost_ref, gpre_ref, gple_ref,
                    hn_ref, pn_ref, rt_ref):
    t = t_ref[...].astype(F32)
    rstd_t = _rstd(t)
    h = x_ref[...] + (t * rstd_t) * gpost_ref[...]
    hn_ref[...] = _rms_norm_f32(h, gpre_ref[...]).astype(hn_ref.dtype)
    pe = jnp.dot(p_ref[...].astype(BF16), wple_ref[...], preferred_element_type=F32)
    pn_ref[...] = _rms_norm_f32(pe, gple_ref[...]).astype(pn_ref.dtype)
    rt_ref[...] = jnp.broadcast_to(rstd_t, rt_ref.shape)


def _rowwise(x, t, p, w_ple, g_post, g_ple_pre, g_ple_post, bm=256):
    s, d = x.shape
    pd = p.shape[1]
    row = lambda w: pl.BlockSpec((bm, w), lambda i: (i, 0))
    vec = pl.BlockSpec((1, d), lambda i: (0, 0))
    return pl.pallas_call(
        _rowwise_kernel,
        grid=(s // bm,),
        in_specs=[row(d), row(d), row(pd), pl.BlockSpec((pd, d), lambda i: (0, 0)), vec, vec, vec],
        out_specs=[row(d), row(d), row(LANES)],
        out_shape=[jax.ShapeDtypeStruct((s, d), BF16),
                   jax.ShapeDtypeStruct((s, d), BF16),
                   jax.ShapeDtypeStruct((s, LANES), F32)],
        compiler_params=_params(("parallel",)),
        name="residual_norms",
    )(x, t, p, w_ple, g_post, g_ple_pre, g_ple_post)


def _ple_kernel(hn_ref, chunk_ref, x_ref, t_ref, pn_ref, rt_ref, gpost_ref, o_ref, wb0_ref, wb1_ref):
    def compute(w):
        g = jnp.dot(hn_ref[...], w[0][...], preferred_element_type=F32)
        h = x_ref[...] + (t_ref[...].astype(F32) * rt_ref[:, 0:1]) * gpost_ref[...]
        return h + pn_ref[...].astype(F32) * jax.nn.sigmoid(g)

    _stream_weights((chunk_ref,), (wb0_ref,), (wb1_ref,), o_ref, compute)


def _ple_gate(hn, w, x, t, pn, rstd_t, g_post, bm=512, bn=1024):
    m, k = hn.shape
    n = w.shape[1]
    n_tiles, row_tiles = n // bn, m // bm
    tile = pl.BlockSpec((bm, bn), lambda j, i: (_row_tile(j, i), _prev_tile(j)))
    rows = lambda width: pl.BlockSpec((bm, width), lambda j, i: (_row_tile(j, i), 0))
    return pl.pallas_call(
        _ple_kernel,
        grid=(n_tiles + 1, row_tiles),
        in_specs=[rows(k), _chunk_spec(k // row_tiles, bn, n_tiles),
                  tile, tile, tile, rows(LANES),
                  pl.BlockSpec((1, bn), lambda j, i: (0, _prev_tile(j)))],
        out_specs=tile,
        out_shape=jax.ShapeDtypeStruct((m, n), F32),
        scratch_shapes=[pltpu.VMEM((k, bn), BF16), pltpu.VMEM((k, bn), BF16)],
        compiler_params=_params(("arbitrary", "arbitrary")),
        name="ple_gate",
    )(hn, w, x, t, pn, rstd_t, g_post)


def _layer(x, p_i, w_in, conv_w, conv_b, w_rg_a, b_rg_a, w_rg_i, b_rg_i, lru_lambda,
           rel_bias, w_proj_a, w_proj_b, w_out, g_pre, g_post,
           w_ple, w_ple_gate, g_ple_pre, g_ple_post):
    d = x.shape[1]
    lru_width = w_proj_a.shape[0]
    att_width = w_proj_b.shape[0]
    assert lru_width == d and 2 * att_width == d
    row = lambda v: v.reshape(1, -1)

    xn = _rmsnorm(x, row(g_pre))
    q_start = 2 * lru_width
    col_scale = jnp.ones((1, w_in.shape[1]), F32).at[:, q_start:q_start + att_width].set(
        ATT_HEAD_DIM ** -0.5 * LOG2_E)
    proj = _matmul(xn, w_in, "in_proj", col_scale=col_scale)
    y_a = _lru_branch(proj, conv_w, row(conv_b), w_rg_a.astype(BF16), w_rg_i.astype(BF16),
                      row(b_rg_a), row(b_rg_i), row(lru_lambda), lru_width)
    y_b = _attn_branch(proj, rel_bias, att_width, col0=2 * lru_width // att_width)
    merged = _merge(y_a, y_b, w_proj_a, w_proj_b, proj, gate_start=2 * lru_width + 4 * att_width)
    t = _matmul(merged, w_out, "out_proj")
    hn, pn, rstd_t = _rowwise(x, t, p_i, w_ple.astype(BF16), row(g_post), row(g_ple_pre), row(g_ple_post))
    return _ple_gate(hn, w_ple_gate, x, t, pn, rstd_t, row(g_post))


def kernel(x, p, w_in, conv_w, conv_b, w_rg_a, b_rg_a, w_rg_i, b_rg_i, lru_lambda, rel_bias,
           w_proj_a, w_proj_b, w_out, g_pre, g_post, w_ple, w_ple_gate, g_ple_pre, g_ple_post):
    batch = x.shape[0]
    outs = []
    for b in range(batch):
        h = x[b]
        for l in range(w_in.shape[0]):
            h = _layer(h, p[l, b], w_in[l], conv_w[l], conv_b[l], w_rg_a[l], b_rg_a[l],
                       w_rg_i[l], b_rg_i[l], lru_lambda[l], rel_bias[l], w_proj_a[l],
                       w_proj_b[l], w_out[l], g_pre[l], g_post[l], w_ple[l],
                       w_ple_gate[l], g_ple_pre[l], g_ple_post[l])
        outs.append(h)
    return jnp.stack(outs, axis=0)
```

```python
import functools
import math

import jax
import jax.numpy as jnp
from jax import lax
from jax.experimental import pallas as pl
from jax.experimental.pallas import tpu as pltpu

F32 = jnp.float32
BF16 = jnp.bfloat16

EPS = 1e-6
NEG_INF = -1e30
LRU_C = 8.0
LOG2_E = math.log2(math.e)

CHUNK = 64
CTX_CHUNKS = 8
REL_CLIP = 128
ATT_HEAD_DIM = 128
LRU_BLOCK_W = 256
CONV_W = 4

SUBLANES = 8
LANES = 128
ATT_BLOCK_Q = 256
ATT_KEY_BLOCKS = 1 + (CTX_CHUNKS * CHUNK) // ATT_BLOCK_Q
VMEM_LIMIT_BYTES = 56 * 1024 * 1024


def _params(semantics):
    return pltpu.CompilerParams(dimension_semantics=semantics,
                                vmem_limit_bytes=VMEM_LIMIT_BYTES)


def _sigmoid(x):
    return 0.5 * jnp.tanh(0.5 * x) + 0.5


def _rms_norm_f32(x, g):
    ms = jnp.mean(x * x, axis=-1, keepdims=True)
    return (x * lax.rsqrt(ms + EPS)) * g


def _rmsnorm_kernel(x_ref, g_ref, o_ref):
    o_ref[...] = _rms_norm_f32(x_ref[...], g_ref[...]).astype(o_ref.dtype)


def _rmsnorm(x, g, bm=512):
    s, d = x.shape
    return pl.pallas_call(
        _rmsnorm_kernel,
        grid=(s // bm,),
        in_specs=[pl.BlockSpec((bm, d), lambda i: (i, 0)),
                  pl.BlockSpec((1, d), lambda i: (0, 0))],
        out_specs=pl.BlockSpec((bm, d), lambda i: (i, 0)),
        out_shape=jax.ShapeDtypeStruct((s, d), BF16),
        compiler_params=_params(("parallel",)),
        name="rmsnorm_pre",
    )(x, g)


def _cast_chunk(chunk_ref, w_next_ref):
    kc = chunk_ref.shape[0]
    rows = pl.ds(pl.multiple_of(pl.program_id(1) * kc, kc), kc)
    w_next_ref[rows, :] = chunk_ref[...].astype(BF16)


def _stream_weights(chunk_refs, bufs0, bufs1, o_ref, compute):
    def run(cur, nxt):
        def cast():
            for chunk_ref, w_next_ref in zip(chunk_refs, nxt):
                _cast_chunk(chunk_ref, w_next_ref)

        @pl.when(pl.program_id(0) == 0)
        def _():
            cast()

        @pl.when(pl.program_id(0) > 0)
        def _():
            cast()
            o_ref[...] = compute(cur).astype(o_ref.dtype)

    parity = lax.rem(pl.program_id(0), 2)

    @pl.when(parity == 0)
    def _():
        run(bufs1, bufs0)

    @pl.when(parity == 1)
    def _():
        run(bufs0, bufs1)


def _chunk_spec(kc, bn, n_tiles, j0=0):
    return pl.BlockSpec((kc, bn), lambda j, i: (i, j0 + jnp.minimum(j, n_tiles - 1)))


def _prev_tile(j):
    return jnp.maximum(j - 1, 0)


def _row_tile(j, i):
    return jnp.where(j > 0, i, 0)


def _matmul_kernel(a_ref, chunk_ref, *rest, scaled):
    s_ref = rest[0] if scaled else None
    o_ref, wb0_ref, wb1_ref = rest[-3:]

    def compute(w):
        acc = jnp.dot(a_ref[...], w[0][...], preferred_element_type=F32)
        return acc * s_ref[...] if scaled else acc

    _stream_weights((chunk_ref,), (wb0_ref,), (wb1_ref,), o_ref, compute)


def _matmul(a, b, name, col_scale=None, bm=1024, bn=1024):
    m, k = a.shape
    _, n = b.shape
    n_tiles, row_tiles = n // bn, m // bm
    in_specs = [pl.BlockSpec((bm, k), lambda j, i: (_row_tile(j, i), 0)),
                _chunk_spec(k // row_tiles, bn, n_tiles)]
    args = (a, b)
    if col_scale is not None:
        in_specs.append(pl.BlockSpec((1, bn), lambda j, i: (0, _prev_tile(j))))
        args += (col_scale,)
    return pl.pallas_call(
        functools.partial(_matmul_kernel, scaled=col_scale is not None),
        grid=(n_tiles + 1, row_tiles),
        in_specs=in_specs,
        out_specs=pl.BlockSpec((bm, bn), lambda j, i: (_row_tile(j, i), _prev_tile(j))),
        out_shape=jax.ShapeDtypeStruct((m, n), BF16),
        scratch_shapes=[pltpu.VMEM((k, bn), BF16), pltpu.VMEM((k, bn), BF16)],
        compiler_params=_params(("arbitrary", "arbitrary")),
        name=name,
    )(*args)


def _lru_kernel(xa_ref, za_ref, cw_ref, cb_ref, wa_ref, wi_ref, ba_ref, bi_ref,
                lam_ref, o_ref, xe_ref, hc_ref, *, bm, n_blocks):
    @pl.when(pl.program_id(0) == 0)
    def _():
        xe_ref[0:SUBLANES, :] = jnp.zeros((SUBLANES, xe_ref.shape[1]), F32)
        hc_ref[...] = jnp.zeros(hc_ref.shape, F32)

    groups = bm // SUBLANES
    sub = lax.broadcasted_iota(jnp.int32, (groups, SUBLANES, LRU_BLOCK_W), 1)

    def block(n, carry):
        sl = pl.ds(pl.multiple_of(n * LRU_BLOCK_W, LRU_BLOCK_W), LRU_BLOCK_W)
        xa = xa_ref[:, sl].astype(F32)
        xe_ref[SUBLANES:, sl] = xa
        xc = cb_ref[:, sl] + cw_ref[CONV_W - 1:CONV_W, sl] * xa
        for k in range(CONV_W - 1):
            shift = CONV_W - 1 - k
            xc = xc + cw_ref[k:k + 1, sl] * xe_ref[pl.ds(SUBLANES - shift, bm), sl]
        xe_ref[0:SUBLANES, sl] = xa[bm - SUBLANES:, :]

        xcb = xc.astype(BF16)
        r = _sigmoid(jnp.dot(xcb, wa_ref[n], preferred_element_type=F32) + ba_ref[:, sl])
        i = _sigmoid(jnp.dot(xcb, wi_ref[n], preferred_element_type=F32) + bi_ref[:, sl])
        lam = lam_ref[:, sl]
        softplus_neg_lam = jnp.maximum(-lam, 0.0) + jnp.log1p(jnp.exp(-jnp.abs(lam)))
        log_a = (-LRU_C * r) * softplus_neg_lam
        a = jnp.exp(log_a)
        y = -jnp.tanh(log_a) * (a * a + 1.0)
        u = jnp.where(y > 0.0, y * lax.rsqrt(y), 0.0) * (i * xc)

        a = a.reshape(groups, SUBLANES, LRU_BLOCK_W)
        u = u.reshape(groups, SUBLANES, LRU_BLOCK_W)
        d = 1
        while d < SUBLANES:
            keep = sub >= d
            a_prev = jnp.where(keep, pltpu.roll(a, d, 1), 1.0)
            u_prev = jnp.where(keep, pltpu.roll(u, d, 1), 0.0)
            u = u + a * u_prev
            a = a * a_prev
            d *= 2
        h_prev = hc_ref[0:1, sl]
        hs = []
        for g in range(groups):
            hg = a[g] * h_prev + u[g]
            hs.append(hg)
            h_prev = hg[SUBLANES - 1:SUBLANES, :]
        hc_ref[0:1, sl] = h_prev
        h = jnp.concatenate(hs, axis=0)

        z = za_ref[:, sl].astype(F32)
        o_ref[:, sl] = (h * (z * _sigmoid(z))).astype(o_ref.dtype)
        return carry

    lax.fori_loop(0, n_blocks, block, 0)


def _lru_branch(proj, conv_w, conv_b, w_a, w_i, b_a, b_i, lam, width, bm=512):
    s = proj.shape[0]
    n_blocks = width // LRU_BLOCK_W
    vec = lambda rows: pl.BlockSpec((rows, width), lambda i: (0, 0))
    wspec = pl.BlockSpec((n_blocks, LRU_BLOCK_W, LRU_BLOCK_W), lambda i: (0, 0, 0))
    return pl.pallas_call(
        functools.partial(_lru_kernel, bm=bm, n_blocks=n_blocks),
        grid=(s // bm,),
        in_specs=[pl.BlockSpec((bm, width), lambda i: (i, 0)),
                  pl.BlockSpec((bm, width), lambda i: (i, 1)),
                  vec(CONV_W), vec(1), wspec, wspec, vec(1), vec(1), vec(1)],
        out_specs=pl.BlockSpec((bm, width), lambda i: (i, 0)),
        out_shape=jax.ShapeDtypeStruct((s, width), BF16),
        scratch_shapes=[pltpu.VMEM((bm + SUBLANES, width), F32),
                        pltpu.VMEM((SUBLANES, width), F32)],
        compiler_params=_params(("arbitrary",)),
        name="rglru_branch",
    )(proj, proj, conv_w, conv_b, w_a, w_i, b_a, b_i, lam)


def _attn_kernel(q_ref, k0_ref, k1_ref, k2_ref, v0_ref, v1_ref, v2_ref, zb_ref, w_ref,
                 o_ref, bias_ref, s_ref, p_ref, l_ref, *, n_heads):
    bq = q_ref.shape[0]
    nk = ATT_KEY_BLOCKS * bq
    step = pl.program_id(0)

    @pl.when(step < ATT_KEY_BLOCKS)
    def _():
        qi = lax.broadcasted_iota(jnp.int32, (bq, nk), 0)
        kj = lax.broadcasted_iota(jnp.int32, (bq, nk), 1)
        q_chunk = lax.shift_right_logical(qi, CHUNK.bit_length() - 1)
        k_chunk = lax.shift_right_logical(kj, CHUNK.bit_length() - 1)
        visible = ((k_chunk >= q_chunk) & (k_chunk <= q_chunk + CTX_CHUNKS)
                   & (kj + step * bq >= (ATT_KEY_BLOCKS - 1) * bq))

        def build(h, carry):
            rows = jnp.broadcast_to(w_ref[h], (bq, w_ref.shape[2]))
            toeplitz = pltpu.roll(rows, 0, 1, stride=1, stride_axis=0)
            bias_ref[h] = jnp.where(visible, toeplitz[:, :nk], NEG_INF)
            return carry

        lax.fori_loop(0, n_heads, build, 0)

    def head_cols(h):
        return pl.ds(pl.multiple_of(h * ATT_HEAD_DIM, ATT_HEAD_DIM), ATT_HEAD_DIM)

    def scores(h, carry):
        hs = head_cols(h)
        kh = jnp.concatenate([k0_ref[:, hs], k1_ref[:, hs], k2_ref[:, hs]], axis=0)
        s = lax.dot_general(q_ref[:, hs], kh, (((1,), (1,)), ((), ())),
                            preferred_element_type=F32)
        s_ref[h] = s + bias_ref[h]
        return carry

    def numerators(h, carry):
        s = s_ref[h]
        m = jnp.max(s, axis=-1, keepdims=True)
        p = jnp.exp2(s - m)
        l_ref[h] = jnp.broadcast_to(jnp.sum(p, axis=-1, keepdims=True), l_ref.shape[1:])
        p_ref[h] = p.astype(BF16)
        return carry

    def values(h, carry):
        hs = head_cols(h)
        vh = jnp.concatenate([v0_ref[:, hs], v1_ref[:, hs], v2_ref[:, hs]], axis=0)
        o = jnp.dot(p_ref[h], vh, preferred_element_type=F32) / l_ref[h]
        z = zb_ref[:, hs].astype(F32)
        o_ref[:, hs] = (o * (z * _sigmoid(z))).astype(o_ref.dtype)
        return carry

    lax.fori_loop(0, n_heads, scores, 0, unroll=n_heads)
    lax.fori_loop(0, n_heads, numerators, 0, unroll=4)
    lax.fori_loop(0, n_heads, values, 0, unroll=n_heads)


def _bias_by_offset(rel_bias):
    n_heads = rel_bias.shape[0]
    bq = ATT_BLOCK_Q
    nk = ATT_KEY_BLOCKS * bq
    off = nk - bq
    length = nk + bq
    n_far = off - REL_CLIP + 1
    n_near = nk - n_far - (2 * REL_CLIP - 1)
    far = rel_bias[:, 2 * REL_CLIP:]
    near = rel_bias[:, :1]
    w = jnp.concatenate([
        jnp.broadcast_to(far, (n_heads, n_far)),
        jnp.flip(rel_bias[:, 1:2 * REL_CLIP], axis=1),
        jnp.broadcast_to(near, (n_heads, n_near + 1)),
        jnp.broadcast_to(far, (n_heads, bq - 1)),
    ], axis=1).astype(F32)
    assert w.shape[1] == length
    return (w * LOG2_E)[:, None, :]


def _attn_branch(proj, rel_bias, att_width, col0):
    s = proj.shape[0]
    n_heads = att_width // ATT_HEAD_DIM
    bq = ATT_BLOCK_Q
    nk = ATT_KEY_BLOCKS * bq
    w = _bias_by_offset(rel_bias)

    def kv_spec(col, back):
        return pl.BlockSpec((bq, att_width), lambda i: (jnp.maximum(i - back, 0), col))

    return pl.pallas_call(
        functools.partial(_attn_kernel, n_heads=n_heads),
        grid=(s // bq,),
        in_specs=[pl.BlockSpec((bq, att_width), lambda i: (i, col0)),
                  kv_spec(col0 + 1, 2), kv_spec(col0 + 1, 1), kv_spec(col0 + 1, 0),
                  kv_spec(col0 + 2, 2), kv_spec(col0 + 2, 1), kv_spec(col0 + 2, 0),
                  pl.BlockSpec((bq, att_width), lambda i: (i, col0 + 3)),
                  pl.BlockSpec(w.shape, lambda i: (0, 0, 0))],
        out_specs=pl.BlockSpec((bq, att_width), lambda i: (i, 0)),
        out_shape=jax.ShapeDtypeStruct((s, att_width), BF16),
        scratch_shapes=[pltpu.VMEM((n_heads, bq, nk), F32),
                        pltpu.VMEM((n_heads, bq, nk), F32),
                        pltpu.VMEM((n_heads, bq, nk), BF16),
                        pltpu.VMEM((n_heads, bq, ATT_HEAD_DIM), F32)],
        compiler_params=_params(("arbitrary",)),
        name="chunk_attention",
    )(proj, proj, proj, proj, proj, proj, proj, proj, w)


def _merge_kernel(ya_ref, yb_ref, ca_ref, cb_ref, ga_ref, gb_ref, o_ref,
                  wa0_ref, wb0_ref, wa1_ref, wb1_ref):
    def compute(w):
        pa = jnp.dot(ya_ref[...], w[0][...], preferred_element_type=F32)
        pb = jnp.dot(yb_ref[...], w[1][...], preferred_element_type=F32)
        ga = jax.nn.sigmoid(ga_ref[...].astype(F32))
        gb = jax.nn.sigmoid(gb_ref[...].astype(F32))
        return ga * pa + gb * pb

    _stream_weights((ca_ref, cb_ref), (wa0_ref, wb0_ref), (wa1_ref, wb1_ref), o_ref, compute)


def _merge(y_a, y_b, w_pa, w_pb, proj, gate_start, bm=1024, bn=512):
    m, ka = y_a.shape
    kb = y_b.shape[1]
    n = w_pa.shape[1]
    n_tiles, row_tiles = n // bn, m // bm
    gate_col0 = gate_start // bn
    return pl.pallas_call(
        _merge_kernel,
        grid=(n_tiles + 1, row_tiles),
        in_specs=[pl.BlockSpec((bm, ka), lambda j, i: (_row_tile(j, i), 0)),
                  pl.BlockSpec((bm, kb), lambda j, i: (_row_tile(j, i), 0)),
                  _chunk_spec(ka // row_tiles, bn, n_tiles),
                  _chunk_spec(kb // row_tiles, bn, n_tiles),
                  pl.BlockSpec((bm, bn), lambda j, i: (_row_tile(j, i), gate_col0 + _prev_tile(j))),
                  pl.BlockSpec((bm, bn),
                               lambda j, i: (_row_tile(j, i), gate_col0 + n_tiles + _prev_tile(j)))],
        out_specs=pl.BlockSpec((bm, bn), lambda j, i: (_row_tile(j, i), _prev_tile(j))),
        out_shape=jax.ShapeDtypeStruct((m, n), BF16),
        scratch_shapes=[pltpu.VMEM((ka, bn), BF16), pltpu.VMEM((kb, bn), BF16),
                        pltpu.VMEM((ka, bn), BF16), pltpu.VMEM((kb, bn), BF16)],
        compiler_params=_params(("arbitrary", "arbitrary")),
        name="branch_merge",
    )(y_a, y_b, w_pa, w_pb, proj, proj)


def _rstd(v):
    return lax.rsqrt(jnp.mean(v * v, axis=-1, keepdims=True) + EPS)


def _rowwise_kernel(x_ref, t_ref, p_ref, wple_ref, gpost_ref, gpre_ref, gple_ref,
                    hn_ref, pn_ref, rt_ref):
    t = t_ref[...].astype(F32)
    rstd_t = _rstd(t)
    h = x_ref[...] + (t * rstd_t) * gpost_ref[...]
    hn_ref[...] = _rms_norm_f32(h, gpre_ref[...]).astype(hn_ref.dtype)
    pe = jnp.dot(p_ref[...].astype(BF16), wple_ref[...], preferred_element_type=F32)
    pn_ref[...] = _rms_norm_f32(pe, gple_ref[...]).astype(pn_ref.dtype)
    rt_ref[...] = jnp.broadcast_to(rstd_t, rt_ref.shape)


def _rowwise(x, t, p, w_ple, g_post, g_ple_pre, g_ple_post, bm=256):
    s, d = x.shape
    pd = p.shape[1]
    row = lambda w: pl.BlockSpec((bm, w), lambda i: (i, 0))
    vec = pl.BlockSpec((1, d), lambda i: (0, 0))
    return pl.pallas_call(
        _rowwise_kernel,
        grid=(s // bm,),
        in_specs=[row(d), row(d), row(pd), pl.BlockSpec((pd, d), lambda i: (0, 0)), vec, vec, vec],
        out_specs=[row(d), row(d), row(LANES)],
        out_shape=[jax.ShapeDtypeStruct((s, d), BF16),
                   jax.ShapeDtypeStruct((s, d), BF16),
                   jax.ShapeDtypeStruct((s, LANES), F32)],
        compiler_params=_params(("parallel",)),
        name="residual_norms",
    )(x, t, p, w_ple, g_post, g_ple_pre, g_ple_post)


def _ple_kernel(hn_ref, chunk_ref, x_ref, t_ref, pn_ref, rt_ref, gpost_ref, o_ref, wb0_ref, wb1_ref):
    def compute(w):
        g = jnp.dot(hn_ref[...], w[0][...], preferred_element_type=F32)
        h = x_ref[...] + (t_ref[...].astype(F32) * rt_ref[:, 0:1]) * gpost_ref[...]
        return h + pn_ref[...].astype(F32) * jax.nn.sigmoid(g)

    _stream_weights((chunk_ref,), (wb0_ref,), (wb1_ref,), o_ref, compute)


def _ple_gate(hn, w, x, t, pn, rstd_t, g_post, bm=512, bn=1024):
    m, k = hn.shape
    n = w.shape[1]
    n_tiles, row_tiles = n // bn, m // bm
    tile = pl.BlockSpec((bm, bn), lambda j, i: (_row_tile(j, i), _prev_tile(j)))
    rows = lambda width: pl.BlockSpec((bm, width), lambda j, i: (_row_tile(j, i), 0))
    return pl.pallas_call(
        _ple_kernel,
        grid=(n_tiles + 1, row_tiles),
        in_specs=[rows(k), _chunk_spec(k // row_tiles, bn, n_tiles),
                  tile, tile, tile, rows(LANES),
                  pl.BlockSpec((1, bn), lambda j, i: (0, _prev_tile(j)))],
        out_specs=tile,
        out_shape=jax.ShapeDtypeStruct((m, n), F32),
        scratch_shapes=[pltpu.VMEM((k, bn), BF16), pltpu.VMEM((k, bn), BF16)],
        compiler_params=_params(("arbitrary", "arbitrary")),
        name="ple_gate",
    )(hn, w, x, t, pn, rstd_t, g_post)


def _layer(x, p_i, w_in, conv_w, conv_b, w_rg_a, b_rg_a, w_rg_i, b_rg_i, lru_lambda,
           rel_bias, w_proj_a, w_proj_b, w_out, g_pre, g_post,
           w_ple, w_ple_gate, g_ple_pre, g_ple_post):
    d = x.shape[1]
    lru_width = w_proj_a.shape[0]
    att_width = w_proj_b.shape[0]
    assert lru_width == d and 2 * att_width == d
    row = lambda v: v.reshape(1, -1)

    xn = _rmsnorm(x, row(g_pre))
    q_start = 2 * lru_width
    col_scale = jnp.ones((1, w_in.shape[1]), F32).at[:, q_start:q_start + att_width].set(
        ATT_HEAD_DIM ** -0.5 * LOG2_E)
    proj = _matmul(xn, w_in, "in_proj", col_scale=col_scale)
    y_a = _lru_branch(proj, conv_w, row(conv_b), w_rg_a.astype(BF16), w_rg_i.astype(BF16),
                      row(b_rg_a), row(b_rg_i), row(lru_lambda), lru_width)
    y_b = _attn_branch(proj, rel_bias, att_width, col0=2 * lru_width // att_width)
    merged = _merge(y_a, y_b, w_proj_a, w_proj_b, proj, gate_start=2 * lru_width + 4 * att_width)
    t = _matmul(merged, w_out, "out_proj")
    hn, pn, rstd_t = _rowwise(x, t, p_i, w_ple.astype(BF16), row(g_post), row(g_ple_pre), row(g_ple_post))
    return _ple_gate(hn, w_ple_gate, x, t, pn, rstd_t, row(g_post))


def kernel(x, p, w_in, conv_w, conv_b, w_rg_a, b_rg_a, w_rg_i, b_rg_i, lru_lambda, rel_bias,
           w_proj_a, w_proj_b, w_out, g_pre, g_post, w_ple, w_ple_gate, g_ple_pre, g_ple_post):
    batch = x.shape[0]
    outs = []
    for b in range(batch):
        h = x[b]
        for l in range(w_in.shape[0]):
            h = _layer(h, p[l, b], w_in[l], conv_w[l], conv_b[l], w_rg_a[l], b_rg_a[l],
                       w_rg_i[l], b_rg_i[l], lru_lambda[l], rel_bias[l], w_proj_a[l],
                       w_proj_b[l], w_out[l], g_pre[l], g_post[l], w_ple[l],
                       w_ple_gate[l], g_ple_pre[l], g_ple_post[l])
        outs.append(h)
    return jnp.stack(outs, axis=0)
```

```python
import functools
import math

import jax
import jax.numpy as jnp
from jax import lax
from jax.experimental import pallas as pl
from jax.experimental.pallas import tpu as pltpu

F32 = jnp.float32
BF16 = jnp.bfloat16

EPS = 1e-6
NEG_INF = -1e30
LRU_C = 8.0
LOG2_E = math.log2(math.e)

CHUNK = 64
CTX_CHUNKS = 8
REL_CLIP = 128
ATT_HEAD_DIM = 128
LRU_BLOCK_W = 256
CONV_W = 4

SUBLANES = 8
LANES = 128
ATT_BLOCK_Q = 256
ATT_KEY_BLOCKS = 1 + (CTX_CHUNKS * CHUNK) // ATT_BLOCK_Q
VMEM_LIMIT_BYTES = 56 * 1024 * 1024


def _params(semantics):
    return pltpu.CompilerParams(dimension_semantics=semantics,
                                vmem_limit_bytes=VMEM_LIMIT_BYTES)


def _sigmoid(x):
    return 0.5 * jnp.tanh(0.5 * x) + 0.5


def _rms_norm_f32(x, g):
    ms = jnp.mean(x * x, axis=-1, keepdims=True)
    return (x * lax.rsqrt(ms + EPS)) * g


def _rmsnorm_kernel(x_ref, g_ref, o_ref):
    o_ref[...] = _rms_norm_f32(x_ref[...], g_ref[...]).astype(o_ref.dtype)


def _rmsnorm(x, g, bm=512):
    s, d = x.shape
    return pl.pallas_call(
        _rmsnorm_kernel,
        grid=(s // bm,),
        in_specs=[pl.BlockSpec((bm, d), lambda i: (i, 0)),
                  pl.BlockSpec((1, d), lambda i: (0, 0))],
        out_specs=pl.BlockSpec((bm, d), lambda i: (i, 0)),
        out_shape=jax.ShapeDtypeStruct((s, d), BF16),
        compiler_params=_params(("parallel",)),
        name="rmsnorm_pre",
    )(x, g)


def _cast_chunk(chunk_ref, w_next_ref):
    kc = chunk_ref.shape[0]
    rows = pl.ds(pl.multiple_of(pl.program_id(1) * kc, kc), kc)
    w_next_ref[rows, :] = chunk_ref[...].astype(BF16)


def _stream_weights(chunk_refs, bufs0, bufs1, o_ref, compute):
    def run(cur, nxt):
        def cast():
            for chunk_ref, w_next_ref in zip(chunk_refs, nxt):
                _cast_chunk(chunk_ref, w_next_ref)

        @pl.when(pl.program_id(0) == 0)
        def _():
            cast()

        @pl.when(pl.program_id(0) > 0)
        def _():
            cast()
            o_ref[...] = compute(cur).astype(o_ref.dtype)

    parity = lax.rem(pl.program_id(0), 2)

    @pl.when(parity == 0)
    def _():
        run(bufs1, bufs0)

    @pl.when(parity == 1)
    def _():
        run(bufs0, bufs1)


def _chunk_spec(kc, bn, n_tiles, j0=0):
    return pl.BlockSpec((kc, bn), lambda j, i: (i, j0 + jnp.minimum(j, n_tiles - 1)))


def _prev_tile(j):
    return jnp.maximum(j - 1, 0)


def _row_tile(j, i):
    return jnp.where(j > 0, i, 0)


def _matmul_kernel(a_ref, chunk_ref, *rest, scaled):
    s_ref = rest[0] if scaled else None
    o_ref, wb0_ref, wb1_ref = rest[-3:]

    def compute(w):
        acc = jnp.dot(a_ref[...], w[0][...], preferred_element_type=F32)
        return acc * s_ref[...] if scaled else acc

    _stream_weights((chunk_ref,), (wb0_ref,), (wb1_ref,), o_ref, compute)


def _matmul(a, b, name, col_scale=None, bm=1024, bn=1024):
    m, k = a.shape
    _, n = b.shape
    n_tiles, row_tiles = n // bn, m // bm
    in_specs = [pl.BlockSpec((bm, k), lambda j, i: (_row_tile(j, i), 0)),
                _chunk_spec(k // row_tiles, bn, n_tiles)]
    args = (a, b)
    if col_scale is not None:
        in_specs.append(pl.BlockSpec((1, bn), lambda j, i: (0, _prev_tile(j))))
        args += (col_scale,)
    return pl.pallas_call(
        functools.partial(_matmul_kernel, scaled=col_scale is not None),
        grid=(n_tiles + 1, row_tiles),
        in_specs=in_specs,
        out_specs=pl.BlockSpec((bm, bn), lambda j, i: (_row_tile(j, i), _prev_tile(j))),
        out_shape=jax.ShapeDtypeStruct((m, n), BF16),
        scratch_shapes=[pltpu.VMEM((k, bn), BF16), pltpu.VMEM((k, bn), BF16)],
        compiler_params=_params(("arbitrary", "arbitrary")),
        name=name,
    )(*args)


def _lru_kernel(xa_ref, za_ref, cw_ref, cb_ref, wa_ref, wi_ref, ba_ref, bi_ref,
                lam_ref, o_ref, xe_ref, hc_ref, *, bm, n_blocks):
    @pl.when(pl.program_id(0) == 0)
    def _():
        xe_ref[0:SUBLANES, :] = jnp.zeros((SUBLANES, xe_ref.shape[1]), F32)
        hc_ref[...] = jnp.zeros(hc_ref.shape, F32)

    groups = bm // SUBLANES
    sub = lax.broadcasted_iota(jnp.int32, (groups, SUBLANES, LRU_BLOCK_W), 1)

    def block(n, carry):
        sl = pl.ds(pl.multiple_of(n * LRU_BLOCK_W, LRU_BLOCK_W), LRU_BLOCK_W)
        xa = xa_ref[:, sl].astype(F32)
        xe_ref[SUBLANES:, sl] = xa
        xc = cb_ref[:, sl] + cw_ref[CONV_W - 1:CONV_W, sl] * xa
        for k in range(CONV_W - 1):
            shift = CONV_W - 1 - k
            xc = xc + cw_ref[k:k + 1, sl] * xe_ref[pl.ds(SUBLANES - shift, bm), sl]
        xe_ref[0:SUBLANES, sl] = xa[bm - SUBLANES:, :]

        xcb = xc.astype(BF16)
        r = _sigmoid(jnp.dot(xcb, wa_ref[n], preferred_element_type=F32) + ba_ref[:, sl])
        i = _sigmoid(jnp.dot(xcb, wi_ref[n], preferred_element_type=F32) + bi_ref[:, sl])
        lam = lam_ref[:, sl]
        softplus_neg_lam = jnp.maximum(-lam, 0.0) + jnp.log1p(jnp.exp(-jnp.abs(lam)))
        log_a = (-LRU_C * r) * softplus_neg_lam
        a = jnp.exp(log_a)
        y = -jnp.tanh(log_a) * (a * a + 1.0)
        u = jnp.where(y > 0.0, y * lax.rsqrt(y), 0.0) * (i * xc)

        a = a.reshape(groups, SUBLANES, LRU_BLOCK_W)
        u = u.reshape(groups, SUBLANES, LRU_BLOCK_W)
        d = 1
        while d < SUBLANES:
            keep = sub >= d
            a_prev = jnp.where(keep, pltpu.roll(a, d, 1), 1.0)
            u_prev = jnp.where(keep, pltpu.roll(u, d, 1), 0.0)
            u = u + a * u_prev
            a = a * a_prev
            d *= 2
        h_prev = hc_ref[0:1, sl]
        hs = []
        for g in range(groups):
            hg = a[g] * h_prev + u[g]
            hs.append(hg)
            h_prev = hg[SUBLANES - 1:SUBLANES, :]
        hc_ref[0:1, sl] = h_prev
        h = jnp.concatenate(hs, axis=0)

        z = za_ref[:, sl].astype(F32)
        o_ref[:, sl] = (h * (z * _sigmoid(z))).astype(o_ref.dtype)
        return carry

    lax.fori_loop(0, n_blocks, block, 0)


def _lru_branch(proj, conv_w, conv_b, w_a, w_i, b_a, b_i, lam, width, bm=512):
    s = proj.shape[0]
    n_blocks = width // LRU_BLOCK_W
    vec = lambda rows: pl.BlockSpec((rows, width), lambda i: (0, 0))
    wspec = pl.BlockSpec((n_blocks, LRU_BLOCK_W, LRU_BLOCK_W), lambda i: (0, 0, 0))
    return pl.pallas_call(
        functools.partial(_lru_kernel, bm=bm, n_blocks=n_blocks),
        grid=(s // bm,),
        in_specs=[pl.BlockSpec((bm, width), lambda i: (i, 0)),
                  pl.BlockSpec((bm, width), lambda i: (i, 1)),
                  vec(CONV_W), vec(1), wspec, wspec, vec(1), vec(1), vec(1)],
        out_specs=pl.BlockSpec((bm, width), lambda i: (i, 0)),
        out_shape=jax.ShapeDtypeStruct((s, width), BF16),
        scratch_shapes=[pltpu.VMEM((bm + SUBLANES, width), F32),
                        pltpu.VMEM((SUBLANES, width), F32)],
        compiler_params=_params(("arbitrary",)),
        name="rglru_branch",
    )(proj, proj, conv_w, conv_b, w_a, w_i, b_a, b_i, lam)


def _attn_kernel(q_ref, k0_ref, k1_ref, k2_ref, v0_ref, v1_ref, v2_ref, zb_ref, w_ref,
                 o_ref, bias_ref, s_ref, p_ref, l_ref, *, n_heads):
    bq = q_ref.shape[0]
    nk = ATT_KEY_BLOCKS * bq
    step = pl.program_id(0)

    @pl.when(step < ATT_KEY_BLOCKS)
    def _():
        qi = lax.broadcasted_iota(jnp.int32, (bq, nk), 0)
        kj = lax.broadcasted_iota(jnp.int32, (bq, nk), 1)
        q_chunk = lax.shift_right_logical(qi, CHUNK.bit_length() - 1)
        k_chunk = lax.shift_right_logical(kj, CHUNK.bit_length() - 1)
        visible = ((k_chunk >= q_chunk) & (k_chunk <= q_chunk + CTX_CHUNKS)
                   & (kj + step * bq >= (ATT_KEY_BLOCKS - 1) * bq))

        def build(h, carry):
            rows = jnp.broadcast_to(w_ref[h], (bq, w_ref.shape[2]))
            toeplitz = pltpu.roll(rows, 0, 1, stride=1, stride_axis=0)
            bias_ref[h] = jnp.where(visible, toeplitz[:, :nk], NEG_INF)
            return carry

        lax.fori_loop(0, n_heads, build, 0)

    def head_cols(h):
        return pl.ds(pl.multiple_of(h * ATT_HEAD_DIM, ATT_HEAD_DIM), ATT_HEAD_DIM)

    def scores(h, carry):
        hs = head_cols(h)
        kh = jnp.concatenate([k0_ref[:, hs], k1_ref[:, hs], k2_ref[:, hs]], axis=0)
        s = lax.dot_general(q_ref[:, hs], kh, (((1,), (1,)), ((), ())),
                            preferred_element_type=F32)
        s_ref[h] = s + bias_ref[h]
        return carry

    def numerators(h, carry):
        s = s_ref[h]
        m = jnp.max(s, axis=-1, keepdims=True)
        p = jnp.exp2(s - m)
        l_ref[h] = jnp.broadcast_to(jnp.sum(p, axis=-1, keepdims=True), l_ref.shape[1:])
        p_ref[h] = p.astype(BF16)
        return carry

    def values(h, carry):
        hs = head_cols(h)
        vh = jnp.concatenate([v0_ref[:, hs], v1_ref[:, hs], v2_ref[:, hs]], axis=0)
        o = jnp.dot(p_ref[h], vh, preferred_element_type=F32) / l_ref[h]
        z = zb_ref[:, hs].astype(F32)
        o_ref[:, hs] = (o * (z * _sigmoid(z))).astype(o_ref.dtype)
        return carry

    lax.fori_loop(0, n_heads, scores, 0, unroll=n_heads)
    lax.fori_loop(0, n_heads, numerators, 0, unroll=n_heads)
    lax.fori_loop(0, n_heads, values, 0, unroll=n_heads)


def _bias_by_offset(rel_bias):
    n_heads = rel_bias.shape[0]
    bq = ATT_BLOCK_Q
    nk = ATT_KEY_BLOCKS * bq
    off = nk - bq
    length = nk + bq
    n_far = off - REL_CLIP + 1
    n_near = nk - n_far - (2 * REL_CLIP - 1)
    far = rel_bias[:, 2 * REL_CLIP:]
    near = rel_bias[:, :1]
    w = jnp.concatenate([
        jnp.broadcast_to(far, (n_heads, n_far)),
        jnp.flip(rel_bias[:, 1:2 * REL_CLIP], axis=1),
        jnp.broadcast_to(near, (n_heads, n_near + 1)),
        jnp.broadcast_to(far, (n_heads, bq - 1)),
    ], axis=1).astype(F32)
    assert w.shape[1] == length
    return (w * LOG2_E)[:, None, :]


def _attn_branch(proj, rel_bias, att_width, col0):
    s = proj.shape[0]
    n_heads = att_width // ATT_HEAD_DIM
    bq = ATT_BLOCK_Q
    nk = ATT_KEY_BLOCKS * bq
    w = _bias_by_offset(rel_bias)

    def kv_spec(col, back):
        return pl.BlockSpec((bq, att_width), lambda i: (jnp.maximum(i - back, 0), col))

    return pl.pallas_call(
        functools.partial(_attn_kernel, n_heads=n_heads),
        grid=(s // bq,),
        in_specs=[pl.BlockSpec((bq, att_width), lambda i: (i, col0)),
                  kv_spec(col0 + 1, 2), kv_spec(col0 + 1, 1), kv_spec(col0 + 1, 0),
                  kv_spec(col0 + 2, 2), kv_spec(col0 + 2, 1), kv_spec(col0 + 2, 0),
                  pl.BlockSpec((bq, att_width), lambda i: (i, col0 + 3)),
                  pl.BlockSpec(w.shape, lambda i: (0, 0, 0))],
        out_specs=pl.BlockSpec((bq, att_width), lambda i: (i, 0)),
        out_shape=jax.ShapeDtypeStruct((s, att_width), BF16),
        scratch_shapes=[pltpu.VMEM((n_heads, bq, nk), F32),
                        pltpu.VMEM((n_heads, bq, nk), F32),
                        pltpu.VMEM((n_heads, bq, nk), BF16),
                        pltpu.VMEM((n_heads, bq, ATT_HEAD_DIM), F32)],
        compiler_params=_params(("arbitrary",)),
        name="chunk_attention",
    )(proj, proj, proj, proj, proj, proj, proj, proj, w)


def _merge_kernel(ya_ref, yb_ref, ca_ref, cb_ref, ga_ref, gb_ref, o_ref,
                  wa0_ref, wb0_ref, wa1_ref, wb1_ref):
    def compute(w):
        pa = jnp.dot(ya_ref[...], w[0][...], preferred_element_type=F32)
        pb = jnp.dot(yb_ref[...], w[1][...], preferred_element_type=F32)
        ga = jax.nn.sigmoid(ga_ref[...].astype(F32))
        gb = jax.nn.sigmoid(gb_ref[...].astype(F32))
        return ga * pa + gb * pb

    _stream_weights((ca_ref, cb_ref), (wa0_ref, wb0_ref), (wa1_ref, wb1_ref), o_ref, compute)


def _merge(y_a, y_b, w_pa, w_pb, proj, gate_start, bm=512, bn=1024):
    m, ka = y_a.shape
    kb = y_b.shape[1]
    n = w_pa.shape[1]
    n_tiles, row_tiles = n // bn, m // bm
    gate_col0 = gate_start // bn
    return pl.pallas_call(
        _merge_kernel,
        grid=(n_tiles + 1, row_tiles),
        in_specs=[pl.BlockSpec((bm, ka), lambda j, i: (_row_tile(j, i), 0)),
                  pl.BlockSpec((bm, kb), lambda j, i: (_row_tile(j, i), 0)),
                  _chunk_spec(ka // row_tiles, bn, n_tiles),
                  _chunk_spec(kb // row_tiles, bn, n_tiles),
                  pl.BlockSpec((bm, bn), lambda j, i: (_row_tile(j, i), gate_col0 + _prev_tile(j))),
                  pl.BlockSpec((bm, bn),
                               lambda j, i: (_row_tile(j, i), gate_col0 + n_tiles + _prev_tile(j)))],
        out_specs=pl.BlockSpec((bm, bn), lambda j, i: (_row_tile(j, i), _prev_tile(j))),
        out_shape=jax.ShapeDtypeStruct((m, n), BF16),
        scratch_shapes=[pltpu.VMEM((ka, bn), BF16), pltpu.VMEM((kb, bn), BF16),
                        pltpu.VMEM((ka, bn), BF16), pltpu.VMEM((kb, bn), BF16)],
        compiler_params=_params(("arbitrary", "arbitrary")),
        name="branch_merge",
    )(y_a, y_b, w_pa, w_pb, proj, proj)


def _rstd(v):
    return lax.rsqrt(jnp.mean(v * v, axis=-1, keepdims=True) + EPS)


def _rowwise_kernel(x_ref, t_ref, p_ref, wple_ref, gpost_ref, gpre_ref, gple_ref,
                    hn_ref, pn_ref, rt_ref):
    t = t_ref[...].astype(F32)
    rstd_t = _rstd(t)
    h = x_ref[...] + (t * rstd_t) * gpost_ref[...]
    hn_ref[...] = _rms_norm_f32(h, gpre_ref[...]).astype(hn_ref.dtype)
    pe = jnp.dot(p_ref[...].astype(BF16), wple_ref[...], preferred_element_type=F32)
    pn_ref[...] = _rms_norm_f32(pe, gple_ref[...]).astype(pn_ref.dtype)
    rt_ref[...] = jnp.broadcast_to(rstd_t, rt_ref.shape)


def _rowwise(x, t, p, w_ple, g_post, g_ple_pre, g_ple_post, bm=256):
    s, d = x.shape
    pd = p.shape[1]
    row = lambda w: pl.BlockSpec((bm, w), lambda i: (i, 0))
    vec = pl.BlockSpec((1, d), lambda i: (0, 0))
    return pl.pallas_call(
        _rowwise_kernel,
        grid=(s // bm,),
        in_specs=[row(d), row(d), row(pd), pl.BlockSpec((pd, d), lambda i: (0, 0)), vec, vec, vec],
        out_specs=[row(d), row(d), row(LANES)],
        out_shape=[jax.ShapeDtypeStruct((s, d), BF16),
                   jax.ShapeDtypeStruct((s, d), BF16),
                   jax.ShapeDtypeStruct((s, LANES), F32)],
        compiler_params=_params(("parallel",)),
        name="residual_norms",
    )(x, t, p, w_ple, g_post, g_ple_pre, g_ple_post)


def _ple_kernel(hn_ref, chunk_ref, x_ref, t_ref, pn_ref, rt_ref, gpost_ref, o_ref, wb0_ref, wb1_ref):
    def compute(w):
        g = jnp.dot(hn_ref[...], w[0][...], preferred_element_type=F32)
        h = x_ref[...] + (t_ref[...].astype(F32) * rt_ref[:, 0:1]) * gpost_ref[...]
        return h + pn_ref[...].astype(F32) * jax.nn.sigmoid(g)

    _stream_weights((chunk_ref,), (wb0_ref,), (wb1_ref,), o_ref, compute)


def _ple_gate(hn, w, x, t, pn, rstd_t, g_post, bm=512, bn=1024):
    m, k = hn.shape
    n = w.shape[1]
    n_tiles, row_tiles = n // bn, m // bm
    tile = pl.BlockSpec((bm, bn), lambda j, i: (_row_tile(j, i), _prev_tile(j)))
    rows = lambda width: pl.BlockSpec((bm, width), lambda j, i: (_row_tile(j, i), 0))
    return pl.pallas_call(
        _ple_kernel,
        grid=(n_tiles + 1, row_tiles),
        in_specs=[rows(k), _chunk_spec(k // row_tiles, bn, n_tiles),
                  tile, tile, tile, rows(LANES),
                  pl.BlockSpec((1, bn), lambda j, i: (0, _prev_tile(j)))],
        out_specs=tile,
        out_shape=jax.ShapeDtypeStruct((m, n), F32),
        scratch_shapes=[pltpu.VMEM((k, bn), BF16), pltpu.VMEM((k, bn), BF16)],
        compiler_params=_params(("arbitrary", "arbitrary")),
        name="ple_gate",
    )(hn, w, x, t, pn, rstd_t, g_post)


def _layer(x, p_i, w_in, conv_w, conv_b, w_rg_a, b_rg_a, w_rg_i, b_rg_i, lru_lambda,
           rel_bias, w_proj_a, w_proj_b, w_out, g_pre, g_post,
           w_ple, w_ple_gate, g_ple_pre, g_ple_post):
    d = x.shape[1]
    lru_width = w_proj_a.shape[0]
    att_width = w_proj_b.shape[0]
    assert lru_width == d and 2 * att_width == d
    row = lambda v: v.reshape(1, -1)

    xn = _rmsnorm(x, row(g_pre))
    q_start = 2 * lru_width
    col_scale = jnp.ones((1, w_in.shape[1]), F32).at[:, q_start:q_start + att_width].set(
        ATT_HEAD_DIM ** -0.5 * LOG2_E)
    proj = _matmul(xn, w_in, "in_proj", col_scale=col_scale)
    y_a = _lru_branch(proj, conv_w, row(conv_b), w_rg_a.astype(BF16), w_rg_i.astype(BF16),
                      row(b_rg_a), row(b_rg_i), row(lru_lambda), lru_width)
    y_b = _attn_branch(proj, rel_bias, att_width, col0=2 * lru_width // att_width)
    merged = _merge(y_a, y_b, w_proj_a, w_proj_b, proj, gate_start=2 * lru_width + 4 * att_width)
    t = _matmul(merged, w_out, "out_proj")
    hn, pn, rstd_t = _rowwise(x, t, p_i, w_ple.astype(BF16), row(g_post), row(g_ple_pre), row(g_ple_post))
    return _ple_gate(hn, w_ple_gate, x, t, pn, rstd_t, row(g_post))


def kernel(x, p, w_in, conv_w, conv_b, w_rg_a, b_rg_a, w_rg_i, b_rg_i, lru_lambda, rel_bias,
           w_proj_a, w_proj_b, w_out, g_pre, g_post, w_ple, w_ple_gate, g_ple_pre, g_ple_post):
    batch = x.shape[0]
    outs = []
    for b in range(batch):
        h = x[b]
        for l in range(w_in.shape[0]):
            h = _layer(h, p[l, b], w_in[l], conv_w[l], conv_b[l], w_rg_a[l], b_rg_a[l],
                       w_rg_i[l], b_rg_i[l], lru_lambda[l], rel_bias[l], w_proj_a[l],
                       w_proj_b[l], w_out[l], g_pre[l], g_post[l], w_ple[l],
                       w_ple_gate[l], g_ple_pre[l], g_ple_post[l])
        outs.append(h)
    return jnp.stack(outs, axis=0)
```

```python
import functools
import math

import jax
import jax.numpy as jnp
from jax import lax
from jax.experimental import pallas as pl
from jax.experimental.pallas import tpu as pltpu

F32 = jnp.float32
BF16 = jnp.bfloat16

EPS = 1e-6
NEG_INF = -1e30
LRU_C = 8.0
LOG2_E = math.log2(math.e)

CHUNK = 64
CTX_CHUNKS = 8
REL_CLIP = 128
ATT_HEAD_DIM = 128
LRU_BLOCK_W = 256
CONV_W = 4

SUBLANES = 8
LANES = 128
ATT_BLOCK_Q = 256
ATT_KEY_BLOCKS = 1 + (CTX_CHUNKS * CHUNK) // ATT_BLOCK_Q
ATT_HALF_Q = ATT_BLOCK_Q // 2
ATT_HALF_KEYS = ATT_HALF_Q + CTX_CHUNKS * CHUNK
VMEM_LIMIT_BYTES = 56 * 1024 * 1024


def _params(semantics):
    return pltpu.CompilerParams(dimension_semantics=semantics,
                                vmem_limit_bytes=VMEM_LIMIT_BYTES)


def _sigmoid(x):
    return 0.5 * jnp.tanh(0.5 * x) + 0.5


def _rms_norm_f32(x, g):
    ms = jnp.mean(x * x, axis=-1, keepdims=True)
    return (x * lax.rsqrt(ms + EPS)) * g


def _rmsnorm_kernel(x_ref, g_ref, o_ref):
    o_ref[...] = _rms_norm_f32(x_ref[...], g_ref[...]).astype(o_ref.dtype)


def _rmsnorm(x, g, bm=512):
    s, d = x.shape
    return pl.pallas_call(
        _rmsnorm_kernel,
        grid=(s // bm,),
        in_specs=[pl.BlockSpec((bm, d), lambda i: (i, 0)),
                  pl.BlockSpec((1, d), lambda i: (0, 0))],
        out_specs=pl.BlockSpec((bm, d), lambda i: (i, 0)),
        out_shape=jax.ShapeDtypeStruct((s, d), BF16),
        compiler_params=_params(("parallel",)),
        name="rmsnorm_pre",
    )(x, g)


def _cast_chunk(chunk_ref, w_next_ref):
    kc = chunk_ref.shape[0]
    rows = pl.ds(pl.multiple_of(pl.program_id(1) * kc, kc), kc)
    w_next_ref[rows, :] = chunk_ref[...].astype(BF16)


def _stream_weights(chunk_refs, bufs0, bufs1, o_ref, compute):
    def run(cur, nxt):
        def cast():
            for chunk_ref, w_next_ref in zip(chunk_refs, nxt):
                _cast_chunk(chunk_ref, w_next_ref)

        @pl.when(pl.program_id(0) == 0)
        def _():
            cast()

        @pl.when(pl.program_id(0) > 0)
        def _():
            cast()
            o_ref[...] = compute(cur).astype(o_ref.dtype)

    parity = lax.rem(pl.program_id(0), 2)

    @pl.when(parity == 0)
    def _():
        run(bufs1, bufs0)

    @pl.when(parity == 1)
    def _():
        run(bufs0, bufs1)


def _chunk_spec(kc, bn, n_tiles, j0=0):
    return pl.BlockSpec((kc, bn), lambda j, i: (i, j0 + jnp.minimum(j, n_tiles - 1)))


def _prev_tile(j):
    return jnp.maximum(j - 1, 0)


def _row_tile(j, i):
    return jnp.where(j > 0, i, 0)


def _matmul_kernel(a_ref, chunk_ref, *rest, scaled):
    s_ref = rest[0] if scaled else None
    o_ref, wb0_ref, wb1_ref = rest[-3:]

    def compute(w):
        acc = jnp.dot(a_ref[...], w[0][...], preferred_element_type=F32)
        return acc * s_ref[...] if scaled else acc

    _stream_weights((chunk_ref,), (wb0_ref,), (wb1_ref,), o_ref, compute)


def _matmul(a, b, name, col_scale=None, bm=1024, bn=1024):
    m, k = a.shape
    _, n = b.shape
    n_tiles, row_tiles = n // bn, m // bm
    in_specs = [pl.BlockSpec((bm, k), lambda j, i: (_row_tile(j, i), 0)),
                _chunk_spec(k // row_tiles, bn, n_tiles)]
    args = (a, b)
    if col_scale is not None:
        in_specs.append(pl.BlockSpec((1, bn), lambda j, i: (0, _prev_tile(j))))
        args += (col_scale,)
    return pl.pallas_call(
        functools.partial(_matmul_kernel, scaled=col_scale is not None),
        grid=(n_tiles + 1, row_tiles),
        in_specs=in_specs,
        out_specs=pl.BlockSpec((bm, bn), lambda j, i: (_row_tile(j, i), _prev_tile(j))),
        out_shape=jax.ShapeDtypeStruct((m, n), BF16),
        scratch_shapes=[pltpu.VMEM((k, bn), BF16), pltpu.VMEM((k, bn), BF16)],
        compiler_params=_params(("arbitrary", "arbitrary")),
        name=name,
    )(*args)


def _lru_kernel(xa_ref, za_ref, cw_ref, cb_ref, wa_ref, wi_ref, ba_ref, bi_ref,
                lam_ref, o_ref, xe_ref, hc_ref, *, bm, n_blocks):
    @pl.when(pl.program_id(0) == 0)
    def _():
        xe_ref[0:SUBLANES, :] = jnp.zeros((SUBLANES, xe_ref.shape[1]), F32)
        hc_ref[...] = jnp.zeros(hc_ref.shape, F32)

    groups = bm // SUBLANES
    sub = lax.broadcasted_iota(jnp.int32, (groups, SUBLANES, LRU_BLOCK_W), 1)

    def block(n, carry):
        sl = pl.ds(pl.multiple_of(n * LRU_BLOCK_W, LRU_BLOCK_W), LRU_BLOCK_W)
        xa = xa_ref[:, sl].astype(F32)
        xe_ref[SUBLANES:, sl] = xa
        xc = cb_ref[:, sl] + cw_ref[CONV_W - 1:CONV_W, sl] * xa
        for k in range(CONV_W - 1):
            shift = CONV_W - 1 - k
            xc = xc + cw_ref[k:k + 1, sl] * xe_ref[pl.ds(SUBLANES - shift, bm), sl]
        xe_ref[0:SUBLANES, sl] = xa[bm - SUBLANES:, :]

        xcb = xc.astype(BF16)
        r = _sigmoid(jnp.dot(xcb, wa_ref[n], preferred_element_type=F32) + ba_ref[:, sl])
        i = _sigmoid(jnp.dot(xcb, wi_ref[n], preferred_element_type=F32) + bi_ref[:, sl])
        lam = lam_ref[:, sl]
        softplus_neg_lam = jnp.maximum(-lam, 0.0) + jnp.log1p(jnp.exp(-jnp.abs(lam)))
        log_a = (-LRU_C * r) * softplus_neg_lam
        a = jnp.exp(log_a)
        y = -jnp.tanh(log_a) * (a * a + 1.0)
        u = jnp.where(y > 0.0, y * lax.rsqrt(y), 0.0) * (i * xc)

        a = a.reshape(groups, SUBLANES, LRU_BLOCK_W)
        u = u.reshape(groups, SUBLANES, LRU_BLOCK_W)
        d = 1
        while d < SUBLANES:
            keep = sub >= d
            a_prev = jnp.where(keep, pltpu.roll(a, d, 1), 1.0)
            u_prev = jnp.where(keep, pltpu.roll(u, d, 1), 0.0)
            u = u + a * u_prev
            a = a * a_prev
            d *= 2
        h_prev = hc_ref[0:1, sl]
        hs = []
        for g in range(groups):
            hg = a[g] * h_prev + u[g]
            hs.append(hg)
            h_prev = hg[SUBLANES - 1:SUBLANES, :]
        hc_ref[0:1, sl] = h_prev
        h = jnp.concatenate(hs, axis=0)

        z = za_ref[:, sl].astype(F32)
        o_ref[:, sl] = (h * (z * _sigmoid(z))).astype(o_ref.dtype)
        return carry

    lax.fori_loop(0, n_blocks, block, 0)


def _lru_branch(proj, conv_w, conv_b, w_a, w_i, b_a, b_i, lam, width, bm=512):
    s = proj.shape[0]
    n_blocks = width // LRU_BLOCK_W
    vec = lambda rows: pl.BlockSpec((rows, width), lambda i: (0, 0))
    wspec = pl.BlockSpec((n_blocks, LRU_BLOCK_W, LRU_BLOCK_W), lambda i: (0, 0, 0))
    return pl.pallas_call(
        functools.partial(_lru_kernel, bm=bm, n_blocks=n_blocks),
        grid=(s // bm,),
        in_specs=[pl.BlockSpec((bm, width), lambda i: (i, 0)),
                  pl.BlockSpec((bm, width), lambda i: (i, 1)),
                  vec(CONV_W), vec(1), wspec, wspec, vec(1), vec(1), vec(1)],
        out_specs=pl.BlockSpec((bm, width), lambda i: (i, 0)),
        out_shape=jax.ShapeDtypeStruct((s, width), BF16),
        scratch_shapes=[pltpu.VMEM((bm + SUBLANES, width), F32),
                        pltpu.VMEM((SUBLANES, width), F32)],
        compiler_params=_params(("arbitrary",)),
        name="rglru_branch",
    )(proj, proj, conv_w, conv_b, w_a, w_i, b_a, b_i, lam)


def _attn_kernel(q_ref, k0_ref, k1_ref, k2_ref, v0_ref, v1_ref, v2_ref, zb_ref, w_ref,
                 o_ref, bias_ref, s_ref, p_ref, l_ref, *, n_heads):
    bq = q_ref.shape[0]
    nk = ATT_KEY_BLOCKS * bq
    step = pl.program_id(0)

    @pl.when(step < ATT_KEY_BLOCKS)
    def _():
        qi = lax.broadcasted_iota(jnp.int32, (bq, nk), 0)
        kj = lax.broadcasted_iota(jnp.int32, (bq, nk), 1)
        q_chunk = lax.shift_right_logical(qi, CHUNK.bit_length() - 1)
        k_chunk = lax.shift_right_logical(kj, CHUNK.bit_length() - 1)
        visible = ((k_chunk >= q_chunk) & (k_chunk <= q_chunk + CTX_CHUNKS)
                   & (kj + step * bq >= (ATT_KEY_BLOCKS - 1) * bq))

        def build(h, carry):
            rows = jnp.broadcast_to(w_ref[h], (bq, w_ref.shape[2]))
            toeplitz = pltpu.roll(rows, 0, 1, stride=1, stride_axis=0)
            bias_ref[h] = jnp.where(visible, toeplitz[:, :nk], NEG_INF)
            return carry

        lax.fori_loop(0, n_heads, build, 0)

    def head_cols(h):
        return pl.ds(pl.multiple_of(h * ATT_HEAD_DIM, ATT_HEAD_DIM), ATT_HEAD_DIM)

    def window(half):
        return slice(half * ATT_HALF_Q, half * ATT_HALF_Q + ATT_HALF_KEYS)

    def half_rows(half):
        return slice(half * ATT_HALF_Q, (half + 1) * ATT_HALF_Q)

    def scores(h, carry):
        hs = head_cols(h)
        kh = jnp.concatenate([k0_ref[:, hs], k1_ref[:, hs], k2_ref[:, hs]], axis=0)
        s = lax.dot_general(q_ref[:, hs], kh, (((1,), (1,)), ((), ())),
                            preferred_element_type=F32)
        for half in range(2):
            s_ref[h, half] = s[half_rows(half), window(half)] + bias_ref[h, half_rows(half), window(half)]
        return carry

    def numerators(h, carry):
        for half in range(2):
            s = s_ref[h, half]
            m = jnp.max(s, axis=-1, keepdims=True)
            p = jnp.exp2(s - m)
            l_ref[h, half_rows(half), :] = jnp.broadcast_to(jnp.sum(p, axis=-1, keepdims=True),
                                                            (ATT_HALF_Q, l_ref.shape[2]))
            p_ref[h, half] = p.astype(BF16)
        return carry

    def values(h, carry):
        hs = head_cols(h)
        vh = jnp.concatenate([v0_ref[:, hs], v1_ref[:, hs], v2_ref[:, hs]], axis=0)
        o = jnp.concatenate([jnp.dot(p_ref[h, half], vh[window(half)], preferred_element_type=F32)
                             for half in range(2)], axis=0) / l_ref[h]
        z = zb_ref[:, hs].astype(F32)
        o_ref[:, hs] = (o * (z * _sigmoid(z))).astype(o_ref.dtype)
        return carry

    lax.fori_loop(0, n_heads, scores, 0, unroll=n_heads)
    lax.fori_loop(0, n_heads, numerators, 0, unroll=n_heads)
    lax.fori_loop(0, n_heads, values, 0, unroll=n_heads)


def _bias_by_offset(rel_bias):
    n_heads = rel_bias.shape[0]
    bq = ATT_BLOCK_Q
    nk = ATT_KEY_BLOCKS * bq
    off = nk - bq
    length = nk + bq
    n_far = off - REL_CLIP + 1
    n_near = nk - n_far - (2 * REL_CLIP - 1)
    far = rel_bias[:, 2 * REL_CLIP:]
    near = rel_bias[:, :1]
    w = jnp.concatenate([
        jnp.broadcast_to(far, (n_heads, n_far)),
        jnp.flip(rel_bias[:, 1:2 * REL_CLIP], axis=1),
        jnp.broadcast_to(near, (n_heads, n_near + 1)),
        jnp.broadcast_to(far, (n_heads, bq - 1)),
    ], axis=1).astype(F32)
    assert w.shape[1] == length
    return (w * LOG2_E)[:, None, :]


def _attn_branch(proj, rel_bias, att_width, col0):
    s = proj.shape[0]
    n_heads = att_width // ATT_HEAD_DIM
    bq = ATT_BLOCK_Q
    nk = ATT_KEY_BLOCKS * bq
    w = _bias_by_offset(rel_bias)

    def kv_spec(col, back):
        return pl.BlockSpec((bq, att_width), lambda i: (jnp.maximum(i - back, 0), col))

    return pl.pallas_call(
        functools.partial(_attn_kernel, n_heads=n_heads),
        grid=(s // bq,),
        in_specs=[pl.BlockSpec((bq, att_width), lambda i: (i, col0)),
                  kv_spec(col0 + 1, 2), kv_spec(col0 + 1, 1), kv_spec(col0 + 1, 0),
                  kv_spec(col0 + 2, 2), kv_spec(col0 + 2, 1), kv_spec(col0 + 2, 0),
                  pl.BlockSpec((bq, att_width), lambda i: (i, col0 + 3)),
                  pl.BlockSpec(w.shape, lambda i: (0, 0, 0))],
        out_specs=pl.BlockSpec((bq, att_width), lambda i: (i, 0)),
        out_shape=jax.ShapeDtypeStruct((s, att_width), BF16),
        scratch_shapes=[pltpu.VMEM((n_heads, bq, nk), F32),
                        pltpu.VMEM((n_heads, 2, ATT_HALF_Q, ATT_HALF_KEYS), F32),
                        pltpu.VMEM((n_heads, 2, ATT_HALF_Q, ATT_HALF_KEYS), BF16),
                        pltpu.VMEM((n_heads, bq, ATT_HEAD_DIM), F32)],
        compiler_params=_params(("arbitrary",)),
        name="chunk_attention",
    )(proj, proj, proj, proj, proj, proj, proj, proj, w)


def _merge_kernel(ya_ref, yb_ref, ca_ref, cb_ref, ga_ref, gb_ref, o_ref,
                  wa0_ref, wb0_ref, wa1_ref, wb1_ref):
    def compute(w):
        pa = jnp.dot(ya_ref[...], w[0][...], preferred_element_type=F32)
        pb = jnp.dot(yb_ref[...], w[1][...], preferred_element_type=F32)
        ga = jax.nn.sigmoid(ga_ref[...].astype(F32))
        gb = jax.nn.sigmoid(gb_ref[...].astype(F32))
        return ga * pa + gb * pb

    _stream_weights((ca_ref, cb_ref), (wa0_ref, wb0_ref), (wa1_ref, wb1_ref), o_ref, compute)


def _merge(y_a, y_b, w_pa, w_pb, proj, gate_start, bm=512, bn=1024):
    m, ka = y_a.shape
    kb = y_b.shape[1]
    n = w_pa.shape[1]
    n_tiles, row_tiles = n // bn, m // bm
    gate_col0 = gate_start // bn
    return pl.pallas_call(
        _merge_kernel,
        grid=(n_tiles + 1, row_tiles),
        in_specs=[pl.BlockSpec((bm, ka), lambda j, i: (_row_tile(j, i), 0)),
                  pl.BlockSpec((bm, kb), lambda j, i: (_row_tile(j, i), 0)),
                  _chunk_spec(ka // row_tiles, bn, n_tiles),
                  _chunk_spec(kb // row_tiles, bn, n_tiles),
                  pl.BlockSpec((bm, bn), lambda j, i: (_row_tile(j, i), gate_col0 + _prev_tile(j))),
                  pl.BlockSpec((bm, bn),
                               lambda j, i: (_row_tile(j, i), gate_col0 + n_tiles + _prev_tile(j)))],
        out_specs=pl.BlockSpec((bm, bn), lambda j, i: (_row_tile(j, i), _prev_tile(j))),
        out_shape=jax.ShapeDtypeStruct((m, n), BF16),
        scratch_shapes=[pltpu.VMEM((ka, bn), BF16), pltpu.VMEM((kb, bn), BF16),
                        pltpu.VMEM((ka, bn), BF16), pltpu.VMEM((kb, bn), BF16)],
        compiler_params=_params(("arbitrary", "arbitrary")),
        name="branch_merge",
    )(y_a, y_b, w_pa, w_pb, proj, proj)


def _rstd(v):
    return lax.rsqrt(jnp.mean(v * v, axis=-1, keepdims=True) + EPS)


def _rowwise_kernel(x_ref, t_ref, p_ref, wple_ref, gpost_ref, gpre_ref, gple_ref,
                    hn_ref, pn_ref, rt_ref):
    t = t_ref[...].astype(F32)
    rstd_t = _rstd(t)
    h = x_ref[...] + (t * rstd_t) * gpost_ref[...]
    hn_ref[...] = _rms_norm_f32(h, gpre_ref[...]).astype(hn_ref.dtype)
    pe = jnp.dot(p_ref[...].astype(BF16), wple_ref[...], preferred_element_type=F32)
    pn_ref[...] = _rms_norm_f32(pe, gple_ref[...]).astype(pn_ref.dtype)
    rt_ref[...] = jnp.broadcast_to(rstd_t, rt_ref.shape)


def _rowwise(x, t, p, w_ple, g_post, g_ple_pre, g_ple_post, bm=256):
    s, d = x.shape
    pd = p.shape[1]
    row = lambda w: pl.BlockSpec((bm, w), lambda i: (i, 0))
    vec = pl.BlockSpec((1, d), lambda i: (0, 0))
    return pl.pallas_call(
        _rowwise_kernel,
        grid=(s // bm,),
        in_specs=[row(d), row(d), row(pd), pl.BlockSpec((pd, d), lambda i: (0, 0)), vec, vec, vec],
        out_specs=[row(d), row(d), row(LANES)],
        out_shape=[jax.ShapeDtypeStruct((s, d), BF16),
                   jax.ShapeDtypeStruct((s, d), BF16),
                   jax.ShapeDtypeStruct((s, LANES), F32)],
        compiler_params=_params(("parallel",)),
        name="residual_norms",
    )(x, t, p, w_ple, g_post, g_ple_pre, g_ple_post)


def _ple_kernel(hn_ref, chunk_ref, x_ref, t_ref, pn_ref, rt_ref, gpost_ref, o_ref, wb0_ref, wb1_ref):
    def compute(w):
        g = jnp.dot(hn_ref[...], w[0][...], preferred_element_type=F32)
        h = x_ref[...] + (t_ref[...].astype(F32) * rt_ref[:, 0:1]) * gpost_ref[...]
        return h + pn_ref[...].astype(F32) * jax.nn.sigmoid(g)

    _stream_weights((chunk_ref,), (wb0_ref,), (wb1_ref,), o_ref, compute)


def _ple_gate(hn, w, x, t, pn, rstd_t, g_post, bm=512, bn=1024):
    m, k = hn.shape
    n = w.shape[1]
    n_tiles, row_tiles = n // bn, m // bm
    tile = pl.BlockSpec((bm, bn), lambda j, i: (_row_tile(j, i), _prev_tile(j)))
    rows = lambda width: pl.BlockSpec((bm, width), lambda j, i: (_row_tile(j, i), 0))
    return pl.pallas_call(
        _ple_kernel,
        grid=(n_tiles + 1, row_tiles),
        in_specs=[rows(k), _chunk_spec(k // row_tiles, bn, n_tiles),
                  tile, tile, tile, rows(LANES),
                  pl.BlockSpec((1, bn), lambda j, i: (0, _prev_tile(j)))],
        out_specs=tile,
        out_shape=jax.ShapeDtypeStruct((m, n), F32),
        scratch_shapes=[pltpu.VMEM((k, bn), BF16), pltpu.VMEM((k, bn), BF16)],
        compiler_params=_params(("arbitrary", "arbitrary")),
        name="ple_gate",
    )(hn, w, x, t, pn, rstd_t, g_post)


def _layer(x, p_i, w_in, conv_w, conv_b, w_rg_a, b_rg_a, w_rg_i, b_rg_i, lru_lambda,
           rel_bias, w_proj_a, w_proj_b, w_out, g_pre, g_post,
           w_ple, w_ple_gate, g_ple_pre, g_ple_post):
    d = x.shape[1]
    lru_width = w_proj_a.shape[0]
    att_width = w_proj_b.shape[0]
    assert lru_width == d and 2 * att_width == d
    row = lambda v: v.reshape(1, -1)

    xn = _rmsnorm(x, row(g_pre))
    q_start = 2 * lru_width
    col_scale = jnp.ones((1, w_in.shape[1]), F32).at[:, q_start:q_start + att_width].set(
        ATT_HEAD_DIM ** -0.5 * LOG2_E)
    proj = _matmul(xn, w_in, "in_proj", col_scale=col_scale)
    y_a = _lru_branch(proj, conv_w, row(conv_b), w_rg_a.astype(BF16), w_rg_i.astype(BF16),
                      row(b_rg_a), row(b_rg_i), row(lru_lambda), lru_width)
    y_b = _attn_branch(proj, rel_bias, att_width, col0=2 * lru_width // att_width)
    merged = _merge(y_a, y_b, w_proj_a, w_proj_b, proj, gate_start=2 * lru_width + 4 * att_width)
    t = _matmul(merged, w_out, "out_proj")
    hn, pn, rstd_t = _rowwise(x, t, p_i, w_ple.astype(BF16), row(g_post), row(g_ple_pre), row(g_ple_post))
    return _ple_gate(hn, w_ple_gate, x, t, pn, rstd_t, row(g_post))


def kernel(x, p, w_in, conv_w, conv_b, w_rg_a, b_rg_a, w_rg_i, b_rg_i, lru_lambda, rel_bias,
           w_proj_a, w_proj_b, w_out, g_pre, g_post, w_ple, w_ple_gate, g_ple_pre, g_ple_post):
    batch = x.shape[0]
    outs = []
    for b in range(batch):
        h = x[b]
        for l in range(w_in.shape[0]):
            h = _layer(h, p[l, b], w_in[l], conv_w[l], conv_b[l], w_rg_a[l], b_rg_a[l],
                       w_rg_i[l], b_rg_i[l], lru_lambda[l], rel_bias[l], w_proj_a[l],
                       w_proj_b[l], w_out[l], g_pre[l], g_post[l], w_ple[l],
                       w_ple_gate[l], g_ple_pre[l], g_ple_post[l])
        outs.append(h)
    return jnp.stack(outs, axis=0)
```

```python
import functools
import math

import jax
import jax.numpy as jnp
from jax import lax
from jax.experimental import pallas as pl
from jax.experimental.pallas import tpu as pltpu

F32 = jnp.float32
BF16 = jnp.bfloat16

EPS = 1e-6
NEG_INF = -1e30
LRU_C = 8.0
LOG2_E = math.log2(math.e)

CHUNK = 64
CTX_CHUNKS = 8
REL_CLIP = 128
ATT_HEAD_DIM = 128
LRU_BLOCK_W = 256
CONV_W = 4

SUBLANES = 8
LANES = 128
ATT_BLOCK_Q = 256
ATT_KEY_BLOCKS = 1 + (CTX_CHUNKS * CHUNK) // ATT_BLOCK_Q
ATT_HALF_Q = ATT_BLOCK_Q // 2
ATT_HALF_KEYS = ATT_HALF_Q + CTX_CHUNKS * CHUNK
VMEM_LIMIT_BYTES = 56 * 1024 * 1024


def _params(semantics):
    return pltpu.CompilerParams(dimension_semantics=semantics,
                                vmem_limit_bytes=VMEM_LIMIT_BYTES)


def _sigmoid(x):
    return 0.5 * jnp.tanh(0.5 * x) + 0.5


def _silu(x):
    hx = 0.5 * x
    return hx * jnp.tanh(hx) + hx


def _rms_norm_f32(x, g):
    ms = jnp.mean(x * x, axis=-1, keepdims=True)
    return (x * lax.rsqrt(ms + EPS)) * g


def _rmsnorm_kernel(x_ref, g_ref, o_ref):
    o_ref[...] = _rms_norm_f32(x_ref[...], g_ref[...]).astype(o_ref.dtype)


def _rmsnorm(x, g, bm=512):
    s, d = x.shape
    return pl.pallas_call(
        _rmsnorm_kernel,
        grid=(s // bm,),
        in_specs=[pl.BlockSpec((bm, d), lambda i: (i, 0)),
                  pl.BlockSpec((1, d), lambda i: (0, 0))],
        out_specs=pl.BlockSpec((bm, d), lambda i: (i, 0)),
        out_shape=jax.ShapeDtypeStruct((s, d), BF16),
        compiler_params=_params(("parallel",)),
        name="rmsnorm_pre",
    )(x, g)


def _cast_chunk(chunk_ref, w_next_ref):
    kc = chunk_ref.shape[0]
    rows = pl.ds(pl.multiple_of(pl.program_id(1) * kc, kc), kc)
    w_next_ref[rows, :] = chunk_ref[...].astype(BF16)


def _stream_weights(chunk_refs, bufs0, bufs1, o_ref, compute):
    def run(cur, nxt):
        def cast():
            for chunk_ref, w_next_ref in zip(chunk_refs, nxt):
                _cast_chunk(chunk_ref, w_next_ref)

        @pl.when(pl.program_id(0) == 0)
        def _():
            cast()

        @pl.when(pl.program_id(0) > 0)
        def _():
            cast()
            o_ref[...] = compute(cur).astype(o_ref.dtype)

    parity = lax.rem(pl.program_id(0), 2)

    @pl.when(parity == 0)
    def _():
        run(bufs1, bufs0)

    @pl.when(parity == 1)
    def _():
        run(bufs0, bufs1)


def _chunk_spec(kc, bn, n_tiles, j0=0):
    return pl.BlockSpec((kc, bn), lambda j, i: (i, j0 + jnp.minimum(j, n_tiles - 1)))


def _prev_tile(j):
    return jnp.maximum(j - 1, 0)


def _row_tile(j, i):
    return jnp.where(j > 0, i, 0)


def _matmul_kernel(a_ref, chunk_ref, *rest, scaled):
    s_ref = rest[0] if scaled else None
    o_ref, wb0_ref, wb1_ref = rest[-3:]

    def compute(w):
        acc = jnp.dot(a_ref[...], w[0][...], preferred_element_type=F32)
        return acc * s_ref[...] if scaled else acc

    _stream_weights((chunk_ref,), (wb0_ref,), (wb1_ref,), o_ref, compute)


def _matmul(a, b, name, col_scale=None, bm=1024, bn=1024):
    m, k = a.shape
    _, n = b.shape
    n_tiles, row_tiles = n // bn, m // bm
    in_specs = [pl.BlockSpec((bm, k), lambda j, i: (_row_tile(j, i), 0)),
                _chunk_spec(k // row_tiles, bn, n_tiles)]
    args = (a, b)
    if col_scale is not None:
        in_specs.append(pl.BlockSpec((1, bn), lambda j, i: (0, _prev_tile(j))))
        args += (col_scale,)
    return pl.pallas_call(
        functools.partial(_matmul_kernel, scaled=col_scale is not None),
        grid=(n_tiles + 1, row_tiles),
        in_specs=in_specs,
        out_specs=pl.BlockSpec((bm, bn), lambda j, i: (_row_tile(j, i), _prev_tile(j))),
        out_shape=jax.ShapeDtypeStruct((m, n), BF16),
        scratch_shapes=[pltpu.VMEM((k, bn), BF16), pltpu.VMEM((k, bn), BF16)],
        compiler_params=_params(("arbitrary", "arbitrary")),
        name=name,
    )(*args)


def _lru_kernel(xa_ref, za_ref, cw_ref, cb_ref, wa_ref, wi_ref, ba_ref, bi_ref,
                lam_ref, o_ref, xe_ref, hc_ref, *, bm, n_blocks):
    @pl.when(pl.program_id(0) == 0)
    def _():
        xe_ref[0:SUBLANES, :] = jnp.zeros((SUBLANES, xe_ref.shape[1]), F32)
        hc_ref[...] = jnp.zeros(hc_ref.shape, F32)

    groups = bm // SUBLANES
    sub = lax.broadcasted_iota(jnp.int32, (groups, SUBLANES, LRU_BLOCK_W), 1)

    def block(n, carry):
        sl = pl.ds(pl.multiple_of(n * LRU_BLOCK_W, LRU_BLOCK_W), LRU_BLOCK_W)
        xa = xa_ref[:, sl].astype(F32)
        xe_ref[SUBLANES:, sl] = xa
        xc = cb_ref[:, sl] + cw_ref[CONV_W - 1:CONV_W, sl] * xa
        for k in range(CONV_W - 1):
            shift = CONV_W - 1 - k
            xc = xc + cw_ref[k:k + 1, sl] * xe_ref[pl.ds(SUBLANES - shift, bm), sl]
        xe_ref[0:SUBLANES, sl] = xa[bm - SUBLANES:, :]

        xcb = xc.astype(BF16)
        r_pre = jnp.dot(xcb, wa_ref[n], preferred_element_type=F32) + ba_ref[:, sl]
        i = _sigmoid(jnp.dot(xcb, wi_ref[n], preferred_element_type=F32) + bi_ref[:, sl])
        lam = lam_ref[:, sl]
        softplus_neg_lam = jnp.maximum(-lam, 0.0) + jnp.log1p(jnp.exp(-jnp.abs(lam)))
        half_rate = (-0.5 * LRU_C) * softplus_neg_lam
        log_a = half_rate * jnp.tanh(0.5 * r_pre) + half_rate
        a = jnp.exp(log_a)
        y = -jnp.tanh(log_a) * (a * a + 1.0)
        u = jnp.where(y > 0.0, y * lax.rsqrt(y), 0.0) * (i * xc)

        a = a.reshape(groups, SUBLANES, LRU_BLOCK_W)
        u = u.reshape(groups, SUBLANES, LRU_BLOCK_W)
        d = 1
        while d < SUBLANES:
            keep = sub >= d
            a_prev = jnp.where(keep, pltpu.roll(a, d, 1), 1.0)
            u_prev = jnp.where(keep, pltpu.roll(u, d, 1), 0.0)
            u = u + a * u_prev
            a = a * a_prev
            d *= 2
        h_prev = hc_ref[0:1, sl]
        hs = []
        for g in range(groups):
            hg = a[g] * h_prev + u[g]
            hs.append(hg)
            h_prev = hg[SUBLANES - 1:SUBLANES, :]
        hc_ref[0:1, sl] = h_prev
        h = jnp.concatenate(hs, axis=0)

        z = za_ref[:, sl].astype(F32)
        o_ref[:, sl] = (h * _silu(z)).astype(o_ref.dtype)
        return carry

    lax.fori_loop(0, n_blocks, block, 0)


def _lru_branch(proj, conv_w, conv_b, w_a, w_i, b_a, b_i, lam, width, bm=512):
    s = proj.shape[0]
    n_blocks = width // LRU_BLOCK_W
    vec = lambda rows: pl.BlockSpec((rows, width), lambda i: (0, 0))
    wspec = pl.BlockSpec((n_blocks, LRU_BLOCK_W, LRU_BLOCK_W), lambda i: (0, 0, 0))
    return pl.pallas_call(
        functools.partial(_lru_kernel, bm=bm, n_blocks=n_blocks),
        grid=(s // bm,),
        in_specs=[pl.BlockSpec((bm, width), lambda i: (i, 0)),
                  pl.BlockSpec((bm, width), lambda i: (i, 1)),
                  vec(CONV_W), vec(1), wspec, wspec, vec(1), vec(1), vec(1)],
        out_specs=pl.BlockSpec((bm, width), lambda i: (i, 0)),
        out_shape=jax.ShapeDtypeStruct((s, width), BF16),
        scratch_shapes=[pltpu.VMEM((bm + SUBLANES, width), F32),
                        pltpu.VMEM((SUBLANES, width), F32)],
        compiler_params=_params(("arbitrary",)),
        name="rglru_branch",
    )(proj, proj, conv_w, conv_b, w_a, w_i, b_a, b_i, lam)


def _attn_kernel(q_ref, k0_ref, k1_ref, k2_ref, v0_ref, v1_ref, v2_ref, zb_ref, w_ref,
                 o_ref, bias_ref, s_ref, p_ref, l_ref, *, n_heads):
    bq = q_ref.shape[0]
    nk = ATT_KEY_BLOCKS * bq
    step = pl.program_id(0)

    @pl.when(step < ATT_KEY_BLOCKS)
    def _():
        qi = lax.broadcasted_iota(jnp.int32, (bq, nk), 0)
        kj = lax.broadcasted_iota(jnp.int32, (bq, nk), 1)
        q_chunk = lax.shift_right_logical(qi, CHUNK.bit_length() - 1)
        k_chunk = lax.shift_right_logical(kj, CHUNK.bit_length() - 1)
        visible = ((k_chunk >= q_chunk) & (k_chunk <= q_chunk + CTX_CHUNKS)
                   & (kj + step * bq >= (ATT_KEY_BLOCKS - 1) * bq))

        def build(h, carry):
            rows = jnp.broadcast_to(w_ref[h], (bq, w_ref.shape[2]))
            toeplitz = pltpu.roll(rows, 0, 1, stride=1, stride_axis=0)
            bias_ref[h] = jnp.where(visible, toeplitz[:, :nk], NEG_INF)
            return carry

        lax.fori_loop(0, n_heads, build, 0)

    def head_cols(h):
        return pl.ds(pl.multiple_of(h * ATT_HEAD_DIM, ATT_HEAD_DIM), ATT_HEAD_DIM)

    def window(half):
        return slice(half * ATT_HALF_Q, half * ATT_HALF_Q + ATT_HALF_KEYS)

    def half_rows(half):
        return slice(half * ATT_HALF_Q, (half + 1) * ATT_HALF_Q)

    def scores(h, carry):
        hs = head_cols(h)
        kh = jnp.concatenate([k0_ref[:, hs], k1_ref[:, hs], k2_ref[:, hs]], axis=0)
        s = lax.dot_general(q_ref[:, hs], kh, (((1,), (1,)), ((), ())),
                            preferred_element_type=F32)
        for half in range(2):
            s_ref[h, half] = s[half_rows(half), window(half)] + bias_ref[h, half_rows(half), window(half)]
        return carry

    def numerators(h, carry):
        for half in range(2):
            s = s_ref[h, half]
            m = jnp.max(s, axis=-1, keepdims=True)
            p = jnp.exp2(s - m)
            l_ref[h, half_rows(half), :] = jnp.broadcast_to(jnp.sum(p, axis=-1, keepdims=True),
                                                            (ATT_HALF_Q, l_ref.shape[2]))
            p_ref[h, half] = p.astype(BF16)
        return carry

    def values(h, carry):
        hs = head_cols(h)
        vh = jnp.concatenate([v0_ref[:, hs], v1_ref[:, hs], v2_ref[:, hs]], axis=0)
        o = jnp.concatenate([jnp.dot(p_ref[h, half], vh[window(half)], preferred_element_type=F32)
                             for half in range(2)], axis=0) / l_ref[h]
        z = zb_ref[:, hs].astype(F32)
        o_ref[:, hs] = (o * _silu(z)).astype(o_ref.dtype)
        return carry

    lax.fori_loop(0, n_heads, scores, 0, unroll=n_heads)
    lax.fori_loop(0, n_heads, numerators, 0, unroll=n_heads)
    lax.fori_loop(0, n_heads, values, 0, unroll=n_heads)


def _bias_by_offset(rel_bias):
    n_heads = rel_bias.shape[0]
    bq = ATT_BLOCK_Q
    nk = ATT_KEY_BLOCKS * bq
    off = nk - bq
    length = nk + bq
    n_far = off - REL_CLIP + 1
    n_near = nk - n_far - (2 * REL_CLIP - 1)
    far = rel_bias[:, 2 * REL_CLIP:]
    near = rel_bias[:, :1]
    w = jnp.concatenate([
        jnp.broadcast_to(far, (n_heads, n_far)),
        jnp.flip(rel_bias[:, 1:2 * REL_CLIP], axis=1),
        jnp.broadcast_to(near, (n_heads, n_near + 1)),
        jnp.broadcast_to(far, (n_heads, bq - 1)),
    ], axis=1).astype(F32)
    assert w.shape[1] == length
    return (w * LOG2_E)[:, None, :]


def _attn_branch(proj, rel_bias, att_width, col0):
    s = proj.shape[0]
    n_heads = att_width // ATT_HEAD_DIM
    bq = ATT_BLOCK_Q
    nk = ATT_KEY_BLOCKS * bq
    w = _bias_by_offset(rel_bias)

    def kv_spec(col, back):
        return pl.BlockSpec((bq, att_width), lambda i: (jnp.maximum(i - back, 0), col))

    return pl.pallas_call(
        functools.partial(_attn_kernel, n_heads=n_heads),
        grid=(s // bq,),
        in_specs=[pl.BlockSpec((bq, att_width), lambda i: (i, col0)),
                  kv_spec(col0 + 1, 2), kv_spec(col0 + 1, 1), kv_spec(col0 + 1, 0),
                  kv_spec(col0 + 2, 2), kv_spec(col0 + 2, 1), kv_spec(col0 + 2, 0),
                  pl.BlockSpec((bq, att_width), lambda i: (i, col0 + 3)),
                  pl.BlockSpec(w.shape, lambda i: (0, 0, 0))],
        out_specs=pl.BlockSpec((bq, att_width), lambda i: (i, 0)),
        out_shape=jax.ShapeDtypeStruct((s, att_width), BF16),
        scratch_shapes=[pltpu.VMEM((n_heads, bq, nk), F32),
                        pltpu.VMEM((n_heads, 2, ATT_HALF_Q, ATT_HALF_KEYS), F32),
                        pltpu.VMEM((n_heads, 2, ATT_HALF_Q, ATT_HALF_KEYS), BF16),
                        pltpu.VMEM((n_heads, bq, ATT_HEAD_DIM), F32)],
        compiler_params=_params(("arbitrary",)),
        name="chunk_attention",
    )(proj, proj, proj, proj, proj, proj, proj, proj, w)


def _merge_kernel(ya_ref, yb_ref, ca_ref, cb_ref, ga_ref, gb_ref, o_ref,
                  wa0_ref, wb0_ref, wa1_ref, wb1_ref):
    def compute(w):
        pa = jnp.dot(ya_ref[...], w[0][...], preferred_element_type=F32)
        pb = jnp.dot(yb_ref[...], w[1][...], preferred_element_type=F32)
        ga = jax.nn.sigmoid(ga_ref[...].astype(F32))
        gb = jax.nn.sigmoid(gb_ref[...].astype(F32))
        return ga * pa + gb * pb

    _stream_weights((ca_ref, cb_ref), (wa0_ref, wb0_ref), (wa1_ref, wb1_ref), o_ref, compute)


def _merge(y_a, y_b, w_pa, w_pb, proj, gate_start, bm=512, bn=1024):
    m, ka = y_a.shape
    kb = y_b.shape[1]
    n = w_pa.shape[1]
    n_tiles, row_tiles = n // bn, m // bm
    gate_col0 = gate_start // bn
    return pl.pallas_call(
        _merge_kernel,
        grid=(n_tiles + 1, row_tiles),
        in_specs=[pl.BlockSpec((bm, ka), lambda j, i: (_row_tile(j, i), 0)),
                  pl.BlockSpec((bm, kb), lambda j, i: (_row_tile(j, i), 0)),
                  _chunk_spec(ka // row_tiles, bn, n_tiles),
                  _chunk_spec(kb // row_tiles, bn, n_tiles),
                  pl.BlockSpec((bm, bn), lambda j, i: (_row_tile(j, i), gate_col0 + _prev_tile(j))),
                  pl.BlockSpec((bm, bn),
                               lambda j, i: (_row_tile(j, i), gate_col0 + n_tiles + _prev_tile(j)))],
        out_specs=pl.BlockSpec((bm, bn), lambda j, i: (_row_tile(j, i), _prev_tile(j))),
        out_shape=jax.ShapeDtypeStruct((m, n), BF16),
        scratch_shapes=[pltpu.VMEM((ka, bn), BF16), pltpu.VMEM((kb, bn), BF16),
                        pltpu.VMEM((ka, bn), BF16), pltpu.VMEM((kb, bn), BF16)],
        compiler_params=_params(("arbitrary", "arbitrary")),
        name="branch_merge",
    )(y_a, y_b, w_pa, w_pb, proj, proj)


def _rstd(v):
    return lax.rsqrt(jnp.mean(v * v, axis=-1, keepdims=True) + EPS)


def _rowwise_kernel(x_ref, t_ref, p_ref, wple_ref, gpost_ref, gpre_ref, gple_ref,
                    hn_ref, pn_ref, rt_ref):
    t = t_ref[...].astype(F32)
    rstd_t = _rstd(t)
    h = x_ref[...] + (t * rstd_t) * gpost_ref[...]
    hn_ref[...] = _rms_norm_f32(h, gpre_ref[...]).astype(hn_ref.dtype)
    pe = jnp.dot(p_ref[...].astype(BF16), wple_ref[...], preferred_element_type=F32)
    pn_ref[...] = _rms_norm_f32(pe, gple_ref[...]).astype(pn_ref.dtype)
    rt_ref[...] = jnp.broadcast_to(rstd_t, rt_ref.shape)


def _rowwise(x, t, p, w_ple, g_post, g_ple_pre, g_ple_post, bm=256):
    s, d = x.shape
    pd = p.shape[1]
    row = lambda w: pl.BlockSpec((bm, w), lambda i: (i, 0))
    vec = pl.BlockSpec((1, d), lambda i: (0, 0))
    return pl.pallas_call(
        _rowwise_kernel,
        grid=(s // bm,),
        in_specs=[row(d), row(d), row(pd), pl.BlockSpec((pd, d), lambda i: (0, 0)), vec, vec, vec],
        out_specs=[row(d), row(d), row(LANES)],
        out_shape=[jax.ShapeDtypeStruct((s, d), BF16),
                   jax.ShapeDtypeStruct((s, d), BF16),
                   jax.ShapeDtypeStruct((s, LANES), F32)],
        compiler_params=_params(("parallel",)),
        name="residual_norms",
    )(x, t, p, w_ple, g_post, g_ple_pre, g_ple_post)


def _ple_kernel(hn_ref, chunk_ref, x_ref, t_ref, pn_ref, rt_ref, gpost_ref, o_ref, wb0_ref, wb1_ref):
    def compute(w):
        g = jnp.dot(hn_ref[...], w[0][...], preferred_element_type=F32)
        h = x_ref[...] + (t_ref[...].astype(F32) * rt_ref[:, 0:1]) * gpost_ref[...]
        return h + pn_ref[...].astype(F32) * jax.nn.sigmoid(g)

    _stream_weights((chunk_ref,), (wb0_ref,), (wb1_ref,), o_ref, compute)


def _ple_gate(hn, w, x, t, pn, rstd_t, g_post, bm=512, bn=1024):
    m, k = hn.shape
    n = w.shape[1]
    n_tiles, row_tiles = n // bn, m // bm
    tile = pl.BlockSpec((bm, bn), lambda j, i: (_row_tile(j, i), _prev_tile(j)))
    rows = lambda width: pl.BlockSpec((bm, width), lambda j, i: (_row_tile(j, i), 0))
    return pl.pallas_call(
        _ple_kernel,
        grid=(n_tiles + 1, row_tiles),
        in_specs=[rows(k), _chunk_spec(k // row_tiles, bn, n_tiles),
                  tile, tile, tile, rows(LANES),
                  pl.BlockSpec((1, bn), lambda j, i: (0, _prev_tile(j)))],
        out_specs=tile,
        out_shape=jax.ShapeDtypeStruct((m, n), F32),
        scratch_shapes=[pltpu.VMEM((k, bn), BF16), pltpu.VMEM((k, bn), BF16)],
        compiler_params=_params(("arbitrary", "arbitrary")),
        name="ple_gate",
    )(hn, w, x, t, pn, rstd_t, g_post)


def _layer(x, p_i, w_in, conv_w, conv_b, w_rg_a, b_rg_a, w_rg_i, b_rg_i, lru_lambda,
           rel_bias, w_proj_a, w_proj_b, w_out, g_pre, g_post,
           w_ple, w_ple_gate, g_ple_pre, g_ple_post):
    d = x.shape[1]
    lru_width = w_proj_a.shape[0]
    att_width = w_proj_b.shape[0]
    assert lru_width == d and 2 * att_width == d
    row = lambda v: v.reshape(1, -1)

    xn = _rmsnorm(x, row(g_pre))
    q_start = 2 * lru_width
    col_scale = jnp.ones((1, w_in.shape[1]), F32).at[:, q_start:q_start + att_width].set(
        ATT_HEAD_DIM ** -0.5 * LOG2_E)
    proj = _matmul(xn, w_in, "in_proj", col_scale=col_scale)
    y_a = _lru_branch(proj, conv_w, row(conv_b), w_rg_a.astype(BF16), w_rg_i.astype(BF16),
                      row(b_rg_a), row(b_rg_i), row(lru_lambda), lru_width)
    y_b = _attn_branch(proj, rel_bias, att_width, col0=2 * lru_width // att_width)
    merged = _merge(y_a, y_b, w_proj_a, w_proj_b, proj, gate_start=2 * lru_width + 4 * att_width)
    t = _matmul(merged, w_out, "out_proj")
    hn, pn, rstd_t = _rowwise(x, t, p_i, w_ple.astype(BF16), row(g_post), row(g_ple_pre), row(g_ple_post))
    return _ple_gate(hn, w_ple_gate, x, t, pn, rstd_t, row(g_post))


def kernel(x, p, w_in, conv_w, conv_b, w_rg_a, b_rg_a, w_rg_i, b_rg_i, lru_lambda, rel_bias,
           w_proj_a, w_proj_b, w_out, g_pre, g_post, w_ple, w_ple_gate, g_ple_pre, g_ple_post):
    batch = x.shape[0]
    outs = []
    for b in range(batch):
        h = x[b]
        for l in range(w_in.shape[0]):
            h = _layer(h, p[l, b], w_in[l], conv_w[l], conv_b[l], w_rg_a[l], b_rg_a[l],
                       w_rg_i[l], b_rg_i[l], lru_lambda[l], rel_bias[l], w_proj_a[l],
                       w_proj_b[l], w_out[l], g_pre[l], g_post[l], w_ple[l],
                       w_ple_gate[l], g_ple_pre[l], g_ple_post[l])
        outs.append(h)
    return jnp.stack(outs, axis=0)
```

```python
import functools
import math

import jax
import jax.numpy as jnp
from jax import lax
from jax.experimental import pallas as pl
from jax.experimental.pallas import tpu as pltpu

F32 = jnp.float32
BF16 = jnp.bfloat16

EPS = 1e-6
NEG_INF = -1e30
LRU_C = 8.0
LOG2_E = math.log2(math.e)

CHUNK = 64
CTX_CHUNKS = 8
REL_CLIP = 128
ATT_HEAD_DIM = 128
LRU_BLOCK_W = 256
CONV_W = 4

SUBLANES = 8
LANES = 128
ATT_BLOCK_Q = 256
ATT_KEY_BLOCKS = 1 + (CTX_CHUNKS * CHUNK) // ATT_BLOCK_Q
ATT_HALF_Q = ATT_BLOCK_Q // 2
ATT_HALF_KEYS = ATT_HALF_Q + CTX_CHUNKS * CHUNK
VMEM_LIMIT_BYTES = 56 * 1024 * 1024


def _params(semantics):
    return pltpu.CompilerParams(dimension_semantics=semantics,
                                vmem_limit_bytes=VMEM_LIMIT_BYTES)


def _sigmoid(x):
    return 0.5 * jnp.tanh(0.5 * x) + 0.5


def _silu(x):
    hx = 0.5 * x
    return hx * jnp.tanh(hx) + hx


def _rms_norm_f32(x, g):
    ms = jnp.mean(x * x, axis=-1, keepdims=True)
    return (x * lax.rsqrt(ms + EPS)) * g


def _rmsnorm_kernel(x_ref, g_ref, o_ref):
    o_ref[...] = _rms_norm_f32(x_ref[...], g_ref[...]).astype(o_ref.dtype)


def _rmsnorm(x, g, bm=512):
    s, d = x.shape
    return pl.pallas_call(
        _rmsnorm_kernel,
        grid=(s // bm,),
        in_specs=[pl.BlockSpec((bm, d), lambda i: (i, 0)),
                  pl.BlockSpec((1, d), lambda i: (0, 0))],
        out_specs=pl.BlockSpec((bm, d), lambda i: (i, 0)),
        out_shape=jax.ShapeDtypeStruct((s, d), BF16),
        compiler_params=_params(("parallel",)),
        name="rmsnorm_pre",
    )(x, g)


def _cast_chunk(chunk_ref, w_next_ref):
    kc = chunk_ref.shape[0]
    rows = pl.ds(pl.multiple_of(pl.program_id(1) * kc, kc), kc)
    w_next_ref[rows, :] = chunk_ref[...].astype(BF16)


def _stream_weights(chunk_refs, bufs0, bufs1, o_ref, compute):
    def run(cur, nxt):
        def cast():
            for chunk_ref, w_next_ref in zip(chunk_refs, nxt):
                _cast_chunk(chunk_ref, w_next_ref)

        @pl.when(pl.program_id(0) == 0)
        def _():
            cast()

        @pl.when(pl.program_id(0) > 0)
        def _():
            cast()
            o_ref[...] = compute(cur).astype(o_ref.dtype)

    parity = lax.rem(pl.program_id(0), 2)

    @pl.when(parity == 0)
    def _():
        run(bufs1, bufs0)

    @pl.when(parity == 1)
    def _():
        run(bufs0, bufs1)


def _chunk_spec(kc, bn, n_tiles, j0=0):
    return pl.BlockSpec((kc, bn), lambda j, i: (i, j0 + jnp.minimum(j, n_tiles - 1)))


def _prev_tile(j):
    return jnp.maximum(j - 1, 0)


def _row_tile(j, i):
    return jnp.where(j > 0, i, 0)


def _matmul_kernel(a_ref, chunk_ref, *rest, scaled):
    s_ref = rest[0] if scaled else None
    o_ref, wb0_ref, wb1_ref = rest[-3:]

    def compute(w):
        acc = jnp.dot(a_ref[...], w[0][...], preferred_element_type=F32)
        return acc * s_ref[...] if scaled else acc

    _stream_weights((chunk_ref,), (wb0_ref,), (wb1_ref,), o_ref, compute)


def _matmul(a, b, name, col_scale=None, bm=1024, bn=1024):
    m, k = a.shape
    _, n = b.shape
    n_tiles, row_tiles = n // bn, m // bm
    in_specs = [pl.BlockSpec((bm, k), lambda j, i: (_row_tile(j, i), 0)),
                _chunk_spec(k // row_tiles, bn, n_tiles)]
    args = (a, b)
    if col_scale is not None:
        in_specs.append(pl.BlockSpec((1, bn), lambda j, i: (0, _prev_tile(j))))
        args += (col_scale,)
    return pl.pallas_call(
        functools.partial(_matmul_kernel, scaled=col_scale is not None),
        grid=(n_tiles + 1, row_tiles),
        in_specs=in_specs,
        out_specs=pl.BlockSpec((bm, bn), lambda j, i: (_row_tile(j, i), _prev_tile(j))),
        out_shape=jax.ShapeDtypeStruct((m, n), BF16),
        scratch_shapes=[pltpu.VMEM((k, bn), BF16), pltpu.VMEM((k, bn), BF16)],
        compiler_params=_params(("arbitrary", "arbitrary")),
        name=name,
    )(*args)


def _lru_kernel(xa_ref, za_ref, cw_ref, cb_ref, wa_ref, wi_ref, ba_ref, bi_ref,
                lam_ref, o_ref, halo_ref, hc_ref, xs_ref, zs_ref, ys_ref, *, bm, n_blocks):
    seg = bm // SUBLANES
    pitch = xs_ref.shape[1] // SUBLANES
    lane_tiles = LRU_BLOCK_W // LANES

    @pl.when(pl.program_id(0) == 0)
    def _():
        halo_ref[...] = jnp.zeros(halo_ref.shape, F32)
        hc_ref[...] = jnp.zeros(hc_ref.shape, F32)

    row8 = lax.broadcasted_iota(jnp.int32, (SUBLANES, LRU_BLOCK_W), 0)

    def to_groups(stage_ref, value):
        for lt in range(lane_tiles):
            for sgm in range(SUBLANES):
                stage_ref[lt, sgm * pitch:sgm * pitch + seg, :] = (
                    value[sgm * seg:(sgm + 1) * seg, lt * LANES:(lt + 1) * LANES])
        return jnp.concatenate(
            [jnp.concatenate([stage_ref[lt, pl.ds(j, SUBLANES, stride=pitch), :]
                              for lt in range(lane_tiles)], axis=1) for j in range(seg)], axis=0)

    def block(n, carry):
        sl = pl.ds(pl.multiple_of(n * LRU_BLOCK_W, LRU_BLOCK_W), LRU_BLOCK_W)
        xp = to_groups(xs_ref, xa_ref[:, sl].astype(F32))
        zp = to_groups(zs_ref, za_ref[:, sl].astype(F32))

        halo = halo_ref[:, sl]
        before = []
        for m in range(CONV_W - 1, 0, -1):
            last = xp[(seg - m) * SUBLANES:(seg - m + 1) * SUBLANES]
            before.append(jnp.where(row8 == 0, halo[SUBLANES - m:SUBLANES - m + 1], pltpu.roll(last, 1, 0)))
            halo_ref[SUBLANES - m:SUBLANES - m + 1, sl] = last[SUBLANES - 1:SUBLANES]
        xc = cb_ref[:, sl] + cw_ref[CONV_W - 1:CONV_W, sl] * xp
        for k in range(CONV_W - 1):
            shift = CONV_W - 1 - k
            shifted = jnp.concatenate(before[CONV_W - 1 - shift:] + [xp[:(seg - shift) * SUBLANES]], axis=0)
            xc = xc + cw_ref[k:k + 1, sl] * shifted

        xcb = xc.astype(BF16)
        r_pre = jnp.dot(xcb, wa_ref[n], preferred_element_type=F32) + ba_ref[:, sl]
        i = _sigmoid(jnp.dot(xcb, wi_ref[n], preferred_element_type=F32) + bi_ref[:, sl])
        lam = lam_ref[:, sl]
        softplus_neg_lam = jnp.maximum(-lam, 0.0) + jnp.log1p(jnp.exp(-jnp.abs(lam)))
        half_rate = (-0.5 * LRU_C) * softplus_neg_lam
        log_a = half_rate * jnp.tanh(0.5 * r_pre) + half_rate
        a = jnp.exp(log_a)
        y = -jnp.tanh(log_a) * (a * a + 1.0)
        u = jnp.where(y > 0.0, y * lax.rsqrt(y), 0.0) * (i * xc)

        a = a.reshape(seg, SUBLANES, LRU_BLOCK_W)
        u = u.reshape(seg, SUBLANES, LRU_BLOCK_W)
        h_loc, a_cum = [u[0]], [a[0]]
        for j in range(1, seg):
            h_loc.append(a[j] * h_loc[-1] + u[j])
            a_cum.append(a[j] * a_cum[-1])
        ea = jnp.where(row8 == 0, 0.0, pltpu.roll(a_cum[-1], 1, 0))
        eh = jnp.where(row8 == 0, hc_ref[0:1, sl], pltpu.roll(h_loc[-1], 1, 0))
        d = 1
        while d < SUBLANES:
            keep = row8 >= d
            ea_prev = jnp.where(keep, pltpu.roll(ea, d, 0), 1.0)
            eh_prev = jnp.where(keep, pltpu.roll(eh, d, 0), 0.0)
            eh = eh + ea * eh_prev
            ea = ea * ea_prev
            d *= 2
        h = jnp.concatenate([h_loc[j] + a_cum[j] * eh for j in range(seg)], axis=0)
        hc_ref[0:1, sl] = h[bm - 1:bm, :]

        yp = h * _silu(zp)
        for lt in range(lane_tiles):
            for j in range(seg):
                ys_ref[lt, pl.ds(j, SUBLANES, stride=pitch), :] = (
                    yp[j * SUBLANES:(j + 1) * SUBLANES, lt * LANES:(lt + 1) * LANES])
        out = jnp.concatenate(
            [jnp.concatenate([ys_ref[lt, sgm * pitch:sgm * pitch + seg, :] for lt in range(lane_tiles)], axis=1)
             for sgm in range(SUBLANES)], axis=0)
        o_ref[:, sl] = out.astype(o_ref.dtype)
        return carry

    lax.fori_loop(0, n_blocks, block, 0)


def _lru_branch(proj, conv_w, conv_b, w_a, w_i, b_a, b_i, lam, width, bm=512):
    s = proj.shape[0]
    n_blocks = width // LRU_BLOCK_W
    vec = lambda rows: pl.BlockSpec((rows, width), lambda i: (0, 0))
    wspec = pl.BlockSpec((n_blocks, LRU_BLOCK_W, LRU_BLOCK_W), lambda i: (0, 0, 0))
    return pl.pallas_call(
        functools.partial(_lru_kernel, bm=bm, n_blocks=n_blocks),
        grid=(s // bm,),
        in_specs=[pl.BlockSpec((bm, width), lambda i: (i, 0)),
                  pl.BlockSpec((bm, width), lambda i: (i, 1)),
                  vec(CONV_W), vec(1), wspec, wspec, vec(1), vec(1), vec(1)],
        out_specs=pl.BlockSpec((bm, width), lambda i: (i, 0)),
        out_shape=jax.ShapeDtypeStruct((s, width), BF16),
        scratch_shapes=[pltpu.VMEM((SUBLANES, width), F32),
                        pltpu.VMEM((SUBLANES, width), F32)]
        + [pltpu.VMEM((LRU_BLOCK_W // LANES, bm + SUBLANES * SUBLANES, LANES), F32)] * 3,
        compiler_params=_params(("arbitrary",)),
        name="rglru_branch",
    )(proj, proj, conv_w, conv_b, w_a, w_i, b_a, b_i, lam)


def _attn_kernel(q_ref, k0_ref, k1_ref, k2_ref, v0_ref, v1_ref, v2_ref, zb_ref, w_ref,
                 o_ref, bias_ref, s_ref, p_ref, l_ref, *, n_heads):
    bq = q_ref.shape[0]
    nk = ATT_KEY_BLOCKS * bq
    step = pl.program_id(0)

    @pl.when(step < ATT_KEY_BLOCKS)
    def _():
        qi = lax.broadcasted_iota(jnp.int32, (bq, nk), 0)
        kj = lax.broadcasted_iota(jnp.int32, (bq, nk), 1)
        q_chunk = lax.shift_right_logical(qi, CHUNK.bit_length() - 1)
        k_chunk = lax.shift_right_logical(kj, CHUNK.bit_length() - 1)
        visible = ((k_chunk >= q_chunk) & (k_chunk <= q_chunk + CTX_CHUNKS)
                   & (kj + step * bq >= (ATT_KEY_BLOCKS - 1) * bq))

        def build(h, carry):
            rows = jnp.broadcast_to(w_ref[h], (bq, w_ref.shape[2]))
            toeplitz = pltpu.roll(rows, 0, 1, stride=1, stride_axis=0)
            bias_ref[h] = jnp.where(visible, toeplitz[:, :nk], NEG_INF)
            return carry

        lax.fori_loop(0, n_heads, build, 0)

    def head_cols(h):
        return pl.ds(pl.multiple_of(h * ATT_HEAD_DIM, ATT_HEAD_DIM), ATT_HEAD_DIM)

    def window(half):
        return slice(half * ATT_HALF_Q, half * ATT_HALF_Q + ATT_HALF_KEYS)

    def half_rows(half):
        return slice(half * ATT_HALF_Q, (half + 1) * ATT_HALF_Q)

    def scores(h, carry):
        hs = head_cols(h)
        kh = jnp.concatenate([k0_ref[:, hs], k1_ref[:, hs], k2_ref[:, hs]], axis=0)
        s = lax.dot_general(q_ref[:, hs], kh, (((1,), (1,)), ((), ())),
                            preferred_element_type=F32)
        for half in range(2):
            s_ref[h, half] = s[half_rows(half), window(half)] + bias_ref[h, half_rows(half), window(half)]
        return carry

    def numerators(h, carry):
        for half in range(2):
            s = s_ref[h, half]
            m = jnp.max(s, axis=-1, keepdims=True)
            p = jnp.exp2(s - m)
            l_ref[h, half_rows(half), :] = jnp.broadcast_to(jnp.sum(p, axis=-1, keepdims=True),
                                                            (ATT_HALF_Q, l_ref.shape[2]))
            p_ref[h, half] = p.astype(BF16)
        return carry

    def values(h, carry):
        hs = head_cols(h)
        vh = jnp.concatenate([v0_ref[:, hs], v1_ref[:, hs], v2_ref[:, hs]], axis=0)
        o = jnp.concatenate([jnp.dot(p_ref[h, half], vh[window(half)], preferred_element_type=F32)
                             for half in range(2)], axis=0) / l_ref[h]
        z = zb_ref[:, hs].astype(F32)
        o_ref[:, hs] = (o * _silu(z)).astype(o_ref.dtype)
        return carry

    lax.fori_loop(0, n_heads, scores, 0, unroll=n_heads)
    lax.fori_loop(0, n_heads, numerators, 0, unroll=n_heads)
    lax.fori_loop(0, n_heads, values, 0, unroll=n_heads)


def _bias_by_offset(rel_bias):
    n_heads = rel_bias.shape[0]
    bq = ATT_BLOCK_Q
    nk = ATT_KEY_BLOCKS * bq
    off = nk - bq
    length = nk + bq
    n_far = off - REL_CLIP + 1
    n_near = nk - n_far - (2 * REL_CLIP - 1)
    far = rel_bias[:, 2 * REL_CLIP:]
    near = rel_bias[:, :1]
    w = jnp.concatenate([
        jnp.broadcast_to(far, (n_heads, n_far)),
        jnp.flip(rel_bias[:, 1:2 * REL_CLIP], axis=1),
        jnp.broadcast_to(near, (n_heads, n_near + 1)),
        jnp.broadcast_to(far, (n_heads, bq - 1)),
    ], axis=1).astype(F32)
    assert w.shape[1] == length
    return (w * LOG2_E)[:, None, :]


def _attn_branch(proj, rel_bias, att_width, col0):
    s = proj.shape[0]
    n_heads = att_width // ATT_HEAD_DIM
    bq = ATT_BLOCK_Q
    nk = ATT_KEY_BLOCKS * bq
    w = _bias_by_offset(rel_bias)

    def kv_spec(col, back):
        return pl.BlockSpec((bq, att_width), lambda i: (jnp.maximum(i - back, 0), col))

    return pl.pallas_call(
        functools.partial(_attn_kernel, n_heads=n_heads),
        grid=(s // bq,),
        in_specs=[pl.BlockSpec((bq, att_width), lambda i: (i, col0)),
                  kv_spec(col0 + 1, 2), kv_spec(col0 + 1, 1), kv_spec(col0 + 1, 0),
                  kv_spec(col0 + 2, 2), kv_spec(col0 + 2, 1), kv_spec(col0 + 2, 0),
                  pl.BlockSpec((bq, att_width), lambda i: (i, col0 + 3)),
                  pl.BlockSpec(w.shape, lambda i: (0, 0, 0))],
        out_specs=pl.BlockSpec((bq, att_width), lambda i: (i, 0)),
        out_shape=jax.ShapeDtypeStruct((s, att_width), BF16),
        scratch_shapes=[pltpu.VMEM((n_heads, bq, nk), F32),
                        pltpu.VMEM((n_heads, 2, ATT_HALF_Q, ATT_HALF_KEYS), F32),
                        pltpu.VMEM((n_heads, 2, ATT_HALF_Q, ATT_HALF_KEYS), BF16),
                        pltpu.VMEM((n_heads, bq, ATT_HEAD_DIM), F32)],
        compiler_params=_params(("arbitrary",)),
        name="chunk_attention",
    )(proj, proj, proj, proj, proj, proj, proj, proj, w)


def _merge_kernel(ya_ref, yb_ref, ca_ref, cb_ref, ga_ref, gb_ref, o_ref,
                  wa0_ref, wb0_ref, wa1_ref, wb1_ref):
    def compute(w):
        pa = jnp.dot(ya_ref[...], w[0][...], preferred_element_type=F32)
        pb = jnp.dot(yb_ref[...], w[1][...], preferred_element_type=F32)
        ga = jax.nn.sigmoid(ga_ref[...].astype(F32))
        gb = jax.nn.sigmoid(gb_ref[...].astype(F32))
        return ga * pa + gb * pb

    _stream_weights((ca_ref, cb_ref), (wa0_ref, wb0_ref), (wa1_ref, wb1_ref), o_ref, compute)


def _merge(y_a, y_b, w_pa, w_pb, proj, gate_start, bm=512, bn=1024):
    m, ka = y_a.shape
    kb = y_b.shape[1]
    n = w_pa.shape[1]
    n_tiles, row_tiles = n // bn, m // bm
    gate_col0 = gate_start // bn
    return pl.pallas_call(
        _merge_kernel,
        grid=(n_tiles + 1, row_tiles),
        in_specs=[pl.BlockSpec((bm, ka), lambda j, i: (_row_tile(j, i), 0)),
                  pl.BlockSpec((bm, kb), lambda j, i: (_row_tile(j, i), 0)),
                  _chunk_spec(ka // row_tiles, bn, n_tiles),
                  _chunk_spec(kb // row_tiles, bn, n_tiles),
                  pl.BlockSpec((bm, bn), lambda j, i: (_row_tile(j, i), gate_col0 + _prev_tile(j))),
                  pl.BlockSpec((bm, bn),
                               lambda j, i: (_row_tile(j, i), gate_col0 + n_tiles + _prev_tile(j)))],
        out_specs=pl.BlockSpec((bm, bn), lambda j, i: (_row_tile(j, i), _prev_tile(j))),
        out_shape=jax.ShapeDtypeStruct((m, n), BF16),
        scratch_shapes=[pltpu.VMEM((ka, bn), BF16), pltpu.VMEM((kb, bn), BF16),
                        pltpu.VMEM((ka, bn), BF16), pltpu.VMEM((kb, bn), BF16)],
        compiler_params=_params(("arbitrary", "arbitrary")),
        name="branch_merge",
    )(y_a, y_b, w_pa, w_pb, proj, proj)


def _rstd(v):
    return lax.rsqrt(jnp.mean(v * v, axis=-1, keepdims=True) + EPS)


def _rowwise_kernel(x_ref, t_ref, p_ref, wple_ref, gpost_ref, gpre_ref, gple_ref,
                    hn_ref, pn_ref, rt_ref):
    t = t_ref[...].astype(F32)
    rstd_t = _rstd(t)
    h = x_ref[...] + (t * rstd_t) * gpost_ref[...]
    hn_ref[...] = _rms_norm_f32(h, gpre_ref[...]).astype(hn_ref.dtype)
    pe = jnp.dot(p_ref[...].astype(BF16), wple_ref[...], preferred_element_type=F32)
    pn_ref[...] = _rms_norm_f32(pe, gple_ref[...]).astype(pn_ref.dtype)
    rt_ref[...] = jnp.broadcast_to(rstd_t, rt_ref.shape)


def _rowwise(x, t, p, w_ple, g_post, g_ple_pre, g_ple_post, bm=256):
    s, d = x.shape
    pd = p.shape[1]
    row = lambda w: pl.BlockSpec((bm, w), lambda i: (i, 0))
    vec = pl.BlockSpec((1, d), lambda i: (0, 0))
    return pl.pallas_call(
        _rowwise_kernel,
        grid=(s // bm,),
        in_specs=[row(d), row(d), row(pd), pl.BlockSpec((pd, d), lambda i: (0, 0)), vec, vec, vec],
        out_specs=[row(d), row(d), row(LANES)],
        out_shape=[jax.ShapeDtypeStruct((s, d), BF16),
                   jax.ShapeDtypeStruct((s, d), BF16),
                   jax.ShapeDtypeStruct((s, LANES), F32)],
        compiler_params=_params(("parallel",)),
        name="residual_norms",
    )(x, t, p, w_ple, g_post, g_ple_pre, g_ple_post)


def _ple_kernel(hn_ref, chunk_ref, x_ref, t_ref, pn_ref, rt_ref, gpost_ref, o_ref, wb0_ref, wb1_ref):
    def compute(w):
        g = jnp.dot(hn_ref[...], w[0][...], preferred_element_type=F32)
        h = x_ref[...] + (t_ref[...].astype(F32) * rt_ref[:, 0:1]) * gpost_ref[...]
        return h + pn_ref[...].astype(F32) * jax.nn.sigmoid(g)

    _stream_weights((chunk_ref,), (wb0_ref,), (wb1_ref,), o_ref, compute)


def _ple_gate(hn, w, x, t, pn, rstd_t, g_post, bm=512, bn=1024):
    m, k = hn.shape
    n = w.shape[1]
    n_tiles, row_tiles = n // bn, m // bm
    tile = pl.BlockSpec((bm, bn), lambda j, i: (_row_tile(j, i), _prev_tile(j)))
    rows = lambda width: pl.BlockSpec((bm, width), lambda j, i: (_row_tile(j, i), 0))
    return pl.pallas_call(
        _ple_kernel,
        grid=(n_tiles + 1, row_tiles),
        in_specs=[rows(k), _chunk_spec(k // row_tiles, bn, n_tiles),
                  tile, tile, tile, rows(LANES),
                  pl.BlockSpec((1, bn), lambda j, i: (0, _prev_tile(j)))],
        out_specs=tile,
        out_shape=jax.ShapeDtypeStruct((m, n), F32),
        scratch_shapes=[pltpu.VMEM((k, bn), BF16), pltpu.VMEM((k, bn), BF16)],
        compiler_params=_params(("arbitrary", "arbitrary")),
        name="ple_gate",
    )(hn, w, x, t, pn, rstd_t, g_post)


def _layer(x, p_i, w_in, conv_w, conv_b, w_rg_a, b_rg_a, w_rg_i, b_rg_i, lru_lambda,
           rel_bias, w_proj_a, w_proj_b, w_out, g_pre, g_post,
           w_ple, w_ple_gate, g_ple_pre, g_ple_post):
    d = x.shape[1]
    lru_width = w_proj_a.shape[0]
    att_width = w_proj_b.shape[0]
    assert lru_width == d and 2 * att_width == d
    row = lambda v: v.reshape(1, -1)

    xn = _rmsnorm(x, row(g_pre))
    q_start = 2 * lru_width
    col_scale = jnp.ones((1, w_in.shape[1]), F32).at[:, q_start:q_start + att_width].set(
        ATT_HEAD_DIM ** -0.5 * LOG2_E)
    proj = _matmul(xn, w_in, "in_proj", col_scale=col_scale)
    y_a = _lru_branch(proj, conv_w, row(conv_b), w_rg_a.astype(BF16), w_rg_i.astype(BF16),
                      row(b_rg_a), row(b_rg_i), row(lru_lambda), lru_width)
    y_b = _attn_branch(proj, rel_bias, att_width, col0=2 * lru_width // att_width)
    merged = _merge(y_a, y_b, w_proj_a, w_proj_b, proj, gate_start=2 * lru_width + 4 * att_width)
    t = _matmul(merged, w_out, "out_proj")
    hn, pn, rstd_t = _rowwise(x, t, p_i, w_ple.astype(BF16), row(g_post), row(g_ple_pre), row(g_ple_post))
    return _ple_gate(hn, w_ple_gate, x, t, pn, rstd_t, row(g_post))


def kernel(x, p, w_in, conv_w, conv_b, w_rg_a, b_rg_a, w_rg_i, b_rg_i, lru_lambda, rel_bias,
           w_proj_a, w_proj_b, w_out, g_pre, g_post, w_ple, w_ple_gate, g_ple_pre, g_ple_post):
    batch = x.shape[0]
    outs = []
    for b in range(batch):
        h = x[b]
        for l in range(w_in.shape[0]):
            h = _layer(h, p[l, b], w_in[l], conv_w[l], conv_b[l], w_rg_a[l], b_rg_a[l],
                       w_rg_i[l], b_rg_i[l], lru_lambda[l], rel_bias[l], w_proj_a[l],
                       w_proj_b[l], w_out[l], g_pre[l], g_post[l], w_ple[l],
                       w_ple_gate[l], g_ple_pre[l], g_ple_post[l])
        outs.append(h)
    return jnp.stack(outs, axis=0)
```

```python
import functools
import math

import jax
import jax.numpy as jnp
from jax import lax
from jax.experimental import pallas as pl
from jax.experimental.pallas import tpu as pltpu

F32 = jnp.float32
BF16 = jnp.bfloat16

EPS = 1e-6
NEG_INF = -1e30
LRU_C = 8.0
LOG2_E = math.log2(math.e)

CHUNK = 64
CTX_CHUNKS = 8
REL_CLIP = 128
ATT_HEAD_DIM = 128
LRU_BLOCK_W = 256
CONV_W = 4

SUBLANES = 8
LANES = 128
ATT_BLOCK_Q = 256
ATT_KEY_BLOCKS = 1 + (CTX_CHUNKS * CHUNK) // ATT_BLOCK_Q
ATT_HALF_Q = ATT_BLOCK_Q // 2
ATT_HALF_KEYS = ATT_HALF_Q + CTX_CHUNKS * CHUNK
VMEM_LIMIT_BYTES = 56 * 1024 * 1024


def _params(semantics):
    return pltpu.CompilerParams(dimension_semantics=semantics,
                                vmem_limit_bytes=VMEM_LIMIT_BYTES)


def _sigmoid(x):
    return 0.5 * jnp.tanh(0.5 * x) + 0.5


def _silu(x):
    hx = 0.5 * x
    return hx * jnp.tanh(hx) + hx


def _rms_norm_f32(x, g):
    ms = jnp.mean(x * x, axis=-1, keepdims=True)
    return (x * lax.rsqrt(ms + EPS)) * g


def _rmsnorm_kernel(x_ref, g_ref, o_ref):
    o_ref[...] = _rms_norm_f32(x_ref[...], g_ref[...]).astype(o_ref.dtype)


def _rmsnorm(x, g, bm=512):
    s, d = x.shape
    return pl.pallas_call(
        _rmsnorm_kernel,
        grid=(s // bm,),
        in_specs=[pl.BlockSpec((bm, d), lambda i: (i, 0)),
                  pl.BlockSpec((1, d), lambda i: (0, 0))],
        out_specs=pl.BlockSpec((bm, d), lambda i: (i, 0)),
        out_shape=jax.ShapeDtypeStruct((s, d), BF16),
        compiler_params=_params(("parallel",)),
        name="rmsnorm_pre",
    )(x, g)


def _cast_chunk(chunk_ref, w_next_ref):
    kc = chunk_ref.shape[0]
    rows = pl.ds(pl.multiple_of(pl.program_id(1) * kc, kc), kc)
    w_next_ref[rows, :] = chunk_ref[...].astype(BF16)


def _stream_weights(chunk_refs, bufs0, bufs1, o_ref, compute):
    def run(cur, nxt):
        def cast():
            for chunk_ref, w_next_ref in zip(chunk_refs, nxt):
                _cast_chunk(chunk_ref, w_next_ref)

        @pl.when(pl.program_id(0) == 0)
        def _():
            cast()

        @pl.when(pl.program_id(0) > 0)
        def _():
            cast()
            o_ref[...] = compute(cur).astype(o_ref.dtype)

    parity = lax.rem(pl.program_id(0), 2)

    @pl.when(parity == 0)
    def _():
        run(bufs1, bufs0)

    @pl.when(parity == 1)
    def _():
        run(bufs0, bufs1)


def _chunk_spec(kc, bn, n_tiles, j0=0):
    return pl.BlockSpec((kc, bn), lambda j, i: (i, j0 + jnp.minimum(j, n_tiles - 1)))


def _prev_tile(j):
    return jnp.maximum(j - 1, 0)


def _row_tile(j, i):
    return jnp.where(j > 0, i, 0)


def _matmul_kernel(a_ref, chunk_ref, *rest, scaled):
    s_ref = rest[0] if scaled else None
    o_ref, wb0_ref, wb1_ref = rest[-3:]

    def compute(w):
        acc = jnp.dot(a_ref[...], w[0][...], preferred_element_type=F32)
        return acc * s_ref[...] if scaled else acc

    _stream_weights((chunk_ref,), (wb0_ref,), (wb1_ref,), o_ref, compute)


def _matmul(a, b, name, col_scale=None, bm=1024, bn=1024):
    m, k = a.shape
    _, n = b.shape
    n_tiles, row_tiles = n // bn, m // bm
    in_specs = [pl.BlockSpec((bm, k), lambda j, i: (_row_tile(j, i), 0)),
                _chunk_spec(k // row_tiles, bn, n_tiles)]
    args = (a, b)
    if col_scale is not None:
        in_specs.append(pl.BlockSpec((1, bn), lambda j, i: (0, _prev_tile(j))))
        args += (col_scale,)
    return pl.pallas_call(
        functools.partial(_matmul_kernel, scaled=col_scale is not None),
        grid=(n_tiles + 1, row_tiles),
        in_specs=in_specs,
        out_specs=pl.BlockSpec((bm, bn), lambda j, i: (_row_tile(j, i), _prev_tile(j))),
        out_shape=jax.ShapeDtypeStruct((m, n), BF16),
        scratch_shapes=[pltpu.VMEM((k, bn), BF16), pltpu.VMEM((k, bn), BF16)],
        compiler_params=_params(("arbitrary", "arbitrary")),
        name=name,
    )(*args)


def _lru_kernel(xa_ref, za_ref, cw_ref, cb_ref, wa_ref, wi_ref, ba_ref, bi_ref,
                lam_ref, o_ref, halo_ref, hc_ref, xs_ref, zs_ref, ys_ref, *, bm, n_blocks):
    seg = bm // SUBLANES
    pitch = xs_ref.shape[1] // SUBLANES
    lane_tiles = LRU_BLOCK_W // LANES

    @pl.when(pl.program_id(0) == 0)
    def _():
        halo_ref[...] = jnp.zeros(halo_ref.shape, F32)
        hc_ref[...] = jnp.zeros(hc_ref.shape, F32)

    row8 = lax.broadcasted_iota(jnp.int32, (SUBLANES, LRU_BLOCK_W), 0)

    def to_groups(stage_ref, value):
        for lt in range(lane_tiles):
            for sgm in range(SUBLANES):
                stage_ref[lt, sgm * pitch:sgm * pitch + seg, :] = (
                    value[sgm * seg:(sgm + 1) * seg, lt * LANES:(lt + 1) * LANES])
        return jnp.concatenate(
            [jnp.concatenate([stage_ref[lt, pl.ds(j, SUBLANES, stride=pitch), :]
                              for lt in range(lane_tiles)], axis=1) for j in range(seg)], axis=0)

    def block(n, carry):
        sl = pl.ds(pl.multiple_of(n * LRU_BLOCK_W, LRU_BLOCK_W), LRU_BLOCK_W)
        xp = to_groups(xs_ref, xa_ref[:, sl].astype(F32))
        zp = to_groups(zs_ref, za_ref[:, sl].astype(F32))

        halo = halo_ref[:, sl]
        before = []
        for m in range(CONV_W - 1, 0, -1):
            last = xp[(seg - m) * SUBLANES:(seg - m + 1) * SUBLANES]
            before.append(jnp.where(row8 == 0, halo[SUBLANES - m:SUBLANES - m + 1], pltpu.roll(last, 1, 0)))
            halo_ref[SUBLANES - m:SUBLANES - m + 1, sl] = last[SUBLANES - 1:SUBLANES]
        xc = cb_ref[:, sl] + cw_ref[CONV_W - 1:CONV_W, sl] * xp
        for k in range(CONV_W - 1):
            shift = CONV_W - 1 - k
            shifted = jnp.concatenate(before[CONV_W - 1 - shift:] + [xp[:(seg - shift) * SUBLANES]], axis=0)
            xc = xc + cw_ref[k:k + 1, sl] * shifted

        xcb = xc.astype(BF16)
        r_pre = jnp.dot(xcb, wa_ref[n], preferred_element_type=F32) + ba_ref[:, sl]
        i = _sigmoid(jnp.dot(xcb, wi_ref[n], preferred_element_type=F32) + bi_ref[:, sl])
        lam = lam_ref[:, sl]
        softplus_neg_lam = jnp.maximum(-lam, 0.0) + jnp.log1p(jnp.exp(-jnp.abs(lam)))
        half_rate = (-0.5 * LRU_C) * softplus_neg_lam
        log_a = half_rate * jnp.tanh(0.5 * r_pre) + half_rate
        a = jnp.exp(log_a)
        y = -jnp.tanh(log_a) * (a * a + 1.0)
        u = jnp.where(y > 0.0, y * lax.rsqrt(y), 0.0) * (i * xc)

        a = a.reshape(seg, SUBLANES, LRU_BLOCK_W)
        u = u.reshape(seg, SUBLANES, LRU_BLOCK_W)
        h_end, a_end = u[0], a[0]
        for j in range(1, seg):
            h_end = a[j] * h_end + u[j]
            a_end = a[j] * a_end
        ea = jnp.where(row8 == 0, 0.0, pltpu.roll(a_end, 1, 0))
        eh = jnp.where(row8 == 0, hc_ref[0:1, sl], pltpu.roll(h_end, 1, 0))
        d = 1
        while d < SUBLANES:
            keep = row8 >= d
            ea_prev = jnp.where(keep, pltpu.roll(ea, d, 0), 1.0)
            eh_prev = jnp.where(keep, pltpu.roll(eh, d, 0), 0.0)
            eh = eh + ea * eh_prev
            ea = ea * ea_prev
            d *= 2
        hs = []
        state = eh
        for j in range(seg):
            state = a[j] * state + u[j]
            hs.append(state)
        h = jnp.concatenate(hs, axis=0)
        hc_ref[0:1, sl] = h[bm - 1:bm, :]

        yp = h * _silu(zp)
        for lt in range(lane_tiles):
            for j in range(seg):
                ys_ref[lt, pl.ds(j, SUBLANES, stride=pitch), :] = (
                    yp[j * SUBLANES:(j + 1) * SUBLANES, lt * LANES:(lt + 1) * LANES])
        out = jnp.concatenate(
            [jnp.concatenate([ys_ref[lt, sgm * pitch:sgm * pitch + seg, :] for lt in range(lane_tiles)], axis=1)
             for sgm in range(SUBLANES)], axis=0)
        o_ref[:, sl] = out.astype(o_ref.dtype)
        return carry

    lax.fori_loop(0, n_blocks, block, 0)


def _lru_branch(proj, conv_w, conv_b, w_a, w_i, b_a, b_i, lam, width, bm=512):
    s = proj.shape[0]
    n_blocks = width // LRU_BLOCK_W
    vec = lambda rows: pl.BlockSpec((rows, width), lambda i: (0, 0))
    wspec = pl.BlockSpec((n_blocks, LRU_BLOCK_W, LRU_BLOCK_W), lambda i: (0, 0, 0))
    return pl.pallas_call(
        functools.partial(_lru_kernel, bm=bm, n_blocks=n_blocks),
        grid=(s // bm,),
        in_specs=[pl.BlockSpec((bm, width), lambda i: (i, 0)),
                  pl.BlockSpec((bm, width), lambda i: (i, 1)),
                  vec(CONV_W), vec(1), wspec, wspec, vec(1), vec(1), vec(1)],
        out_specs=pl.BlockSpec((bm, width), lambda i: (i, 0)),
        out_shape=jax.ShapeDtypeStruct((s, width), BF16),
        scratch_shapes=[pltpu.VMEM((SUBLANES, width), F32),
                        pltpu.VMEM((SUBLANES, width), F32)]
        + [pltpu.VMEM((LRU_BLOCK_W // LANES, bm + SUBLANES * SUBLANES, LANES), F32)] * 3,
        compiler_params=_params(("arbitrary",)),
        name="rglru_branch",
    )(proj, proj, conv_w, conv_b, w_a, w_i, b_a, b_i, lam)


def _attn_kernel(q_ref, k0_ref, k1_ref, k2_ref, v0_ref, v1_ref, v2_ref, zb_ref, w_ref,
                 o_ref, bias_ref, s_ref, p_ref, l_ref, *, n_heads):
    bq = q_ref.shape[0]
    nk = ATT_KEY_BLOCKS * bq
    step = pl.program_id(0)

    @pl.when(step < ATT_KEY_BLOCKS)
    def _():
        qi = lax.broadcasted_iota(jnp.int32, (bq, nk), 0)
        kj = lax.broadcasted_iota(jnp.int32, (bq, nk), 1)
        q_chunk = lax.shift_right_logical(qi, CHUNK.bit_length() - 1)
        k_chunk = lax.shift_right_logical(kj, CHUNK.bit_length() - 1)
        visible = ((k_chunk >= q_chunk) & (k_chunk <= q_chunk + CTX_CHUNKS)
                   & (kj + step * bq >= (ATT_KEY_BLOCKS - 1) * bq))

        def build(h, carry):
            rows = jnp.broadcast_to(w_ref[h], (bq, w_ref.shape[2]))
            toeplitz = pltpu.roll(rows, 0, 1, stride=1, stride_axis=0)
            bias_ref[h] = jnp.where(visible, toeplitz[:, :nk], NEG_INF)
            return carry

        lax.fori_loop(0, n_heads, build, 0)

    def head_cols(h):
        return pl.ds(pl.multiple_of(h * ATT_HEAD_DIM, ATT_HEAD_DIM), ATT_HEAD_DIM)

    def window(half):
        return slice(half * ATT_HALF_Q, half * ATT_HALF_Q + ATT_HALF_KEYS)

    def half_rows(half):
        return slice(half * ATT_HALF_Q, (half + 1) * ATT_HALF_Q)

    def scores(h, carry):
        hs = head_cols(h)
        kh = jnp.concatenate([k0_ref[:, hs], k1_ref[:, hs], k2_ref[:, hs]], axis=0)
        s = lax.dot_general(q_ref[:, hs], kh, (((1,), (1,)), ((), ())),
                            preferred_element_type=F32)
        for half in range(2):
            s_ref[h, half] = s[half_rows(half), window(half)] + bias_ref[h, half_rows(half), window(half)]
        return carry

    def numerators(h, carry):
        for half in range(2):
            s = s_ref[h, half]
            m = jnp.max(s, axis=-1, keepdims=True)
            p = jnp.exp2(s - m)
            l_ref[h, half_rows(half), :] = jnp.broadcast_to(jnp.sum(p, axis=-1, keepdims=True),
                                                            (ATT_HALF_Q, l_ref.shape[2]))
            p_ref[h, half] = p.astype(BF16)
        return carry

    def values(h, carry):
        hs = head_cols(h)
        vh = jnp.concatenate([v0_ref[:, hs], v1_ref[:, hs], v2_ref[:, hs]], axis=0)
        o = jnp.concatenate([jnp.dot(p_ref[h, half], vh[window(half)], preferred_element_type=F32)
                             for half in range(2)], axis=0) / l_ref[h]
        z = zb_ref[:, hs].astype(F32)
        o_ref[:, hs] = (o * _silu(z)).astype(o_ref.dtype)
        return carry

    lax.fori_loop(0, n_heads, scores, 0, unroll=n_heads)
    lax.fori_loop(0, n_heads, numerators, 0, unroll=n_heads)
    lax.fori_loop(0, n_heads, values, 0, unroll=n_heads)


def _bias_by_offset(rel_bias):
    n_heads = rel_bias.shape[0]
    bq = ATT_BLOCK_Q
    nk = ATT_KEY_BLOCKS * bq
    off = nk - bq
    length = nk + bq
    n_far = off - REL_CLIP + 1
    n_near = nk - n_far - (2 * REL_CLIP - 1)
    far = rel_bias[:, 2 * REL_CLIP:]
    near = rel_bias[:, :1]
    w = jnp.concatenate([
        jnp.broadcast_to(far, (n_heads, n_far)),
        jnp.flip(rel_bias[:, 1:2 * REL_CLIP], axis=1),
        jnp.broadcast_to(near, (n_heads, n_near + 1)),
        jnp.broadcast_to(far, (n_heads, bq - 1)),
    ], axis=1).astype(F32)
    assert w.shape[1] == length
    return (w * LOG2_E)[:, None, :]


def _attn_branch(proj, rel_bias, att_width, col0):
    s = proj.shape[0]
    n_heads = att_width // ATT_HEAD_DIM
    bq = ATT_BLOCK_Q
    nk = ATT_KEY_BLOCKS * bq
    w = _bias_by_offset(rel_bias)

    def kv_spec(col, back):
        return pl.BlockSpec((bq, att_width), lambda i: (jnp.maximum(i - back, 0), col))

    return pl.pallas_call(
        functools.partial(_attn_kernel, n_heads=n_heads),
        grid=(s // bq,),
        in_specs=[pl.BlockSpec((bq, att_width), lambda i: (i, col0)),
                  kv_spec(col0 + 1, 2), kv_spec(col0 + 1, 1), kv_spec(col0 + 1, 0),
                  kv_spec(col0 + 2, 2), kv_spec(col0 + 2, 1), kv_spec(col0 + 2, 0),
                  pl.BlockSpec((bq, att_width), lambda i: (i, col0 + 3)),
                  pl.BlockSpec(w.shape, lambda i: (0, 0, 0))],
        out_specs=pl.BlockSpec((bq, att_width), lambda i: (i, 0)),
        out_shape=jax.ShapeDtypeStruct((s, att_width), BF16),
        scratch_shapes=[pltpu.VMEM((n_heads, bq, nk), F32),
                        pltpu.VMEM((n_heads, 2, ATT_HALF_Q, ATT_HALF_KEYS), F32),
                        pltpu.VMEM((n_heads, 2, ATT_HALF_Q, ATT_HALF_KEYS), BF16),
                        pltpu.VMEM((n_heads, bq, ATT_HEAD_DIM), F32)],
        compiler_params=_params(("arbitrary",)),
        name="chunk_attention",
    )(proj, proj, proj, proj, proj, proj, proj, proj, w)


def _merge_kernel(ya_ref, yb_ref, ca_ref, cb_ref, ga_ref, gb_ref, o_ref,
                  wa0_ref, wb0_ref, wa1_ref, wb1_ref):
    def compute(w):
        pa = jnp.dot(ya_ref[...], w[0][...], preferred_element_type=F32)
        pb = jnp.dot(yb_ref[...], w[1][...], preferred_element_type=F32)
        ga = jax.nn.sigmoid(ga_ref[...].astype(F32))
        gb = jax.nn.sigmoid(gb_ref[...].astype(F32))
        return ga * pa + gb * pb

    _stream_weights((ca_ref, cb_ref), (wa0_ref, wb0_ref), (wa1_ref, wb1_ref), o_ref, compute)


def _merge(y_a, y_b, w_pa, w_pb, proj, gate_start, bm=512, bn=1024):
    m, ka = y_a.shape
    kb = y_b.shape[1]
    n = w_pa.shape[1]
    n_tiles, row_tiles = n // bn, m // bm
    gate_col0 = gate_start // bn
    return pl.pallas_call(
        _merge_kernel,
        grid=(n_tiles + 1, row_tiles),
        in_specs=[pl.BlockSpec((bm, ka), lambda j, i: (_row_tile(j, i), 0)),
                  pl.BlockSpec((bm, kb), lambda j, i: (_row_tile(j, i), 0)),
                  _chunk_spec(ka // row_tiles, bn, n_tiles),
                  _chunk_spec(kb // row_tiles, bn, n_tiles),
                  pl.BlockSpec((bm, bn), lambda j, i: (_row_tile(j, i), gate_col0 + _prev_tile(j))),
                  pl.BlockSpec((bm, bn),
                               lambda j, i: (_row_tile(j, i), gate_col0 + n_tiles + _prev_tile(j)))],
        out_specs=pl.BlockSpec((bm, bn), lambda j, i: (_row_tile(j, i), _prev_tile(j))),
        out_shape=jax.ShapeDtypeStruct((m, n), BF16),
        scratch_shapes=[pltpu.VMEM((ka, bn), BF16), pltpu.VMEM((kb, bn), BF16),
                        pltpu.VMEM((ka, bn), BF16), pltpu.VMEM((kb, bn), BF16)],
        compiler_params=_params(("arbitrary", "arbitrary")),
        name="branch_merge",
    )(y_a, y_b, w_pa, w_pb, proj, proj)


def _rstd(v):
    return lax.rsqrt(jnp.mean(v * v, axis=-1, keepdims=True) + EPS)


def _rowwise_kernel(x_ref, t_ref, p_ref, wple_ref, gpost_ref, gpre_ref, gple_ref,
                    hn_ref, pn_ref, rt_ref):
    t = t_ref[...].astype(F32)
    rstd_t = _rstd(t)
    h = x_ref[...] + (t * rstd_t) * gpost_ref[...]
    hn_ref[...] = _rms_norm_f32(h, gpre_ref[...]).astype(hn_ref.dtype)
    pe = jnp.dot(p_ref[...].astype(BF16), wple_ref[...], preferred_element_type=F32)
    pn_ref[...] = _rms_norm_f32(pe, gple_ref[...]).astype(pn_ref.dtype)
    rt_ref[...] = jnp.broadcast_to(rstd_t, rt_ref.shape)


def _rowwise(x, t, p, w_ple, g_post, g_ple_pre, g_ple_post, bm=256):
    s, d = x.shape
    pd = p.shape[1]
    row = lambda w: pl.BlockSpec((bm, w), lambda i: (i, 0))
    vec = pl.BlockSpec((1, d), lambda i: (0, 0))
    return pl.pallas_call(
        _rowwise_kernel,
        grid=(s // bm,),
        in_specs=[row(d), row(d), row(pd), pl.BlockSpec((pd, d), lambda i: (0, 0)), vec, vec, vec],
        out_specs=[row(d), row(d), row(LANES)],
        out_shape=[jax.ShapeDtypeStruct((s, d), BF16),
                   jax.ShapeDtypeStruct((s, d), BF16),
                   jax.ShapeDtypeStruct((s, LANES), F32)],
        compiler_params=_params(("parallel",)),
        name="residual_norms",
    )(x, t, p, w_ple, g_post, g_ple_pre, g_ple_post)


def _ple_kernel(hn_ref, chunk_ref, x_ref, t_ref, pn_ref, rt_ref, gpost_ref, o_ref, wb0_ref, wb1_ref):
    def compute(w):
        g = jnp.dot(hn_ref[...], w[0][...], preferred_element_type=F32)
        h = x_ref[...] + (t_ref[...].astype(F32) * rt_ref[:, 0:1]) * gpost_ref[...]
        return h + pn_ref[...].astype(F32) * jax.nn.sigmoid(g)

    _stream_weights((chunk_ref,), (wb0_ref,), (wb1_ref,), o_ref, compute)


def _ple_gate(hn, w, x, t, pn, rstd_t, g_post, bm=512, bn=1024):
    m, k = hn.shape
    n = w.shape[1]
    n_tiles, row_tiles = n // bn, m // bm
    tile = pl.BlockSpec((bm, bn), lambda j, i: (_row_tile(j, i), _prev_tile(j)))
    rows = lambda width: pl.BlockSpec((bm, width), lambda j, i: (_row_tile(j, i), 0))
    return pl.pallas_call(
        _ple_kernel,
        grid=(n_tiles + 1, row_tiles),
        in_specs=[rows(k), _chunk_spec(k // row_tiles, bn, n_tiles),
                  tile, tile, tile, rows(LANES),
                  pl.BlockSpec((1, bn), lambda j, i: (0, _prev_tile(j)))],
        out_specs=tile,
        out_shape=jax.ShapeDtypeStruct((m, n), F32),
        scratch_shapes=[pltpu.VMEM((k, bn), BF16), pltpu.VMEM((k, bn), BF16)],
        compiler_params=_params(("arbitrary", "arbitrary")),
        name="ple_gate",
    )(hn, w, x, t, pn, rstd_t, g_post)


def _layer(x, p_i, w_in, conv_w, conv_b, w_rg_a, b_rg_a, w_rg_i, b_rg_i, lru_lambda,
           rel_bias, w_proj_a, w_proj_b, w_out, g_pre, g_post,
           w_ple, w_ple_gate, g_ple_pre, g_ple_post):
    d = x.shape[1]
    lru_width = w_proj_a.shape[0]
    att_width = w_proj_b.shape[0]
    assert lru_width == d and 2 * att_width == d
    row = lambda v: v.reshape(1, -1)

    xn = _rmsnorm(x, row(g_pre))
    q_start = 2 * lru_width
    col_scale = jnp.ones((1, w_in.shape[1]), F32).at[:, q_start:q_start + att_width].set(
        ATT_HEAD_DIM ** -0.5 * LOG2_E)
    proj = _matmul(xn, w_in, "in_proj", col_scale=col_scale)
    y_a = _lru_branch(proj, conv_w, row(conv_b), w_rg_a.astype(BF16), w_rg_i.astype(BF16),
                      row(b_rg_a), row(b_rg_i), row(lru_lambda), lru_width)
    y_b = _attn_branch(proj, rel_bias, att_width, col0=2 * lru_width // att_width)
    merged = _merge(y_a, y_b, w_proj_a, w_proj_b, proj, gate_start=2 * lru_width + 4 * att_width)
    t = _matmul(merged, w_out, "out_proj")
    hn, pn, rstd_t = _rowwise(x, t, p_i, w_ple.astype(BF16), row(g_post), row(g_ple_pre), row(g_ple_post))
    return _ple_gate(hn, w_ple_gate, x, t, pn, rstd_t, row(g_post))


def kernel(x, p, w_in, conv_w, conv_b, w_rg_a, b_rg_a, w_rg_i, b_rg_i, lru_lambda, rel_bias,
           w_proj_a, w_proj_b, w_out, g_pre, g_post, w_ple, w_ple_gate, g_ple_pre, g_ple_post):
    batch = x.shape[0]
    outs = []
    for b in range(batch):
        h = x[b]
        for l in range(w_in.shape[0]):
            h = _layer(h, p[l, b], w_in[l], conv_w[l], conv_b[l], w_rg_a[l], b_rg_a[l],
                       w_rg_i[l], b_rg_i[l], lru_lambda[l], rel_bias[l], w_proj_a[l],
                       w_proj_b[l], w_out[l], g_pre[l], g_post[l], w_ple[l],
                       w_ple_gate[l], g_ple_pre[l], g_ple_post[l])
        outs.append(h)
    return jnp.stack(outs, axis=0)
```

```python
import functools
import math

import jax
import jax.numpy as jnp
from jax import lax
from jax.experimental import pallas as pl
from jax.experimental.pallas import tpu as pltpu

F32 = jnp.float32
BF16 = jnp.bfloat16

EPS = 1e-6
NEG_INF = -1e30
LRU_C = 8.0
LOG2_E = math.log2(math.e)

CHUNK = 64
CTX_CHUNKS = 8
REL_CLIP = 128
ATT_HEAD_DIM = 128
LRU_BLOCK_W = 256
CONV_W = 4

SUBLANES = 8
LANES = 128
ATT_BLOCK_Q = 256
ATT_KEY_BLOCKS = 1 + (CTX_CHUNKS * CHUNK) // ATT_BLOCK_Q
ATT_HALF_Q = ATT_BLOCK_Q // 2
ATT_HALF_KEYS = ATT_HALF_Q + CTX_CHUNKS * CHUNK
VMEM_LIMIT_BYTES = 56 * 1024 * 1024


def _params(semantics):
    return pltpu.CompilerParams(dimension_semantics=semantics,
                                vmem_limit_bytes=VMEM_LIMIT_BYTES)


def _sigmoid(x):
    return 0.5 * jnp.tanh(0.5 * x) + 0.5


def _silu(x):
    hx = 0.5 * x
    return hx * jnp.tanh(hx) + hx


def _rms_norm_f32(x, g):
    ms = jnp.mean(x * x, axis=-1, keepdims=True)
    return (x * lax.rsqrt(ms + EPS)) * g


def _rmsnorm_kernel(x_ref, g_ref, o_ref):
    o_ref[...] = _rms_norm_f32(x_ref[...], g_ref[...]).astype(o_ref.dtype)


def _rmsnorm(x, g, bm=512):
    s, d = x.shape
    return pl.pallas_call(
        _rmsnorm_kernel,
        grid=(s // bm,),
        in_specs=[pl.BlockSpec((bm, d), lambda i: (i, 0)),
                  pl.BlockSpec((1, d), lambda i: (0, 0))],
        out_specs=pl.BlockSpec((bm, d), lambda i: (i, 0)),
        out_shape=jax.ShapeDtypeStruct((s, d), BF16),
        compiler_params=_params(("parallel",)),
        name="rmsnorm_pre",
    )(x, g)


def _cast_chunk(chunk_ref, w_next_ref):
    kc = chunk_ref.shape[0]
    rows = pl.ds(pl.multiple_of(pl.program_id(1) * kc, kc), kc)
    w_next_ref[rows, :] = chunk_ref[...].astype(BF16)


def _stream_weights(chunk_refs, bufs0, bufs1, o_ref, compute):
    def run(cur, nxt):
        def cast():
            for chunk_ref, w_next_ref in zip(chunk_refs, nxt):
                _cast_chunk(chunk_ref, w_next_ref)

        @pl.when(pl.program_id(0) == 0)
        def _():
            cast()

        @pl.when(pl.program_id(0) > 0)
        def _():
            cast()
            o_ref[...] = compute(cur).astype(o_ref.dtype)

    parity = lax.rem(pl.program_id(0), 2)

    @pl.when(parity == 0)
    def _():
        run(bufs1, bufs0)

    @pl.when(parity == 1)
    def _():
        run(bufs0, bufs1)


def _chunk_spec(kc, bn, n_tiles, j0=0):
    return pl.BlockSpec((kc, bn), lambda j, i: (i, j0 + jnp.minimum(j, n_tiles - 1)))


def _prev_tile(j):
    return jnp.maximum(j - 1, 0)


def _row_tile(j, i):
    return jnp.where(j > 0, i, 0)


def _matmul_kernel(a_ref, chunk_ref, *rest, scaled):
    s_ref = rest[0] if scaled else None
    o_ref, wb0_ref, wb1_ref = rest[-3:]

    def compute(w):
        acc = jnp.dot(a_ref[...], w[0][...], preferred_element_type=F32)
        return acc * s_ref[...] if scaled else acc

    _stream_weights((chunk_ref,), (wb0_ref,), (wb1_ref,), o_ref, compute)


def _matmul(a, b, name, col_scale=None, bm=1024, bn=1024):
    m, k = a.shape
    _, n = b.shape
    n_tiles, row_tiles = n // bn, m // bm
    in_specs = [pl.BlockSpec((bm, k), lambda j, i: (_row_tile(j, i), 0)),
                _chunk_spec(k // row_tiles, bn, n_tiles)]
    args = (a, b)
    if col_scale is not None:
        in_specs.append(pl.BlockSpec((1, bn), lambda j, i: (0, _prev_tile(j))))
        args += (col_scale,)
    return pl.pallas_call(
        functools.partial(_matmul_kernel, scaled=col_scale is not None),
        grid=(n_tiles + 1, row_tiles),
        in_specs=in_specs,
        out_specs=pl.BlockSpec((bm, bn), lambda j, i: (_row_tile(j, i), _prev_tile(j))),
        out_shape=jax.ShapeDtypeStruct((m, n), BF16),
        scratch_shapes=[pltpu.VMEM((k, bn), BF16), pltpu.VMEM((k, bn), BF16)],
        compiler_params=_params(("arbitrary", "arbitrary")),
        name=name,
    )(*args)


def _lru_kernel(xa_ref, za_ref, cw_ref, cb_ref, wa_ref, wi_ref, ba_ref, bi_ref,
                lam_ref, o_ref, halo_ref, hc_ref, xs_ref, ys_ref, *, bm, n_blocks):
    seg = bm // SUBLANES
    pitch = xs_ref.shape[1] // SUBLANES
    lane_tiles = LRU_BLOCK_W // LANES

    @pl.when(pl.program_id(0) == 0)
    def _():
        halo_ref[...] = jnp.zeros(halo_ref.shape, F32)
        hc_ref[...] = jnp.zeros(hc_ref.shape, F32)

    row8 = lax.broadcasted_iota(jnp.int32, (SUBLANES, LRU_BLOCK_W), 0)

    def to_groups(stage_ref, value):
        for lt in range(lane_tiles):
            for sgm in range(SUBLANES):
                stage_ref[lt, sgm * pitch:sgm * pitch + seg, :] = (
                    value[sgm * seg:(sgm + 1) * seg, lt * LANES:(lt + 1) * LANES])
        return jnp.concatenate(
            [jnp.concatenate([stage_ref[lt, pl.ds(j, SUBLANES, stride=pitch), :]
                              for lt in range(lane_tiles)], axis=1) for j in range(seg)], axis=0)

    def block(n, carry):
        sl = pl.ds(pl.multiple_of(n * LRU_BLOCK_W, LRU_BLOCK_W), LRU_BLOCK_W)
        xp = to_groups(xs_ref, xa_ref[:, sl].astype(F32))

        halo = halo_ref[:, sl]
        before = []
        for m in range(CONV_W - 1, 0, -1):
            last = xp[(seg - m) * SUBLANES:(seg - m + 1) * SUBLANES]
            before.append(jnp.where(row8 == 0, halo[SUBLANES - m:SUBLANES - m + 1], pltpu.roll(last, 1, 0)))
            halo_ref[SUBLANES - m:SUBLANES - m + 1, sl] = last[SUBLANES - 1:SUBLANES]
        xc = cb_ref[:, sl] + cw_ref[CONV_W - 1:CONV_W, sl] * xp
        for k in range(CONV_W - 1):
            shift = CONV_W - 1 - k
            shifted = jnp.concatenate(before[CONV_W - 1 - shift:] + [xp[:(seg - shift) * SUBLANES]], axis=0)
            xc = xc + cw_ref[k:k + 1, sl] * shifted

        xcb = xc.astype(BF16)
        r_pre = jnp.dot(xcb, wa_ref[n], preferred_element_type=F32) + ba_ref[:, sl]
        i = _sigmoid(jnp.dot(xcb, wi_ref[n], preferred_element_type=F32) + bi_ref[:, sl])
        lam = lam_ref[:, sl]
        softplus_neg_lam = jnp.maximum(-lam, 0.0) + jnp.log1p(jnp.exp(-jnp.abs(lam)))
        half_rate = (-0.5 * LRU_C) * softplus_neg_lam
        log_a = half_rate * jnp.tanh(0.5 * r_pre) + half_rate
        a = jnp.exp(log_a)
        y = -jnp.tanh(log_a) * (a * a + 1.0)
        u = jnp.where(y > 0.0, y * lax.rsqrt(y), 0.0) * (i * xc)

        a = a.reshape(seg, SUBLANES, LRU_BLOCK_W)
        u = u.reshape(seg, SUBLANES, LRU_BLOCK_W)
        h_end, a_end = u[0], a[0]
        for j in range(1, seg):
            h_end = a[j] * h_end + u[j]
            a_end = a[j] * a_end
        ea = jnp.where(row8 == 0, 0.0, pltpu.roll(a_end, 1, 0))
        eh = jnp.where(row8 == 0, hc_ref[0:1, sl], pltpu.roll(h_end, 1, 0))
        d = 1
        while d < SUBLANES:
            keep = row8 >= d
            ea_prev = jnp.where(keep, pltpu.roll(ea, d, 0), 1.0)
            eh_prev = jnp.where(keep, pltpu.roll(eh, d, 0), 0.0)
            eh = eh + ea * eh_prev
            ea = ea * ea_prev
            d *= 2
        hs = []
        state = eh
        for j in range(seg):
            state = a[j] * state + u[j]
            hs.append(state)
        h = jnp.concatenate(hs, axis=0)
        hc_ref[0:1, sl] = h[bm - 1:bm, :]

        for lt in range(lane_tiles):
            for j in range(seg):
                ys_ref[lt, pl.ds(j, SUBLANES, stride=pitch), :] = (
                    h[j * SUBLANES:(j + 1) * SUBLANES, lt * LANES:(lt + 1) * LANES])
        h_t = jnp.concatenate(
            [jnp.concatenate([ys_ref[lt, sgm * pitch:sgm * pitch + seg, :] for lt in range(lane_tiles)], axis=1)
             for sgm in range(SUBLANES)], axis=0)
        o_ref[:, sl] = (h_t * _silu(za_ref[:, sl].astype(F32))).astype(o_ref.dtype)
        return carry

    lax.fori_loop(0, n_blocks, block, 0)


def _lru_branch(proj, conv_w, conv_b, w_a, w_i, b_a, b_i, lam, width, bm=512):
    s = proj.shape[0]
    n_blocks = width // LRU_BLOCK_W
    vec = lambda rows: pl.BlockSpec((rows, width), lambda i: (0, 0))
    wspec = pl.BlockSpec((n_blocks, LRU_BLOCK_W, LRU_BLOCK_W), lambda i: (0, 0, 0))
    return pl.pallas_call(
        functools.partial(_lru_kernel, bm=bm, n_blocks=n_blocks),
        grid=(s // bm,),
        in_specs=[pl.BlockSpec((bm, width), lambda i: (i, 0)),
                  pl.BlockSpec((bm, width), lambda i: (i, 1)),
                  vec(CONV_W), vec(1), wspec, wspec, vec(1), vec(1), vec(1)],
        out_specs=pl.BlockSpec((bm, width), lambda i: (i, 0)),
        out_shape=jax.ShapeDtypeStruct((s, width), BF16),
        scratch_shapes=[pltpu.VMEM((SUBLANES, width), F32),
                        pltpu.VMEM((SUBLANES, width), F32)]
        + [pltpu.VMEM((LRU_BLOCK_W // LANES, bm + SUBLANES * SUBLANES, LANES), F32)] * 2,
        compiler_params=_params(("arbitrary",)),
        name="rglru_branch",
    )(proj, proj, conv_w, conv_b, w_a, w_i, b_a, b_i, lam)


def _attn_kernel(q_ref, k0_ref, k1_ref, k2_ref, v0_ref, v1_ref, v2_ref, zb_ref, w_ref,
                 o_ref, bias_ref, s_ref, p_ref, l_ref, *, n_heads):
    bq = q_ref.shape[0]
    nk = ATT_KEY_BLOCKS * bq
    step = pl.program_id(0)

    @pl.when(step < ATT_KEY_BLOCKS)
    def _():
        qi = lax.broadcasted_iota(jnp.int32, (bq, nk), 0)
        kj = lax.broadcasted_iota(jnp.int32, (bq, nk), 1)
        q_chunk = lax.shift_right_logical(qi, CHUNK.bit_length() - 1)
        k_chunk = lax.shift_right_logical(kj, CHUNK.bit_length() - 1)
        visible = ((k_chunk >= q_chunk) & (k_chunk <= q_chunk + CTX_CHUNKS)
                   & (kj + step * bq >= (ATT_KEY_BLOCKS - 1) * bq))

        def build(h, carry):
            rows = jnp.broadcast_to(w_ref[h], (bq, w_ref.shape[2]))
            toeplitz = pltpu.roll(rows, 0, 1, stride=1, stride_axis=0)
            bias_ref[h] = jnp.where(visible, toeplitz[:, :nk], NEG_INF)
            return carry

        lax.fori_loop(0, n_heads, build, 0)

    def head_cols(h):
        return pl.ds(pl.multiple_of(h * ATT_HEAD_DIM, ATT_HEAD_DIM), ATT_HEAD_DIM)

    def window(half):
        return slice(half * ATT_HALF_Q, half * ATT_HALF_Q + ATT_HALF_KEYS)

    def half_rows(half):
        return slice(half * ATT_HALF_Q, (half + 1) * ATT_HALF_Q)

    def scores(h, carry):
        hs = head_cols(h)
        kh = jnp.concatenate([k0_ref[:, hs], k1_ref[:, hs], k2_ref[:, hs]], axis=0)
        s = lax.dot_general(q_ref[:, hs], kh, (((1,), (1,)), ((), ())),
                            preferred_element_type=F32)
        for half in range(2):
            s_ref[h, half] = s[half_rows(half), window(half)] + bias_ref[h, half_rows(half), window(half)]
        return carry

    def numerators(h, carry):
        for half in range(2):
            s = s_ref[h, half]
            m = jnp.max(s, axis=-1, keepdims=True)
            p = jnp.exp2(s - m)
            l_ref[h, half_rows(half), :] = jnp.broadcast_to(jnp.sum(p, axis=-1, keepdims=True),
                                                            (ATT_HALF_Q, l_ref.shape[2]))
            p_ref[h, half] = p.astype(BF16)
        return carry

    def values(h, carry):
        hs = head_cols(h)
        vh = jnp.concatenate([v0_ref[:, hs], v1_ref[:, hs], v2_ref[:, hs]], axis=0)
        o = jnp.concatenate([jnp.dot(p_ref[h, half], vh[window(half)], preferred_element_type=F32)
                             for half in range(2)], axis=0) / l_ref[h]
        z = zb_ref[:, hs].astype(F32)
        o_ref[:, hs] = (o * _silu(z)).astype(o_ref.dtype)
        return carry

    lax.fori_loop(0, n_heads, scores, 0, unroll=n_heads)
    lax.fori_loop(0, n_heads, numerators, 0, unroll=n_heads)
    lax.fori_loop(0, n_heads, values, 0, unroll=n_heads)


def _bias_by_offset(rel_bias):
    n_heads = rel_bias.shape[0]
    bq = ATT_BLOCK_Q
    nk = ATT_KEY_BLOCKS * bq
    off = nk - bq
    length = nk + bq
    n_far = off - REL_CLIP + 1
    n_near = nk - n_far - (2 * REL_CLIP - 1)
    far = rel_bias[:, 2 * REL_CLIP:]
    near = rel_bias[:, :1]
    w = jnp.concatenate([
        jnp.broadcast_to(far, (n_heads, n_far)),
        jnp.flip(rel_bias[:, 1:2 * REL_CLIP], axis=1),
        jnp.broadcast_to(near, (n_heads, n_near + 1)),
        jnp.broadcast_to(far, (n_heads, bq - 1)),
    ], axis=1).astype(F32)
    assert w.shape[1] == length
    return (w * LOG2_E)[:, None, :]


def _attn_branch(proj, rel_bias, att_width, col0):
    s = proj.shape[0]
    n_heads = att_width // ATT_HEAD_DIM
    bq = ATT_BLOCK_Q
    nk = ATT_KEY_BLOCKS * bq
    w = _bias_by_offset(rel_bias)

    def kv_spec(col, back):
        return pl.BlockSpec((bq, att_width), lambda i: (jnp.maximum(i - back, 0), col))

    return pl.pallas_call(
        functools.partial(_attn_kernel, n_heads=n_heads),
        grid=(s // bq,),
        in_specs=[pl.BlockSpec((bq, att_width), lambda i: (i, col0)),
                  kv_spec(col0 + 1, 2), kv_spec(col0 + 1, 1), kv_spec(col0 + 1, 0),
                  kv_spec(col0 + 2, 2), kv_spec(col0 + 2, 1), kv_spec(col0 + 2, 0),
                  pl.BlockSpec((bq, att_width), lambda i: (i, col0 + 3)),
                  pl.BlockSpec(w.shape, lambda i: (0, 0, 0))],
        out_specs=pl.BlockSpec((bq, att_width), lambda i: (i, 0)),
        out_shape=jax.ShapeDtypeStruct((s, att_width), BF16),
        scratch_shapes=[pltpu.VMEM((n_heads, bq, nk), F32),
                        pltpu.VMEM((n_heads, 2, ATT_HALF_Q, ATT_HALF_KEYS), F32),
                        pltpu.VMEM((n_heads, 2, ATT_HALF_Q, ATT_HALF_KEYS), BF16),
                        pltpu.VMEM((n_heads, bq, ATT_HEAD_DIM), F32)],
        compiler_params=_params(("arbitrary",)),
        name="chunk_attention",
    )(proj, proj, proj, proj, proj, proj, proj, proj, w)


def _merge_kernel(ya_ref, yb_ref, ca_ref, cb_ref, ga_ref, gb_ref, o_ref,
                  wa0_ref, wb0_ref, wa1_ref, wb1_ref):
    def compute(w):
        pa = jnp.dot(ya_ref[...], w[0][...], preferred_element_type=F32)
        pb = jnp.dot(yb_ref[...], w[1][...], preferred_element_type=F32)
        ga = jax.nn.sigmoid(ga_ref[...].astype(F32))
        gb = jax.nn.sigmoid(gb_ref[...].astype(F32))
        return ga * pa + gb * pb

    _stream_weights((ca_ref, cb_ref), (wa0_ref, wb0_ref), (wa1_ref, wb1_ref), o_ref, compute)


def _merge(y_a, y_b, w_pa, w_pb, proj, gate_start, bm=512, bn=1024):
    m, ka = y_a.shape
    kb = y_b.shape[1]
    n = w_pa.shape[1]
    n_tiles, row_tiles = n // bn, m // bm
    gate_col0 = gate_start // bn
    return pl.pallas_call(
        _merge_kernel,
        grid=(n_tiles + 1, row_tiles),
        in_specs=[pl.BlockSpec((bm, ka), lambda j, i: (_row_tile(j, i), 0)),
                  pl.BlockSpec((bm, kb), lambda j, i: (_row_tile(j, i), 0)),
                  _chunk_spec(ka // row_tiles, bn, n_tiles),
                  _chunk_spec(kb // row_tiles, bn, n_tiles),
                  pl.BlockSpec((bm, bn), lambda j, i: (_row_tile(j, i), gate_col0 + _prev_tile(j))),
                  pl.BlockSpec((bm, bn),
                               lambda j, i: (_row_tile(j, i), gate_col0 + n_tiles + _prev_tile(j)))],
        out_specs=pl.BlockSpec((bm, bn), lambda j, i: (_row_tile(j, i), _prev_tile(j))),
        out_shape=jax.ShapeDtypeStruct((m, n), BF16),
        scratch_shapes=[pltpu.VMEM((ka, bn), BF16), pltpu.VMEM((kb, bn), BF16),
                        pltpu.VMEM((ka, bn), BF16), pltpu.VMEM((kb, bn), BF16)],
        compiler_params=_params(("arbitrary", "arbitrary")),
        name="branch_merge",
    )(y_a, y_b, w_pa, w_pb, proj, proj)


def _rstd(v):
    return lax.rsqrt(jnp.mean(v * v, axis=-1, keepdims=True) + EPS)


def _rowwise_kernel(x_ref, t_ref, p_ref, wple_ref, gpost_ref, gpre_ref, gple_ref,
                    hn_ref, pn_ref, rt_ref):
    t = t_ref[...].astype(F32)
    rstd_t = _rstd(t)
    h = x_ref[...] + (t * rstd_t) * gpost_ref[...]
    hn_ref[...] = _rms_norm_f32(h, gpre_ref[...]).astype(hn_ref.dtype)
    pe = jnp.dot(p_ref[...].astype(BF16), wple_ref[...], preferred_element_type=F32)
    pn_ref[...] = _rms_norm_f32(pe, gple_ref[...]).astype(pn_ref.dtype)
    rt_ref[...] = jnp.broadcast_to(rstd_t, rt_ref.shape)


def _rowwise(x, t, p, w_ple, g_post, g_ple_pre, g_ple_post, bm=256):
    s, d = x.shape
    pd = p.shape[1]
    row = lambda w: pl.BlockSpec((bm, w), lambda i: (i, 0))
    vec = pl.BlockSpec((1, d), lambda i: (0, 0))
    return pl.pallas_call(
        _rowwise_kernel,
        grid=(s // bm,),
        in_specs=[row(d), row(d), row(pd), pl.BlockSpec((pd, d), lambda i: (0, 0)), vec, vec, vec],
        out_specs=[row(d), row(d), row(LANES)],
        out_shape=[jax.ShapeDtypeStruct((s, d), BF16),
                   jax.ShapeDtypeStruct((s, d), BF16),
                   jax.ShapeDtypeStruct((s, LANES), F32)],
        compiler_params=_params(("parallel",)),
        name="residual_norms",
    )(x, t, p, w_ple, g_post, g_ple_pre, g_ple_post)


def _ple_kernel(hn_ref, chunk_ref, x_ref, t_ref, pn_ref, rt_ref, gpost_ref, o_ref, wb0_ref, wb1_ref):
    def compute(w):
        g = jnp.dot(hn_ref[...], w[0][...], preferred_element_type=F32)
        h = x_ref[...] + (t_ref[...].astype(F32) * rt_ref[:, 0:1]) * gpost_ref[...]
        return h + pn_ref[...].astype(F32) * jax.nn.sigmoid(g)

    _stream_weights((chunk_ref,), (wb0_ref,), (wb1_ref,), o_ref, compute)


def _ple_gate(hn, w, x, t, pn, rstd_t, g_post, bm=512, bn=1024):
    m, k = hn.shape
    n = w.shape[1]
    n_tiles, row_tiles = n // bn, m // bm
    tile = pl.BlockSpec((bm, bn), lambda j, i: (_row_tile(j, i), _prev_tile(j)))
    rows = lambda width: pl.BlockSpec((bm, width), lambda j, i: (_row_tile(j, i), 0))
    return pl.pallas_call(
        _ple_kernel,
        grid=(n_tiles + 1, row_tiles),
        in_specs=[rows(k), _chunk_spec(k // row_tiles, bn, n_tiles),
                  tile, tile, tile, rows(LANES),
                  pl.BlockSpec((1, bn), lambda j, i: (0, _prev_tile(j)))],
        out_specs=tile,
        out_shape=jax.ShapeDtypeStruct((m, n), F32),
        scratch_shapes=[pltpu.VMEM((k, bn), BF16), pltpu.VMEM((k, bn), BF16)],
        compiler_params=_params(("arbitrary", "arbitrary")),
        name="ple_gate",
    )(hn, w, x, t, pn, rstd_t, g_post)


def _layer(x, p_i, w_in, conv_w, conv_b, w_rg_a, b_rg_a, w_rg_i, b_rg_i, lru_lambda,
           rel_bias, w_proj_a, w_proj_b, w_out, g_pre, g_post,
           w_ple, w_ple_gate, g_ple_pre, g_ple_post):
    d = x.shape[1]
    lru_width = w_proj_a.shape[0]
    att_width = w_proj_b.shape[0]
    assert lru_width == d and 2 * att_width == d
    row = lambda v: v.reshape(1, -1)

    xn = _rmsnorm(x, row(g_pre))
    q_start = 2 * lru_width
    col_scale = jnp.ones((1, w_in.shape[1]), F32).at[:, q_start:q_start + att_width].set(
        ATT_HEAD_DIM ** -0.5 * LOG2_E)
    proj = _matmul(xn, w_in, "in_proj", col_scale=col_scale)
    y_a = _lru_branch(proj, conv_w, row(conv_b), w_rg_a.astype(BF16), w_rg_i.astype(BF16),
                      row(b_rg_a), row(b_rg_i), row(lru_lambda), lru_width)
    y_b = _attn_branch(proj, rel_bias, att_width, col0=2 * lru_width // att_width)
    merged = _merge(y_a, y_b, w_proj_a, w_proj_b, proj, gate_start=2 * lru_width + 4 * att_width)
    t = _matmul(merged, w_out, "out_proj")
    hn, pn, rstd_t = _rowwise(x, t, p_i, w_ple.astype(BF16), row(g_post), row(g_ple_pre), row(g_ple_post))
    return _ple_gate(hn, w_ple_gate, x, t, pn, rstd_t, row(g_post))


def kernel(x, p, w_in, conv_w, conv_b, w_rg_a, b_rg_a, w_rg_i, b_rg_i, lru_lambda, rel_bias,
           w_proj_a, w_proj_b, w_out, g_pre, g_post, w_ple, w_ple_gate, g_ple_pre, g_ple_post):
    batch = x.shape[0]
    outs = []
    for b in range(batch):
        h = x[b]
        for l in range(w_in.shape[0]):
            h = _layer(h, p[l, b], w_in[l], conv_w[l], conv_b[l], w_rg_a[l], b_rg_a[l],
                       w_rg_i[l], b_rg_i[l], lru_lambda[l], rel_bias[l], w_proj_a[l],
                       w_proj_b[l], w_out[l], g_pre[l], g_post[l], w_ple[l],
                       w_ple_gate[l], g_ple_pre[l], g_ple_post[l])
        outs.append(h)
    return jnp.stack(outs, axis=0)
```

```python
import functools
import math

import jax
import jax.numpy as jnp
from jax import lax
from jax.experimental import pallas as pl
from jax.experimental.pallas import tpu as pltpu

F32 = jnp.float32
BF16 = jnp.bfloat16

EPS = 1e-6
NEG_INF = -1e30
LRU_C = 8.0
LOG2_E = math.log2(math.e)

CHUNK = 64
CTX_CHUNKS = 8
REL_CLIP = 128
ATT_HEAD_DIM = 128
LRU_BLOCK_W = 256
CONV_W = 4

SUBLANES = 8
LANES = 128
ATT_BLOCK_Q = 256
ATT_KEY_BLOCKS = 1 + (CTX_CHUNKS * CHUNK) // ATT_BLOCK_Q
ATT_HALF_Q = ATT_BLOCK_Q // 2
ATT_HALF_KEYS = ATT_HALF_Q + CTX_CHUNKS * CHUNK
VMEM_LIMIT_BYTES = 56 * 1024 * 1024


def _params(semantics):
    return pltpu.CompilerParams(dimension_semantics=semantics,
                                vmem_limit_bytes=VMEM_LIMIT_BYTES)


def _sigmoid_of_half(hx):
    return 0.5 * jnp.tanh(hx) + 0.5


def _silu_of_half(hx):
    return hx * jnp.tanh(hx) + hx


def _rms_norm_f32(x, g):
    ms = jnp.mean(x * x, axis=-1, keepdims=True)
    return (x * lax.rsqrt(ms + EPS)) * g


def _rmsnorm_kernel(x_ref, g_ref, o_ref):
    o_ref[...] = _rms_norm_f32(x_ref[...], g_ref[...]).astype(o_ref.dtype)


def _rmsnorm(x, g, bm=512):
    s, d = x.shape
    return pl.pallas_call(
        _rmsnorm_kernel,
        grid=(s // bm,),
        in_specs=[pl.BlockSpec((bm, d), lambda i: (i, 0)),
                  pl.BlockSpec((1, d), lambda i: (0, 0))],
        out_specs=pl.BlockSpec((bm, d), lambda i: (i, 0)),
        out_shape=jax.ShapeDtypeStruct((s, d), BF16),
        compiler_params=_params(("parallel",)),
        name="rmsnorm_pre",
    )(x, g)


def _cast_chunk(chunk_ref, w_next_ref):
    kc = chunk_ref.shape[0]
    rows = pl.ds(pl.multiple_of(pl.program_id(1) * kc, kc), kc)
    w_next_ref[rows, :] = chunk_ref[...].astype(BF16)


def _stream_weights(chunk_refs, bufs0, bufs1, o_ref, compute):
    def run(cur, nxt):
        def cast():
            for chunk_ref, w_next_ref in zip(chunk_refs, nxt):
                _cast_chunk(chunk_ref, w_next_ref)

        @pl.when(pl.program_id(0) == 0)
        def _():
            cast()

        @pl.when(pl.program_id(0) > 0)
        def _():
            cast()
            o_ref[...] = compute(cur).astype(o_ref.dtype)

    parity = lax.rem(pl.program_id(0), 2)

    @pl.when(parity == 0)
    def _():
        run(bufs1, bufs0)

    @pl.when(parity == 1)
    def _():
        run(bufs0, bufs1)


def _chunk_spec(kc, bn, n_tiles, j0=0):
    return pl.BlockSpec((kc, bn), lambda j, i: (i, j0 + jnp.minimum(j, n_tiles - 1)))


def _prev_tile(j):
    return jnp.maximum(j - 1, 0)


def _row_tile(j, i):
    return jnp.where(j > 0, i, 0)


def _matmul_kernel(a_ref, chunk_ref, *rest, scaled):
    s_ref = rest[0] if scaled else None
    o_ref, wb0_ref, wb1_ref = rest[-3:]

    def compute(w):
        acc = jnp.dot(a_ref[...], w[0][...], preferred_element_type=F32)
        return acc * s_ref[...] if scaled else acc

    _stream_weights((chunk_ref,), (wb0_ref,), (wb1_ref,), o_ref, compute)


def _matmul(a, b, name, col_scale=None, bm=1024, bn=1024):
    m, k = a.shape
    _, n = b.shape
    n_tiles, row_tiles = n // bn, m // bm
    in_specs = [pl.BlockSpec((bm, k), lambda j, i: (_row_tile(j, i), 0)),
                _chunk_spec(k // row_tiles, bn, n_tiles)]
    args = (a, b)
    if col_scale is not None:
        in_specs.append(pl.BlockSpec((1, bn), lambda j, i: (0, _prev_tile(j))))
        args += (col_scale,)
    return pl.pallas_call(
        functools.partial(_matmul_kernel, scaled=col_scale is not None),
        grid=(n_tiles + 1, row_tiles),
        in_specs=in_specs,
        out_specs=pl.BlockSpec((bm, bn), lambda j, i: (_row_tile(j, i), _prev_tile(j))),
        out_shape=jax.ShapeDtypeStruct((m, n), BF16),
        scratch_shapes=[pltpu.VMEM((k, bn), BF16), pltpu.VMEM((k, bn), BF16)],
        compiler_params=_params(("arbitrary", "arbitrary")),
        name=name,
    )(*args)


def _lru_kernel(xa_ref, za_ref, cw_ref, cb_ref, wa_ref, wi_ref, ba_ref, bi_ref,
                lam_ref, o_ref, halo_ref, hc_ref, xs_ref, ys_ref, *, bm, n_blocks):
    seg = bm // SUBLANES
    pitch = xs_ref.shape[1] // SUBLANES
    lane_tiles = LRU_BLOCK_W // LANES

    @pl.when(pl.program_id(0) == 0)
    def _():
        halo_ref[...] = jnp.zeros(halo_ref.shape, F32)
        hc_ref[...] = jnp.zeros(hc_ref.shape, F32)

    row8 = lax.broadcasted_iota(jnp.int32, (SUBLANES, LRU_BLOCK_W), 0)

    def to_groups(stage_ref, value):
        for lt in range(lane_tiles):
            for sgm in range(SUBLANES):
                stage_ref[lt, sgm * pitch:sgm * pitch + seg, :] = (
                    value[sgm * seg:(sgm + 1) * seg, lt * LANES:(lt + 1) * LANES])
        return jnp.concatenate(
            [jnp.concatenate([stage_ref[lt, pl.ds(j, SUBLANES, stride=pitch), :]
                              for lt in range(lane_tiles)], axis=1) for j in range(seg)], axis=0)

    def block(n, carry):
        sl = pl.ds(pl.multiple_of(n * LRU_BLOCK_W, LRU_BLOCK_W), LRU_BLOCK_W)
        xp = to_groups(xs_ref, xa_ref[:, sl].astype(F32))

        halo = halo_ref[:, sl]
        before = []
        for m in range(CONV_W - 1, 0, -1):
            last = xp[(seg - m) * SUBLANES:(seg - m + 1) * SUBLANES]
            before.append(jnp.where(row8 == 0, halo[SUBLANES - m:SUBLANES - m + 1], pltpu.roll(last, 1, 0)))
            halo_ref[SUBLANES - m:SUBLANES - m + 1, sl] = last[SUBLANES - 1:SUBLANES]
        xc = cb_ref[:, sl] + cw_ref[CONV_W - 1:CONV_W, sl] * xp
        for k in range(CONV_W - 1):
            shift = CONV_W - 1 - k
            shifted = jnp.concatenate(before[CONV_W - 1 - shift:] + [xp[:(seg - shift) * SUBLANES]], axis=0)
            xc = xc + cw_ref[k:k + 1, sl] * shifted

        xcb = xc.astype(BF16)
        r_half = jnp.dot(xcb, wa_ref[n], preferred_element_type=F32) + ba_ref[:, sl]
        i = _sigmoid_of_half(jnp.dot(xcb, wi_ref[n], preferred_element_type=F32) + bi_ref[:, sl])
        lam = lam_ref[:, sl]
        softplus_neg_lam = jnp.maximum(-lam, 0.0) + jnp.log1p(jnp.exp(-jnp.abs(lam)))
        half_rate = (-0.5 * LRU_C) * softplus_neg_lam
        log_a = half_rate * jnp.tanh(r_half) + half_rate
        a = jnp.exp(log_a)
        y = -jnp.tanh(log_a) * (a * a + 1.0)
        u = jnp.where(y > 0.0, y * lax.rsqrt(y), 0.0) * (i * xc)

        a = a.reshape(seg, SUBLANES, LRU_BLOCK_W)
        u = u.reshape(seg, SUBLANES, LRU_BLOCK_W)
        h_end, a_end = u[0], a[0]
        for j in range(1, seg):
            h_end = a[j] * h_end + u[j]
            a_end = a[j] * a_end
        ea = jnp.where(row8 == 0, 0.0, pltpu.roll(a_end, 1, 0))
        eh = jnp.where(row8 == 0, hc_ref[0:1, sl], pltpu.roll(h_end, 1, 0))
        d = 1
        while d < SUBLANES:
            keep = row8 >= d
            ea_prev = jnp.where(keep, pltpu.roll(ea, d, 0), 1.0)
            eh_prev = jnp.where(keep, pltpu.roll(eh, d, 0), 0.0)
            eh = eh + ea * eh_prev
            ea = ea * ea_prev
            d *= 2
        hs = []
        state = eh
        for j in range(seg):
            state = a[j] * state + u[j]
            hs.append(state)
        h = jnp.concatenate(hs, axis=0)
        hc_ref[0:1, sl] = h[bm - 1:bm, :]

        for lt in range(lane_tiles):
            for j in range(seg):
                ys_ref[lt, pl.ds(j, SUBLANES, stride=pitch), :] = (
                    h[j * SUBLANES:(j + 1) * SUBLANES, lt * LANES:(lt + 1) * LANES])
        h_t = jnp.concatenate(
            [jnp.concatenate([ys_ref[lt, sgm * pitch:sgm * pitch + seg, :] for lt in range(lane_tiles)], axis=1)
             for sgm in range(SUBLANES)], axis=0)
        o_ref[:, sl] = (h_t * _silu_of_half(za_ref[:, sl].astype(F32))).astype(o_ref.dtype)
        return carry

    lax.fori_loop(0, n_blocks, block, 0)


def _lru_branch(proj, conv_w, conv_b, w_a, w_i, b_a, b_i, lam, width, bm=512):
    s = proj.shape[0]
    n_blocks = width // LRU_BLOCK_W
    vec = lambda rows: pl.BlockSpec((rows, width), lambda i: (0, 0))
    wspec = pl.BlockSpec((n_blocks, LRU_BLOCK_W, LRU_BLOCK_W), lambda i: (0, 0, 0))
    return pl.pallas_call(
        functools.partial(_lru_kernel, bm=bm, n_blocks=n_blocks),
        grid=(s // bm,),
        in_specs=[pl.BlockSpec((bm, width), lambda i: (i, 0)),
                  pl.BlockSpec((bm, width), lambda i: (i, 1)),
                  vec(CONV_W), vec(1), wspec, wspec, vec(1), vec(1), vec(1)],
        out_specs=pl.BlockSpec((bm, width), lambda i: (i, 0)),
        out_shape=jax.ShapeDtypeStruct((s, width), BF16),
        scratch_shapes=[pltpu.VMEM((SUBLANES, width), F32),
                        pltpu.VMEM((SUBLANES, width), F32)]
        + [pltpu.VMEM((LRU_BLOCK_W // LANES, bm + SUBLANES * SUBLANES, LANES), F32)] * 2,
        compiler_params=_params(("arbitrary",)),
        name="rglru_branch",
    )(proj, proj, conv_w, conv_b, w_a, w_i, b_a, b_i, lam)


def _attn_kernel(q_ref, k0_ref, k1_ref, k2_ref, v0_ref, v1_ref, v2_ref, zb_ref, w_ref,
                 o_ref, bias_ref, s_ref, p_ref, l_ref, *, n_heads):
    bq = q_ref.shape[0]
    nk = ATT_KEY_BLOCKS * bq
    step = pl.program_id(0)

    @pl.when(step < ATT_KEY_BLOCKS)
    def _():
        qi = lax.broadcasted_iota(jnp.int32, (bq, nk), 0)
        kj = lax.broadcasted_iota(jnp.int32, (bq, nk), 1)
        q_chunk = lax.shift_right_logical(qi, CHUNK.bit_length() - 1)
        k_chunk = lax.shift_right_logical(kj, CHUNK.bit_length() - 1)
        visible = ((k_chunk >= q_chunk) & (k_chunk <= q_chunk + CTX_CHUNKS)
                   & (kj + step * bq >= (ATT_KEY_BLOCKS - 1) * bq))

        def build(h, carry):
            rows = jnp.broadcast_to(w_ref[h], (bq, w_ref.shape[2]))
            toeplitz = pltpu.roll(rows, 0, 1, stride=1, stride_axis=0)
            bias_ref[h] = jnp.where(visible, toeplitz[:, :nk], NEG_INF)
            return carry

        lax.fori_loop(0, n_heads, build, 0)

    def head_cols(h):
        return pl.ds(pl.multiple_of(h * ATT_HEAD_DIM, ATT_HEAD_DIM), ATT_HEAD_DIM)

    def window(half):
        return slice(half * ATT_HALF_Q, half * ATT_HALF_Q + ATT_HALF_KEYS)

    def half_rows(half):
        return slice(half * ATT_HALF_Q, (half + 1) * ATT_HALF_Q)

    def scores(h, carry):
        hs = head_cols(h)
        kh = jnp.concatenate([k0_ref[:, hs], k1_ref[:, hs], k2_ref[:, hs]], axis=0)
        s = lax.dot_general(q_ref[:, hs], kh, (((1,), (1,)), ((), ())),
                            preferred_element_type=F32)
        for half in range(2):
            s_ref[h, half] = s[half_rows(half), window(half)] + bias_ref[h, half_rows(half), window(half)]
        return carry

    def numerators(h, carry):
        for half in range(2):
            s = s_ref[h, half]
            m = jnp.max(s, axis=-1, keepdims=True)
            p = jnp.exp2(s - m)
            l_ref[h, half_rows(half), :] = jnp.broadcast_to(jnp.sum(p, axis=-1, keepdims=True),
                                                            (ATT_HALF_Q, l_ref.shape[2]))
            p_ref[h, half] = p.astype(BF16)
        return carry

    def values(h, carry):
        hs = head_cols(h)
        vh = jnp.concatenate([v0_ref[:, hs], v1_ref[:, hs], v2_ref[:, hs]], axis=0)
        o = jnp.concatenate([jnp.dot(p_ref[h, half], vh[window(half)], preferred_element_type=F32)
                             for half in range(2)], axis=0) / l_ref[h]
        z = zb_ref[:, hs].astype(F32)
        o_ref[:, hs] = (o * _silu_of_half(z)).astype(o_ref.dtype)
        return carry

    lax.fori_loop(0, n_heads, scores, 0, unroll=n_heads)
    lax.fori_loop(0, n_heads, numerators, 0, unroll=n_heads)
    lax.fori_loop(0, n_heads, values, 0, unroll=n_heads)


def _bias_by_offset(rel_bias):
    n_heads = rel_bias.shape[0]
    bq = ATT_BLOCK_Q
    nk = ATT_KEY_BLOCKS * bq
    off = nk - bq
    length = nk + bq
    n_far = off - REL_CLIP + 1
    n_near = nk - n_far - (2 * REL_CLIP - 1)
    far = rel_bias[:, 2 * REL_CLIP:]
    near = rel_bias[:, :1]
    w = jnp.concatenate([
        jnp.broadcast_to(far, (n_heads, n_far)),
        jnp.flip(rel_bias[:, 1:2 * REL_CLIP], axis=1),
        jnp.broadcast_to(near, (n_heads, n_near + 1)),
        jnp.broadcast_to(far, (n_heads, bq - 1)),
    ], axis=1).astype(F32)
    assert w.shape[1] == length
    return (w * LOG2_E)[:, None, :]


def _attn_branch(proj, rel_bias, att_width, col0):
    s = proj.shape[0]
    n_heads = att_width // ATT_HEAD_DIM
    bq = ATT_BLOCK_Q
    nk = ATT_KEY_BLOCKS * bq
    w = _bias_by_offset(rel_bias)

    def kv_spec(col, back):
        return pl.BlockSpec((bq, att_width), lambda i: (jnp.maximum(i - back, 0), col))

    return pl.pallas_call(
        functools.partial(_attn_kernel, n_heads=n_heads),
        grid=(s // bq,),
        in_specs=[pl.BlockSpec((bq, att_width), lambda i: (i, col0)),
                  kv_spec(col0 + 1, 2), kv_spec(col0 + 1, 1), kv_spec(col0 + 1, 0),
                  kv_spec(col0 + 2, 2), kv_spec(col0 + 2, 1), kv_spec(col0 + 2, 0),
                  pl.BlockSpec((bq, att_width), lambda i: (i, col0 + 3)),
                  pl.BlockSpec(w.shape, lambda i: (0, 0, 0))],
        out_specs=pl.BlockSpec((bq, att_width), lambda i: (i, 0)),
        out_shape=jax.ShapeDtypeStruct((s, att_width), BF16),
        scratch_shapes=[pltpu.VMEM((n_heads, bq, nk), F32),
                        pltpu.VMEM((n_heads, 2, ATT_HALF_Q, ATT_HALF_KEYS), F32),
                        pltpu.VMEM((n_heads, 2, ATT_HALF_Q, ATT_HALF_KEYS), BF16),
                        pltpu.VMEM((n_heads, bq, ATT_HEAD_DIM), F32)],
        compiler_params=_params(("arbitrary",)),
        name="chunk_attention",
    )(proj, proj, proj, proj, proj, proj, proj, proj, w)


def _merge_kernel(ya_ref, yb_ref, ca_ref, cb_ref, ga_ref, gb_ref, o_ref,
                  wa0_ref, wb0_ref, wa1_ref, wb1_ref):
    def compute(w):
        pa = jnp.dot(ya_ref[...], w[0][...], preferred_element_type=F32)
        pb = jnp.dot(yb_ref[...], w[1][...], preferred_element_type=F32)
        ga = jax.nn.sigmoid(ga_ref[...].astype(F32))
        gb = jax.nn.sigmoid(gb_ref[...].astype(F32))
        return ga * pa + gb * pb

    _stream_weights((ca_ref, cb_ref), (wa0_ref, wb0_ref), (wa1_ref, wb1_ref), o_ref, compute)


def _merge(y_a, y_b, w_pa, w_pb, proj, gate_start, bm=512, bn=1024):
    m, ka = y_a.shape
    kb = y_b.shape[1]
    n = w_pa.shape[1]
    n_tiles, row_tiles = n // bn, m // bm
    gate_col0 = gate_start // bn
    return pl.pallas_call(
        _merge_kernel,
        grid=(n_tiles + 1, row_tiles),
        in_specs=[pl.BlockSpec((bm, ka), lambda j, i: (_row_tile(j, i), 0)),
                  pl.BlockSpec((bm, kb), lambda j, i: (_row_tile(j, i), 0)),
                  _chunk_spec(ka // row_tiles, bn, n_tiles),
                  _chunk_spec(kb // row_tiles, bn, n_tiles),
                  pl.BlockSpec((bm, bn), lambda j, i: (_row_tile(j, i), gate_col0 + _prev_tile(j))),
                  pl.BlockSpec((bm, bn),
                               lambda j, i: (_row_tile(j, i), gate_col0 + n_tiles + _prev_tile(j)))],
        out_specs=pl.BlockSpec((bm, bn), lambda j, i: (_row_tile(j, i), _prev_tile(j))),
        out_shape=jax.ShapeDtypeStruct((m, n), BF16),
        scratch_shapes=[pltpu.VMEM((ka, bn), BF16), pltpu.VMEM((kb, bn), BF16),
                        pltpu.VMEM((ka, bn), BF16), pltpu.VMEM((kb, bn), BF16)],
        compiler_params=_params(("arbitrary", "arbitrary")),
        name="branch_merge",
    )(y_a, y_b, w_pa, w_pb, proj, proj)


def _rstd(v):
    return lax.rsqrt(jnp.mean(v * v, axis=-1, keepdims=True) + EPS)


def _rowwise_kernel(x_ref, t_ref, p_ref, wple_ref, gpost_ref, gpre_ref, gple_ref,
                    hn_ref, pn_ref, rt_ref):
    t = t_ref[...].astype(F32)
    rstd_t = _rstd(t)
    h = x_ref[...] + (t * rstd_t) * gpost_ref[...]
    hn_ref[...] = _rms_norm_f32(h, gpre_ref[...]).astype(hn_ref.dtype)
    pe = jnp.dot(p_ref[...].astype(BF16), wple_ref[...], preferred_element_type=F32)
    pn_ref[...] = _rms_norm_f32(pe, gple_ref[...]).astype(pn_ref.dtype)
    rt_ref[...] = jnp.broadcast_to(rstd_t, rt_ref.shape)


def _rowwise(x, t, p, w_ple, g_post, g_ple_pre, g_ple_post, bm=256):
    s, d = x.shape
    pd = p.shape[1]
    row = lambda w: pl.BlockSpec((bm, w), lambda i: (i, 0))
    vec = pl.BlockSpec((1, d), lambda i: (0, 0))
    return pl.pallas_call(
        _rowwise_kernel,
        grid=(s // bm,),
        in_specs=[row(d), row(d), row(pd), pl.BlockSpec((pd, d), lambda i: (0, 0)), vec, vec, vec],
        out_specs=[row(d), row(d), row(LANES)],
        out_shape=[jax.ShapeDtypeStruct((s, d), BF16),
                   jax.ShapeDtypeStruct((s, d), BF16),
                   jax.ShapeDtypeStruct((s, LANES), F32)],
        compiler_params=_params(("parallel",)),
        name="residual_norms",
    )(x, t, p, w_ple, g_post, g_ple_pre, g_ple_post)


def _ple_kernel(hn_ref, chunk_ref, x_ref, t_ref, pn_ref, rt_ref, gpost_ref, o_ref, wb0_ref, wb1_ref):
    def compute(w):
        g = jnp.dot(hn_ref[...], w[0][...], preferred_element_type=F32)
        h = x_ref[...] + (t_ref[...].astype(F32) * rt_ref[:, 0:1]) * gpost_ref[...]
        return h + pn_ref[...].astype(F32) * jax.nn.sigmoid(g)

    _stream_weights((chunk_ref,), (wb0_ref,), (wb1_ref,), o_ref, compute)


def _ple_gate(hn, w, x, t, pn, rstd_t, g_post, bm=512, bn=1024):
    m, k = hn.shape
    n = w.shape[1]
    n_tiles, row_tiles = n // bn, m // bm
    tile = pl.BlockSpec((bm, bn), lambda j, i: (_row_tile(j, i), _prev_tile(j)))
    rows = lambda width: pl.BlockSpec((bm, width), lambda j, i: (_row_tile(j, i), 0))
    return pl.pallas_call(
        _ple_kernel,
        grid=(n_tiles + 1, row_tiles),
        in_specs=[rows(k), _chunk_spec(k // row_tiles, bn, n_tiles),
                  tile, tile, tile, rows(LANES),
                  pl.BlockSpec((1, bn), lambda j, i: (0, _prev_tile(j)))],
        out_specs=tile,
        out_shape=jax.ShapeDtypeStruct((m, n), F32),
        scratch_shapes=[pltpu.VMEM((k, bn), BF16), pltpu.VMEM((k, bn), BF16)],
        compiler_params=_params(("arbitrary", "arbitrary")),
        name="ple_gate",
    )(hn, w, x, t, pn, rstd_t, g_post)


def _layer(x, p_i, w_in, conv_w, conv_b, w_rg_a, b_rg_a, w_rg_i, b_rg_i, lru_lambda,
           rel_bias, w_proj_a, w_proj_b, w_out, g_pre, g_post,
           w_ple, w_ple_gate, g_ple_pre, g_ple_post):
    d = x.shape[1]
    lru_width = w_proj_a.shape[0]
    att_width = w_proj_b.shape[0]
    assert lru_width == d and 2 * att_width == d
    row = lambda v: v.reshape(1, -1)

    xn = _rmsnorm(x, row(g_pre))
    q_start = 2 * lru_width
    zb_start = q_start + 3 * att_width
    col_scale = jnp.ones((1, w_in.shape[1]), F32)
    col_scale = col_scale.at[:, q_start:q_start + att_width].set(ATT_HEAD_DIM ** -0.5 * LOG2_E)
    col_scale = col_scale.at[:, lru_width:2 * lru_width].set(0.5)
    col_scale = col_scale.at[:, zb_start:zb_start + att_width].set(0.5)
    proj = _matmul(xn, w_in, "in_proj", col_scale=col_scale)
    y_a = _lru_branch(proj, conv_w, row(conv_b), (0.5 * w_rg_a).astype(BF16), (0.5 * w_rg_i).astype(BF16),
                      row(0.5 * b_rg_a), row(0.5 * b_rg_i), row(lru_lambda), lru_width)
    y_b = _attn_branch(proj, rel_bias, att_width, col0=2 * lru_width // att_width)
    merged = _merge(y_a, y_b, w_proj_a, w_proj_b, proj, gate_start=2 * lru_width + 4 * att_width)
    t = _matmul(merged, w_out, "out_proj")
    hn, pn, rstd_t = _rowwise(x, t, p_i, w_ple.astype(BF16), row(g_post), row(g_ple_pre), row(g_ple_post))
    return _ple_gate(hn, w_ple_gate, x, t, pn, rstd_t, row(g_post))


def kernel(x, p, w_in, conv_w, conv_b, w_rg_a, b_rg_a, w_rg_i, b_rg_i, lru_lambda, rel_bias,
           w_proj_a, w_proj_b, w_out, g_pre, g_post, w_ple, w_ple_gate, g_ple_pre, g_ple_post):
    batch = x.shape[0]
    outs = []
    for b in range(batch):
        h = x[b]
        for l in range(w_in.shape[0]):
            h = _layer(h, p[l, b], w_in[l], conv_w[l], conv_b[l], w_rg_a[l], b_rg_a[l],
                       w_rg_i[l], b_rg_i[l], lru_lambda[l], rel_bias[l], w_proj_a[l],
                       w_proj_b[l], w_out[l], g_pre[l], g_post[l], w_ple[l],
                       w_ple_gate[l], g_ple_pre[l], g_ple_post[l])
        outs.append(h)
    return jnp.stack(outs, axis=0)
```
